```python
import math
import jax, jax.numpy as jnp
from jax import lax
import numpy as np

D_MODEL = 2048
BATCH = 4
SEQ = 4096
DEPTH = 1

EPS = 1e-6
NEG_INF = -1e30
TINY = 1e-30
FORCE_SCORE = 1e4

NSA_HEADS = 16
NSA_KV_GROUPS = 4
NSA_REP = NSA_HEADS // NSA_KV_GROUPS
NSA_HEAD_DIM = 128
N_NSA_BRANCH = 3
CMP_BLOCK = 32
CMP_STRIDE = 16
SLC_BLOCK = 64
SLC_TOPK = 16
WINDOW = 512
NSA_QBLOCK = 32

GDN_HEADS = 16
GDN_HEAD_DIM = 128
GDN_CONV = 4
GDN_CHUNK = 64

D_FF = -(-(8 * D_MODEL) // (3 * 256)) * 256

NSA_Q_DIM = NSA_HEADS * NSA_HEAD_DIM
NSA_KV_DIM = N_NSA_BRANCH * 2 * NSA_KV_GROUPS * NSA_HEAD_DIM
NSA_GATE_DIM = NSA_HEADS * N_NSA_BRANCH
GDN_QKV_DIM = 3 * GDN_HEADS * GDN_HEAD_DIM
GDN_Z_DIM = GDN_HEADS * GDN_HEAD_DIM
IN_SPLITS = (NSA_Q_DIM, NSA_KV_DIM, NSA_GATE_DIM, GDN_QKV_DIM, GDN_Z_DIM, GDN_HEADS, GDN_HEADS, D_MODEL, D_MODEL)
D_IN = sum(IN_SPLITS)

kernel_name = "hybrid_nsa_gdn_swiglu_block"


def _rms_norm(t, gain):
    tf = t.astype(jnp.float32)
    y = tf * lax.rsqrt(jnp.mean(tf * tf, axis=-1, keepdims=True) + EPS)
    return (y * gain.astype(jnp.float32)).astype(t.dtype)


def _l2norm(t):
    tf = t.astype(jnp.float32)
    return tf * lax.rsqrt(jnp.sum(tf * tf, axis=-1, keepdims=True) + EPS)


def _masked_softmax(s, mask):
    s = jnp.where(mask, s.astype(jnp.float32), NEG_INF)
    e = jnp.where(mask, jnp.exp(s - jnp.max(s, axis=-1, keepdims=True)), 0.0)
    return e / jnp.maximum(jnp.sum(e, axis=-1, keepdims=True), TINY)


def _split_cols(t, sizes):
    out, start = [], 0
    for s in sizes:
        out.append(t[..., start:start + s])
        start += s
    return out


def _nsa_mixer(q, kv, gate_logits, q_gain, k_gain, cmp_pos, w_cmp):
    B_, S_, _ = q.shape
    G, R, dh = NSA_KV_GROUPS, NSA_REP, NSA_HEAD_DIM
    scale = dh ** -0.5
    q = _rms_norm(q.reshape(B_, S_, G, R, dh), q_gain)
    kv = kv.reshape(B_, S_, N_NSA_BRANCH, 2, G, dh)

    n_cmp = (S_ - CMP_BLOCK) // CMP_STRIDE + 1
    cmp_start = np.arange(n_cmp) * CMP_STRIDE
    blk_idx = cmp_start[:, None] + np.arange(CMP_BLOCK)[None, :]
    k_raw, v_raw = kv[:, :, 0, 0], kv[:, :, 0, 1]
    k_cmp = jnp.einsum('bnlgd,lde->bnge', k_raw[:, blk_idx] + cmp_pos[0][:, None, :], w_cmp[0])
    v_cmp = jnp.einsum('bnlgd,lde->bnge', v_raw[:, blk_idx] + cmp_pos[1][:, None, :], w_cmp[1])
    k_cmp = _rms_norm(k_cmp, k_gain[0])
    cmp_end = jnp.asarray(cmp_start + CMP_BLOCK - 1, jnp.int32)

    n_sel = S_ // SLC_BLOCK
    n_topk = min(SLC_TOPK, n_sel)
    k_sel = _rms_norm(kv[:, :, 1, 0], k_gain[1])
    k_sel_blocks = k_sel.reshape(B_, n_sel, SLC_BLOCK, G, dh).transpose(0, 3, 1, 2, 4)
    v_sel_blocks = kv[:, :, 1, 1].reshape(B_, n_sel, SLC_BLOCK, G, dh).transpose(0, 3, 1, 2, 4)
    sel_start = np.arange(n_sel) * SLC_BLOCK
    overlap = np.minimum(cmp_start[:, None] + CMP_BLOCK, sel_start[None, :] + SLC_BLOCK) - np.maximum(cmp_start[:, None], sel_start[None, :])
    agg = jnp.asarray(np.clip(overlap, 0, None) / CMP_BLOCK, jnp.float32)
    b_ix = jnp.arange(B_)[:, None, None, None]
    g_ix = jnp.arange(G)[None, None, :, None]

    k_win = jnp.pad(_rms_norm(kv[:, :, 2, 0], k_gain[2]), ((0, 0), (WINDOW, 0), (0, 0), (0, 0)))
    v_win = jnp.pad(kv[:, :, 2, 1], ((0, 0), (WINDOW, 0), (0, 0), (0, 0)))

    gates = jax.nn.sigmoid(gate_logits.astype(jnp.float32)).reshape(B_, S_, G, R, N_NSA_BRANCH)
    QC = NSA_QBLOCK
    nq = S_ // QC

    def block(args):
        qc, gc, start = args
        t = start + jnp.arange(QC, dtype=jnp.int32)
        s = jnp.einsum('bqgrd,bngd->bqgrn', qc, k_cmp) * scale
        p_cmp = _masked_softmax(s, (cmp_end[None, :] <= t[:, None])[None, :, None, None, :])
        o_cmp = jnp.einsum('bqgrn,bngd->bqgrd', p_cmp, v_cmp)
        imp = jnp.einsum('bqgrn,nj->bqgj', p_cmp, agg)
        blk = jnp.arange(n_sel, dtype=jnp.int32)[None, :]
        cur = t[:, None] // SLC_BLOCK
        forced = (blk == 0) | (blk == cur) | (blk == cur - 1)
        causal = blk * SLC_BLOCK <= t[:, None]
        score = jnp.where(forced[None, :, None, :], FORCE_SCORE, jnp.where(causal[None, :, None, :], imp, -FORCE_SCORE))
        _, sel = lax.top_k(score, n_topk)
        ks = k_sel_blocks[b_ix, g_ix, sel]
        vs = v_sel_blocks[b_ix, g_ix, sel]
        key_pos = sel[..., None] * SLC_BLOCK + jnp.arange(SLC_BLOCK, dtype=jnp.int32)
        m_sel = (key_pos <= t[None, :, None, None, None])[:, :, :, None].reshape(B_, QC, G, 1, n_topk * SLC_BLOCK)
        s = jnp.einsum('bqgrd,bqgkld->bqgrkl', qc, ks).reshape(B_, QC, G, R, n_topk * SLC_BLOCK) * scale
        p = _masked_softmax(s, m_sel).reshape(B_, QC, G, R, n_topk, SLC_BLOCK)
        o_slc = jnp.einsum('bqgrkl,bqgkld->bqgrd', p, vs)
        kw = lax.dynamic_slice_in_dim(k_win, start, WINDOW + QC, axis=1)
        vw = lax.dynamic_slice_in_dim(v_win, start, WINDOW + QC, axis=1)
        pos = start - WINDOW + jnp.arange(WINDOW + QC, dtype=jnp.int32)
        rel = t[:, None] - pos[None, :]
        m_win = ((pos[None, :] >= 0) & (rel >= 0) & (rel < WINDOW))[None, :, None, None, :]
        s = jnp.einsum('bqgrd,bkgd->bqgrk', qc, kw) * scale
        o_win = jnp.einsum('bqgrk,bkgd->bqgrd', _masked_softmax(s, m_win), vw)
        o = gc[..., 0:1] * o_cmp + gc[..., 1:2] * o_slc + gc[..., 2:3] * o_win
        return o.astype(qc.dtype)

    q_chunks = jnp.moveaxis(q.reshape(B_, nq, QC, G, R, dh), 1, 0)
    g_chunks = jnp.moveaxis(gates.reshape(B_, nq, QC, G, R, N_NSA_BRANCH), 1, 0)
    starts = jnp.arange(nq, dtype=jnp.int32) * QC
    out = lax.map(block, (q_chunks, g_chunks, starts))
    return jnp.moveaxis(out, 0, 1).reshape(B_, S_, G * R * dh)


def _gated_delta_chunked(q, k, v, g, beta):
    B_, S_, H, dk = q.shape
    dv = v.shape[-1]
    C = GDN_CHUNK
    N = S_ // C
    out_dtype = v.dtype

    def chunks(t):
        return jnp.moveaxis(t.astype(jnp.float32).reshape(B_, N, C, H, -1), 3, 2)

    q, k, v = chunks(q), chunks(k), chunks(v)
    g = jnp.cumsum(chunks(g[..., None])[..., 0], axis=-1)
    beta = chunks(beta[..., None])[..., 0]
    k_beta = k * beta[..., None]
    v_beta = v * beta[..., None]
    tril = np.tril(np.ones((C, C), bool))
    strict = np.tril(np.ones((C, C), bool), -1)
    diff = g[..., :, None] - g[..., None, :]
    decay = jnp.where(tril, jnp.exp(jnp.where(tril, diff, 0.0)), 0.0)
    a_mat = jnp.eye(C, dtype=jnp.float32) + jnp.where(strict, jnp.einsum('bnhid,bnhjd->bnhij', k_beta, k) * decay, 0.0)
    rhs = jnp.concatenate([v_beta, k_beta * jnp.exp(g)[..., None]], axis=-1)
    sol = lax.linalg.triangular_solve(a_mat, rhs, left_side=True, lower=True, unit_diagonal=True)
    u, w = sol[..., :dv], sol[..., dv:]
    qk = jnp.einsum('bnhid,bnhjd->bnhij', q, k) * decay

    def step(state, xs):
        q_c, k_c, u_c, w_c, g_c, qk_c = xs
        v_new = u_c - jnp.einsum('bhcd,bhde->bhce', w_c, state)
        o = jnp.einsum('bhcd,bhde->bhce', q_c * jnp.exp(g_c)[..., None], state) + jnp.einsum('bhij,bhje->bhie', qk_c, v_new)
        g_last = g_c[..., -1:]
        state = state * jnp.exp(g_last)[..., None] + jnp.einsum('bhcd,bhce->bhde', k_c * jnp.exp(g_last - g_c)[..., None], v_new)
        return state, o

    xs = tuple(jnp.moveaxis(t, 1, 0) for t in (q, k, u, w, g, qk))
    state0 = jnp.zeros((B_, H, dk, dv), jnp.float32)
    _, o = lax.scan(step, state0, xs)
    o = jnp.transpose(o, (1, 0, 3, 2, 4)).reshape(B_, S_, H, dv)
    return o.astype(out_dtype)


def _gdn_mixer(qkv, z, a, b, conv_w, a_log, dt_bias, out_gain):
    B_, S_, _ = qkv.shape
    H, d = GDN_HEADS, GDN_HEAD_DIM
    pad = jnp.pad(qkv, ((0, 0), (GDN_CONV - 1, 0), (0, 0)))
    conv = conv_w[0] * pad[:, 0:S_]
    for i in range(1, GDN_CONV):
        conv = conv + conv_w[i] * pad[:, i:i + S_]
    conv = jax.nn.silu(conv)
    q, k, v = jnp.split(conv, 3, axis=-1)
    q = _l2norm(q.reshape(B_, S_, H, d)) * (d ** -0.5)
    k = _l2norm(k.reshape(B_, S_, H, d))
    v = v.reshape(B_, S_, H, d)
    beta = jax.nn.sigmoid(b.astype(jnp.float32))
    g = -jnp.exp(a_log.astype(jnp.float32)) * jax.nn.softplus(a.astype(jnp.float32) + dt_bias.astype(jnp.float32))
    o = _gated_delta_chunked(q, k, v, g, beta)
    o = _rms_norm(o, out_gain) * jax.nn.silu(z.reshape(B_, S_, H, d))
    return o.reshape(B_, S_, H * d)


def setup_inputs(seed: int = 0) -> dict:
    key = jax.random.key(seed)
    ks = jax.random.split(key, 20)
    L, dh = DEPTH, NSA_HEAD_DIM

    def nrm(k, shape, scale):
        return jax.random.normal(k, shape, jnp.float32) * scale

    def gain(k, shape):
        return 1.0 + 0.02 * jax.random.normal(k, shape, jnp.float32)

    dt = jnp.exp(jax.random.uniform(ks[9], (L, GDN_HEADS), jnp.float32, minval=math.log(1e-3), maxval=math.log(1e-1)))
    return {
        "x": nrm(ks[0], (BATCH, SEQ, D_MODEL), 1.0),
        "attn_norm": gain(ks[1], (L, D_MODEL)),
        "w_in": nrm(ks[2], (L, D_MODEL, D_IN), D_MODEL ** -0.5),
        "nsa_q_norm": gain(ks[3], (L, dh)),
        "nsa_k_norm": gain(ks[4], (L, N_NSA_BRANCH, dh)),
        "cmp_pos": nrm(ks[5], (L, 2, CMP_BLOCK, dh), 0.1),
        "w_cmp": nrm(ks[6], (L, 2, CMP_BLOCK, dh, dh), (CMP_BLOCK * dh) ** -0.5),
        "gdn_conv": nrm(ks[7], (L, GDN_CONV, GDN_QKV_DIM), GDN_CONV ** -0.5),
        "gdn_a_log": jnp.log(jax.random.uniform(ks[8], (L, GDN_HEADS), jnp.float32, minval=1.0, maxval=16.0)),
        "gdn_dt_bias": dt + jnp.log(-jnp.expm1(-dt)),
        "gdn_out_norm": gain(ks[10], (L, GDN_HEAD_DIM)),
        "w_branch_a": nrm(ks[11], (L, NSA_Q_DIM, D_MODEL), NSA_Q_DIM ** -0.5),
        "w_branch_b": nrm(ks[12], (L, GDN_Z_DIM, D_MODEL), GDN_Z_DIM ** -0.5),
        "w_out": nrm(ks[13], (L, D_MODEL, D_MODEL), D_MODEL ** -0.5),
        "ffn_norm": gain(ks[14], (L, D_MODEL)),
        "w_gate": nrm(ks[15], (L, D_MODEL, D_FF), D_MODEL ** -0.5),
        "w_up": nrm(ks[16], (L, D_MODEL, D_FF), D_MODEL ** -0.5),
        "w_down": nrm(ks[17], (L, D_FF, D_MODEL), D_FF ** -0.5),
    }


def reference(x, attn_norm, w_in, nsa_q_norm, nsa_k_norm, cmp_pos, w_cmp, gdn_conv, gdn_a_log, gdn_dt_bias,
              gdn_out_norm, w_branch_a, w_branch_b, w_out, ffn_norm, w_gate, w_up, w_down):
    for l in range(DEPTH):
        h = _rms_norm(x, attn_norm[l])
        proj = h @ w_in[l]
        q_a, kv_a, gate_a, qkv_b, z_b, a_b, b_b, m_a, m_b = _split_cols(proj, IN_SPLITS)
        o_a = _nsa_mixer(q_a, kv_a, gate_a, nsa_q_norm[l], nsa_k_norm[l], cmp_pos[l], w_cmp[l])
        o_b = _gdn_mixer(qkv_b, z_b, a_b, b_b, gdn_conv[l], gdn_a_log[l], gdn_dt_bias[l], gdn_out_norm[l])
        mix = jax.nn.sigmoid(m_a) * (o_a @ w_branch_a[l]) + jax.nn.sigmoid(m_b) * (o_b @ w_branch_b[l])
        x = x + mix @ w_out[l]
        h2 = _rms_norm(x, ffn_norm[l])
        x = x + (jax.nn.silu(h2 @ w_gate[l]) * (h2 @ w_up[l])) @ w_down[l]
    return x
```

```python
import functools

import numpy as np
import jax
import jax.numpy as jnp
from jax import lax
from jax.experimental import pallas as pl
from jax.experimental.pallas import tpu as pltpu

F32 = jnp.float32
BF16 = jnp.bfloat16

D_MODEL = 2048
EPS = 1e-6
NEG_INF = -1e30
TINY = 1e-30
FORCE_SCORE = 1e4

NSA_HEADS = 16
NSA_GROUPS = 4
NSA_REP = NSA_HEADS // NSA_GROUPS
HEAD_DIM = 128
N_BRANCH = 3
CMP_BLOCK = 32
CMP_STRIDE = 16
SLC_BLOCK = 64
SLC_TOPK = 16
WINDOW = 512

GDN_HEADS = 16
GDN_CONV = 4
GDN_CHUNK = 64

D_FF = 5632

NSA_Q_DIM = NSA_HEADS * HEAD_DIM
NSA_KV_DIM = N_BRANCH * 2 * NSA_GROUPS * HEAD_DIM
GDN_DIM = GDN_HEADS * HEAD_DIM
OFF_Q = 0
OFF_KV = OFF_Q + NSA_Q_DIM
OFF_GQKV = OFF_KV + NSA_KV_DIM
OFF_Z = OFF_GQKV + 3 * GDN_DIM
OFF_MA = OFF_Z + GDN_DIM
OFF_MB = OFF_MA + D_MODEL
NP_BIG = OFF_MB + D_MODEL

GDN_HB = 4
GDN_NHG = GDN_HEADS // GDN_HB
NP_SMALL = (NSA_GROUPS + GDN_NHG) * 128

VMEM_LIMIT = 56 * 1024 * 1024

NT_DIMS = (((1,), (1,)), ((), ()))


def _dot(a, b):
    return jnp.dot(a, b, preferred_element_type=F32)


def _dot_nt(a, b):
    return lax.dot_general(a, b, NT_DIMS, preferred_element_type=F32)


def _cparams(sem):
    return pltpu.CompilerParams(dimension_semantics=sem, vmem_limit_bytes=VMEM_LIMIT)


def _in_proj_kernel(x_ref, g_ref, w_ref, ws_ref, o_ref, os_ref, hn_ref):
    @pl.when(pl.program_id(1) == 0)
    def _():
        x = x_ref[...]
        y = x * lax.rsqrt(jnp.mean(x * x, axis=-1, keepdims=True) + EPS) * g_ref[...]
        hn = y.astype(BF16)
        hn_ref[...] = hn
        os_ref[...] = _dot(hn, ws_ref[...])

    o_ref[...] = _dot(hn_ref[...], w_ref[...]).astype(o_ref.dtype)


def _in_proj(x2, gain, w_big, w_small, bm=1024, bn=512):
    t, d = x2.shape
    n = w_big.shape[1]
    ns = w_small.shape[1]
    return pl.pallas_call(
        _in_proj_kernel,
        out_shape=(jax.ShapeDtypeStruct((t, n), BF16), jax.ShapeDtypeStruct((t, ns), F32)),
        grid=(t // bm, n // bn),
        in_specs=[
            pl.BlockSpec((bm, d), lambda i, j: (i, 0)),
            pl.BlockSpec((1, d), lambda i, j: (0, 0)),
            pl.BlockSpec((d, bn), lambda i, j: (0, j)),
            pl.BlockSpec((d, ns), lambda i, j: (0, 0)),
        ],
        out_specs=(
            pl.BlockSpec((bm, bn), lambda i, j: (i, j)),
            pl.BlockSpec((bm, ns), lambda i, j: (i, 0)),
        ),
        scratch_shapes=[pltpu.VMEM((bm, d), BF16)],
        compiler_params=_cparams(("parallel", "arbitrary")),
        name="in_proj",
    )(x2, gain, w_big, w_small)


def _cmp_kernel(ck_ref, cv_ref, w_ref, pos_ref, kg_ref, kc_ref, vc_ref):
    half = CMP_STRIDE * HEAD_DIM

    def compress(c_ref, idx):
        c = c_ref[...]
        w = w_ref[idx]
        lo = _dot(c, w[:half])
        hi = _dot(c, w[half:])
        bias = _dot(pos_ref[idx], w)[0:1]
        n = lo.shape[0]
        return lo + pltpu.roll(hi, n - 1, 0) + bias

    kc = compress(ck_ref, 0)
    kc = kc * lax.rsqrt(jnp.mean(kc * kc, axis=-1, keepdims=True) + EPS) * kg_ref[...]
    kc_ref[...] = kc.astype(BF16)
    vc_ref[...] = compress(cv_ref, 1).astype(BF16)


def _nsa_compress(ckv, w_cmp2, pos2, k_gain0):
    b, _, nc, width = ckv.shape
    g = NSA_GROUPS
    out = jax.ShapeDtypeStruct((b, g, nc, HEAD_DIM), BF16)
    return pl.pallas_call(
        _cmp_kernel,
        out_shape=(out, out),
        grid=(b, g),
        in_specs=[
            pl.BlockSpec((None, None, nc, width), lambda i, j: (i, j, 0, 0)),
            pl.BlockSpec((None, None, nc, width), lambda i, j: (i, j + NSA_GROUPS, 0, 0)),
            pl.BlockSpec((2, 2 * width, HEAD_DIM), lambda i, j: (0, 0, 0)),
            pl.BlockSpec((2, 8, 2 * width), lambda i, j: (0, 0, 0)),
            pl.BlockSpec((1, HEAD_DIM), lambda i, j: (0, 0)),
        ],
        out_specs=(
            pl.BlockSpec((None, None, nc, HEAD_DIM), lambda i, j: (i, j, 0, 0)),
            pl.BlockSpec((None, None, nc, HEAD_DIM), lambda i, j: (i, j, 0, 0)),
        ),
        compiler_params=_cparams(("parallel", "parallel")),
        name="nsa_compress",
    )(ckv, ckv, w_cmp2, pos2, k_gain0)


NSA_TQ = 128
NSA_TK = 256
NSA_TW = 128
NORM_ROWS = 512


def _rms(xf, gain):
    return xf * lax.rsqrt(jnp.mean(xf * xf, axis=-1, keepdims=True) + EPS) * gain


def _flash_step(qs, k, v, mask, carry):
    m, l, acc = carry
    s = _dot_nt(qs, k)
    s = jnp.where(mask, s, NEG_INF)
    m_new = jnp.maximum(m, jnp.max(s, axis=-1, keepdims=True))
    p = jnp.where(mask, jnp.exp(s - m_new), 0.0)
    alpha = jnp.exp(m - m_new)
    l_new = alpha * l + jnp.sum(p, axis=-1, keepdims=True)
    acc_new = alpha * acc + _dot(p.astype(BF16), v)
    return m_new, l_new, acc_new


def _nsa_kernel(q_ref, gl_ref, ks_ref, vs_ref, kw_ref, vw_ref, kc_ref, vc_ref, aggt_ref, exp_ref,
                eye_ref, qg_ref, kg_ref, o_ref, ksn_ref, kwn_ref, sc_ref, *, seq):
    tq, tk, tw, rep, dh = NSA_TQ, NSA_TK, NSA_TW, NSA_REP, HEAD_DIM
    rows = rep * tq
    n_sel = seq // SLC_BLOCK
    n_topk = min(SLC_TOPK, n_sel)
    qi = pl.program_id(2)
    q0 = qi * tq

    @pl.when(qi == 0)
    def _():
        def body(c, carry):
            r = pl.ds(pl.multiple_of(c * NORM_ROWS, NORM_ROWS), NORM_ROWS)
            ksn_ref[r, :] = _rms(ks_ref[r, :].astype(F32), kg_ref[1:2, :]).astype(BF16)
            kwn_ref[r, :] = _rms(kw_ref[r, :].astype(F32), kg_ref[2:3, :]).astype(BF16)
            return carry
        lax.fori_loop(0, seq // NORM_ROWS, body, 0)

    scale = dh ** -0.5
    q = q_ref[...].astype(F32)
    qs = jnp.concatenate(
        [(_rms(q[:, r * dh:(r + 1) * dh], qg_ref[...]) * scale).astype(BF16) for r in range(rep)], axis=0)

    n_cmp = kc_ref.shape[0]
    s = _dot_nt(qs, kc_ref[...])
    t_c = q0 + (lax.broadcasted_iota(jnp.int32, (rows, n_cmp), 0) & (tq - 1))
    cmp_end = lax.broadcasted_iota(jnp.int32, (rows, n_cmp), 1) * CMP_STRIDE + (CMP_BLOCK - 1)
    mask = cmp_end <= t_c
    s = jnp.where(mask, s, NEG_INF)
    e = jnp.where(mask, jnp.exp(s - jnp.max(s, axis=-1, keepdims=True)), 0.0)
    p = e / jnp.maximum(jnp.sum(e, axis=-1, keepdims=True), TINY)
    o_cmp = _dot(p.astype(BF16), vc_ref[...])
    p_sum = p[0:tq]
    for r in range(1, rep):
        p_sum = p_sum + p[r * tq:(r + 1) * tq]
    imp_t = _dot_nt(aggt_ref[...], p_sum.astype(BF16))

    jidx = lax.broadcasted_iota(jnp.int32, (n_sel, tq), 0)
    t_l = q0 + lax.broadcasted_iota(jnp.int32, (n_sel, tq), 1)
    cur = t_l // SLC_BLOCK
    forced = (jidx == 0) | (jidx == cur) | (jidx == cur - 1)
    causal = jidx * SLC_BLOCK <= t_l
    score = jnp.where(forced, FORCE_SCORE, jnp.where(causal, imp_t, -FORCE_SCORE))
    sc_ref[...] = score

    def rank_body(i, rank):
        row = sc_ref[pl.ds(i, 1), :]
        ge = jnp.where(row >= score, 1.0, 0.0)
        gt = jnp.where(row > score, 1.0, 0.0)
        return rank + jnp.where(jidx > i, ge, gt)

    n_blk = (q0 + tq - 1) // SLC_BLOCK + 1
    rank = lax.fori_loop(0, n_blk, rank_body, jnp.zeros((n_sel, tq), F32))
    sel_t = jnp.where(rank < n_topk, 1.0, 0.0).astype(BF16)
    sel = _dot_nt(eye_ref[...], sel_t).astype(BF16)

    def init():
        return (jnp.full((rows, 1), NEG_INF, F32), jnp.zeros((rows, 1), F32), jnp.zeros((rows, dh), F32))

    def sel_body(kt, carry):
        k0 = pl.multiple_of(kt * tk, tk)
        member = _dot(sel, exp_ref[:, pl.ds(k0, tk)])
        member = jnp.concatenate([member] * rep, axis=0)
        t_r = q0 + (lax.broadcasted_iota(jnp.int32, (rows, tk), 0) & (tq - 1))
        kpos = k0 + lax.broadcasted_iota(jnp.int32, (rows, tk), 1)
        msk = (member > 0.5) & (kpos <= t_r)
        return _flash_step(qs, ksn_ref[pl.ds(k0, tk), :], vs_ref[pl.ds(k0, tk), :], msk, carry)

    _, l_s, acc_s = lax.fori_loop(0, q0 // tk + 1, sel_body, init())
    o_slc = acc_s / jnp.maximum(l_s, TINY)

    n_wt = WINDOW // tw + tq // tw

    def win_body(w, carry):
        k0 = pl.multiple_of(q0 - WINDOW + w * tw, tw)
        t_r = q0 + (lax.broadcasted_iota(jnp.int32, (rows, tw), 0) & (tq - 1))
        rel = t_r - (k0 + lax.broadcasted_iota(jnp.int32, (rows, tw), 1))
        msk = (rel >= 0) & (rel < WINDOW)
        return _flash_step(qs, kwn_ref[pl.ds(k0, tw), :], vw_ref[pl.ds(k0, tw), :], msk, carry)

    w_start = jnp.maximum(0, (WINDOW - q0) // tw)
    _, l_w, acc_w = lax.fori_loop(w_start, n_wt, win_body, init())
    o_win = acc_w / jnp.maximum(l_w, TINY)

    gates = jax.nn.sigmoid(gl_ref[...])
    for r in range(rep):
        rs = slice(r * tq, (r + 1) * tq)
        o = (gates[:, 3 * r:3 * r + 1] * o_cmp[rs] + gates[:, 3 * r + 1:3 * r + 2] * o_slc[rs]
             + gates[:, 3 * r + 2:3 * r + 3] * o_win[rs])
        o_ref[:, r * dh:(r + 1) * dh] = o.astype(o_ref.dtype)


def _nsa_attention(p3, ps3, kc, vc, agg_t, expand, eye, q_gain, k_gain):
    b, seq, _ = p3.shape
    g, tq, dh = NSA_GROUPS, NSA_TQ, HEAD_DIM
    n_sel = seq // SLC_BLOCK
    nc = kc.shape[2]
    kvb = OFF_KV // dh

    def kv_spec(branch, is_v):
        base = kvb + (branch * 2 + is_v) * g
        return pl.BlockSpec((None, seq, dh), lambda i, j, k: (i, 0, base + j))

    qw = NSA_REP * dh
    return pl.pallas_call(
        functools.partial(_nsa_kernel, seq=seq),
        out_shape=jax.ShapeDtypeStruct((b, seq, NSA_Q_DIM), BF16),
        grid=(b, g, seq // tq),
        in_specs=[
            pl.BlockSpec((None, tq, qw), lambda i, j, k: (i, k, j)),
            pl.BlockSpec((None, tq, 128), lambda i, j, k: (i, k, j)),
            kv_spec(1, 0), kv_spec(1, 1), kv_spec(2, 0), kv_spec(2, 1),
            pl.BlockSpec((None, None, nc, dh), lambda i, j, k: (i, j, 0, 0)),
            pl.BlockSpec((None, None, nc, dh), lambda i, j, k: (i, j, 0, 0)),
            pl.BlockSpec((n_sel, nc), lambda i, j, k: (0, 0)),
            pl.BlockSpec((n_sel, seq), lambda i, j, k: (0, 0)),
            pl.BlockSpec((tq, tq), lambda i, j, k: (0, 0)),
            pl.BlockSpec((1, dh), lambda i, j, k: (0, 0)),
            pl.BlockSpec((N_BRANCH, dh), lambda i, j, k: (0, 0)),
        ],
        out_specs=pl.BlockSpec((None, tq, qw), lambda i, j, k: (i, k, j)),
        scratch_shapes=[
            pltpu.VMEM((seq, dh), BF16),
            pltpu.VMEM((seq, dh), BF16),
            pltpu.VMEM((n_sel, tq), F32),
        ],
        compiler_params=_cparams(("parallel", "parallel", "arbitrary")),
        name="nsa_attention",
    )(p3, ps3, p3, p3, p3, p3, kc, vc, agg_t, expand, eye, q_gain, k_gain)


GDN_TS = 256
GDN_INV = 128
HALO = 8


def _split3(x):
    hi = x.astype(BF16)
    r1 = x - hi.astype(F32)
    mid = r1.astype(BF16)
    lo = (r1 - mid.astype(F32)).astype(BF16)
    return hi, mid, lo


def _gdn_kernel(q_ref, k_ref, v_ref, z_ref, ab_ref, cw_ref, alog_ref, dtb_ref, og_ref, ltri_ref, lall_ref,
                eye_ref, o_ref, xe_ref, st_ref):
    ts, hb, dh, ch = GDN_TS, GDN_HB, HEAD_DIM, GDN_CHUNK
    ti = pl.program_id(2)

    @pl.when(ti == 0)
    def _():
        xe_ref[:, 0:HALO, :] = jnp.zeros((3, HALO, hb * dh), F32)
        st_ref[...] = jnp.zeros_like(st_ref)

    conv = []
    for part, ref in enumerate((q_ref, k_ref, v_ref)):
        xe_ref[part, HALO:HALO + ts, :] = ref[...].astype(F32)
        acc = None
        for i in range(GDN_CONV):
            d = GDN_CONV - 1 - i
            term = cw_ref[part, i:i + 1, :] * xe_ref[part, HALO - d:HALO - d + ts, :]
            acc = term if acc is None else acc + term
        conv.append(acc * jax.nn.sigmoid(acc))
        xe_ref[part, 0:HALO, :] = xe_ref[part, ts:ts + HALO, :]

    ab = ab_ref[...]
    xg = ab + dtb_ref[...]
    softplus = jnp.maximum(xg, 0.0) + jnp.log(1.0 + jnp.exp(-jnp.abs(xg)))
    g_tok = -jnp.exp(alog_ref[...]) * softplus
    beta = jax.nn.sigmoid(ab)
    parts = _split3(g_tok)
    ltri, lall = ltri_ref[...], lall_ref[...]
    g_cum = _dot(ltri, parts[0]) + _dot(ltri, parts[1]) + _dot(ltri, parts[2])
    g_end = _dot(lall, parts[0]) + _dot(lall, parts[1]) + _dot(lall, parts[2])

    ri = lax.broadcasted_iota(jnp.int32, (ts, ts), 0)
    ci = lax.broadcasted_iota(jnp.int32, (ts, ts), 1)
    same = (ri // ch) == (ci // ch)
    tril = same & (ri >= ci)
    strict = same & (ri > ci)
    lane = lax.broadcasted_iota(jnp.int32, (ts, dh), 1)
    ri_b = lax.broadcasted_iota(jnp.int32, (GDN_INV, GDN_INV), 0)
    ci_b = lax.broadcasted_iota(jnp.int32, (GDN_INV, GDN_INV), 1)
    ident = jnp.where(ri_b == ci_b, 1.0, 0.0)
    eye_bf = eye_ref[...]

    for hh in range(hb):
        cs = slice(hh * dh, (hh + 1) * dh)
        qh, kh, vh = conv[0][:, cs], conv[1][:, cs], conv[2][:, cs]
        qh = qh * lax.rsqrt(jnp.sum(qh * qh, axis=-1, keepdims=True) + EPS) * (dh ** -0.5)
        kh = kh * lax.rsqrt(jnp.sum(kh * kh, axis=-1, keepdims=True) + EPS)
        gc = g_cum[:, hh:hh + 1]
        ge = g_end[:, hh:hh + 1]
        bt = beta[:, hb + hh:hb + hh + 1]

        hi, mid, lo = (t.astype(F32) for t in _split3(gc))
        xmat = jnp.where(lane == 0, hi, jnp.where(lane == 1, mid, jnp.where(lane == 2, lo,
                         jnp.where(lane < 6, 1.0, 0.0))))
        ymat = jnp.where(lane < 3, 1.0, jnp.where(lane == 3, -hi, jnp.where(lane == 4, -mid,
                         jnp.where(lane == 5, -lo, 0.0))))
        diff = _dot_nt(xmat.astype(BF16), ymat.astype(BF16))
        decay = jnp.where(tril, jnp.exp(jnp.where(tril, diff, 0.0)), 0.0)

        kb = kh * bt
        k_bf = kh.astype(BF16)
        a_mat = jnp.where(strict, _dot_nt(kb.astype(BF16), k_bf) * decay, 0.0)
        qk = (_dot_nt(qh.astype(BF16), k_bf) * decay).astype(BF16)
        vb = (vh * bt).astype(BF16)
        kbg = (kb * jnp.exp(gc)).astype(BF16)

        u_parts, w_parts = [], []
        for blk in range(ts // GDN_INV):
            bs = slice(blk * GDN_INV, (blk + 1) * GDN_INV)
            pw = -a_mat[bs, bs]
            tinv = ident + pw
            for _ in range(5):
                pw_bf = pw.astype(BF16)
                pw = _dot(pw_bf, pw_bf)
                tinv = tinv + _dot(tinv.astype(BF16), pw.astype(BF16))
            t_bf = tinv.astype(BF16)
            u_parts.append(_dot(t_bf, vb[bs]))
            w_parts.append(_dot(t_bf, kbg[bs]))
        u = jnp.concatenate(u_parts, axis=0)
        w = jnp.concatenate(w_parts, axis=0).astype(BF16)
        qg = (qh * jnp.exp(gc)).astype(BF16)
        kd = (kh * jnp.exp(ge - gc)).astype(BF16)

        state = st_ref[hh]
        v_new, o_inter = [], []
        for c in range(ts // ch):
            rs = slice(c * ch, (c + 1) * ch)
            ws = _dot(jnp.concatenate([w[rs], qg[rs]], axis=0), state.astype(BF16))
            vn = u[rs] - ws[:ch]
            o_inter.append(ws[ch:])
            v_new.append(vn)
            kd_t = _dot_nt(eye_bf, kd[rs]).astype(BF16)
            state = state * jnp.exp(ge[c * ch:c * ch + 1, :]) + _dot(kd_t, vn.astype(BF16))
        st_ref[hh] = state
        o = jnp.concatenate(o_inter, axis=0) + _dot(qk, jnp.concatenate(v_new, axis=0).astype(BF16))

        o = o * lax.rsqrt(jnp.mean(o * o, axis=-1, keepdims=True) + EPS) * og_ref[...]
        z = z_ref[:, cs].astype(F32)
        o_ref[:, cs] = (o * (z * jax.nn.sigmoid(z))).astype(o_ref.dtype)


def _gdn(p3, ps3, conv_w3, alog_l, dtb_l, out_gain, ltri, lall, eye):
    b, seq, _ = p3.shape
    ts, hb, dh = GDN_TS, GDN_HB, HEAD_DIM
    wb = hb * dh
    qb, zb = OFF_GQKV // wb, OFF_Z // wb
    gw = GDN_DIM // wb
    return pl.pallas_call(
        _gdn_kernel,
        out_shape=jax.ShapeDtypeStruct((b, seq, GDN_DIM), BF16),
        grid=(b, GDN_NHG, seq // ts),
        in_specs=[
            pl.BlockSpec((None, ts, wb), lambda i, j, k: (i, k, qb + j)),
            pl.BlockSpec((None, ts, wb), lambda i, j, k: (i, k, qb + gw + j)),
            pl.BlockSpec((None, ts, wb), lambda i, j, k: (i, k, qb + 2 * gw + j)),
            pl.BlockSpec((None, ts, wb), lambda i, j, k: (i, k, zb + j)),
            pl.BlockSpec((None, ts, 128), lambda i, j, k: (i, k, NSA_GROUPS + j)),
            pl.BlockSpec((3, GDN_CONV, wb), lambda i, j, k: (0, 0, j)),
            pl.BlockSpec((None, 1, 128), lambda i, j, k: (j, 0, 0)),
            pl.BlockSpec((None, 1, 128), lambda i, j, k: (j, 0, 0)),
            pl.BlockSpec((1, dh), lambda i, j, k: (0, 0)),
            pl.BlockSpec((ts, ts), lambda i, j, k: (0, 0)),
            pl.BlockSpec((ts, ts), lambda i, j, k: (0, 0)),
            pl.BlockSpec((dh, dh), lambda i, j, k: (0, 0)),
        ],
        out_specs=pl.BlockSpec((None, ts, wb), lambda i, j, k: (i, k, j)),
        scratch_shapes=[
            pltpu.VMEM((3, HALO + ts, wb), F32),
            pltpu.VMEM((hb, dh, dh), F32),
        ],
        compiler_params=_cparams(("parallel", "parallel", "arbitrary")),
        name="gdn",
    )(p3, p3, p3, p3, ps3, conv_w3, alog_l, dtb_l, out_gain, ltri, lall, eye)


def _merge_kernel(oa_ref, ob_ref, wa_ref, wb_ref, ma_ref, mb_ref, o_ref):
    ya = _dot(oa_ref[...], wa_ref[...])
    yb = _dot(ob_ref[...], wb_ref[...])
    mix = jax.nn.sigmoid(ma_ref[...].astype(F32)) * ya + jax.nn.sigmoid(mb_ref[...].astype(F32)) * yb
    o_ref[...] = mix.astype(o_ref.dtype)


def _merge(oa, ob, wa, wb, p2, bm=512, bn=1024):
    t, d = oa.shape
    n = wa.shape[1]
    ma_b, mb_b = OFF_MA // bn, OFF_MB // bn
    return pl.pallas_call(
        _merge_kernel,
        out_shape=jax.ShapeDtypeStruct((t, n), BF16),
        grid=(n // bn, t // bm),
        in_specs=[
            pl.BlockSpec((bm, d), lambda j, i: (i, 0)),
            pl.BlockSpec((bm, d), lambda j, i: (i, 0)),
            pl.BlockSpec((d, bn), lambda j, i: (0, j)),
            pl.BlockSpec((d, bn), lambda j, i: (0, j)),
            pl.BlockSpec((bm, bn), lambda j, i: (i, ma_b + j)),
            pl.BlockSpec((bm, bn), lambda j, i: (i, mb_b + j)),
        ],
        out_specs=pl.BlockSpec((bm, bn), lambda j, i: (i, j)),
        compiler_params=_cparams(("parallel", "parallel")),
        name="merge",
    )(oa, ob, wa, wb, p2, p2)


def _resid_matmul_kernel(a_ref, w_ref, r_ref, o_ref):
    o_ref[...] = r_ref[...] + _dot(a_ref[...], w_ref[...])


def _resid_matmul(a, w, resid, bm, bn):
    t, k = a.shape
    n = w.shape[1]
    return pl.pallas_call(
        _resid_matmul_kernel,
        out_shape=jax.ShapeDtypeStruct((t, n), F32),
        grid=(n // bn, t // bm),
        in_specs=[
            pl.BlockSpec((bm, k), lambda j, i: (i, 0)),
            pl.BlockSpec((k, bn), lambda j, i: (0, j)),
            pl.BlockSpec((bm, bn), lambda j, i: (i, j)),
        ],
        out_specs=pl.BlockSpec((bm, bn), lambda j, i: (i, j)),
        compiler_params=_cparams(("parallel", "parallel")),
        name="resid_matmul",
    )(a, w, resid)


def _ffn_up_kernel(x_ref, g_ref, wg_ref, wu_ref, o_ref, hn_ref):
    @pl.when(pl.program_id(1) == 0)
    def _():
        x = x_ref[...]
        hn_ref[...] = (x * lax.rsqrt(jnp.mean(x * x, axis=-1, keepdims=True) + EPS) * g_ref[...]).astype(BF16)

    hn = hn_ref[...]
    gate = _dot(hn, wg_ref[...])
    up = _dot(hn, wu_ref[...])
    o_ref[...] = (gate * jax.nn.sigmoid(gate) * up).astype(o_ref.dtype)


def _ffn_up(x1, gain, wg, wu, bm=1024, bn=512):
    t, d = x1.shape
    n = wg.shape[1]
    return pl.pallas_call(
        _ffn_up_kernel,
        out_shape=jax.ShapeDtypeStruct((t, n), BF16),
        grid=(t // bm, n // bn),
        in_specs=[
            pl.BlockSpec((bm, d), lambda i, j: (i, 0)),
            pl.BlockSpec((1, d), lambda i, j: (0, 0)),
            pl.BlockSpec((d, bn), lambda i, j: (0, j)),
            pl.BlockSpec((d, bn), lambda i, j: (0, j)),
        ],
        out_specs=pl.BlockSpec((bm, bn), lambda i, j: (i, j)),
        scratch_shapes=[pltpu.VMEM((bm, d), BF16)],
        compiler_params=_cparams(("parallel", "arbitrary")),
        name="ffn_up",
    )(x1, gain, wg, wu)


def _pack_in_weights(w):
    sizes = (NSA_Q_DIM, NSA_KV_DIM, NSA_HEADS * N_BRANCH, 3 * GDN_DIM, GDN_DIM, GDN_HEADS, GDN_HEADS, D_MODEL, D_MODEL)
    offs = np.concatenate([[0], np.cumsum(sizes)])
    seg = [w[:, offs[i]:offs[i + 1]] for i in range(len(sizes))]
    q_a, kv_a, gate_a, qkv_b, z_b, a_b, b_b, m_a, m_b = seg
    w_big = jnp.concatenate([q_a, kv_a, qkv_b, z_b, m_a, m_b], axis=1).astype(BF16)
    d = w.shape[0]
    small = []
    per_group = NSA_REP * N_BRANCH
    for g in range(NSA_GROUPS):
        small.append(jnp.pad(gate_a[:, g * per_group:(g + 1) * per_group], ((0, 0), (0, 128 - per_group))))
    for hg in range(GDN_NHG):
        hs = slice(hg * GDN_HB, (hg + 1) * GDN_HB)
        small.append(jnp.pad(jnp.concatenate([a_b[:, hs], b_b[:, hs]], axis=1), ((0, 0), (0, 128 - 2 * GDN_HB))))
    w_small = jnp.concatenate(small, axis=1).astype(BF16)
    assert w_big.shape == (d, NP_BIG) and w_small.shape == (d, NP_SMALL)
    return w_big, w_small


def _lane_rows(v):
    return jnp.pad(v.reshape(GDN_NHG, 1, GDN_HB).astype(F32), ((0, 0), (0, 0), (0, 128 - GDN_HB)))


def _nsa_constants(seq):
    n_cmp = (seq - CMP_BLOCK) // CMP_STRIDE + 1
    nc = seq // CMP_STRIDE
    n_sel = seq // SLC_BLOCK
    cmp_start = np.arange(n_cmp) * CMP_STRIDE
    sel_start = np.arange(n_sel) * SLC_BLOCK
    overlap = (np.minimum(cmp_start[:, None] + CMP_BLOCK, sel_start[None, :] + SLC_BLOCK)
               - np.maximum(cmp_start[:, None], sel_start[None, :]))
    agg = np.zeros((nc, n_sel), np.float32)
    agg[:n_cmp] = np.clip(overlap, 0, None) / CMP_BLOCK
    expand = (np.arange(seq)[None, :] // SLC_BLOCK == np.arange(n_sel)[:, None]).astype(np.float32)
    eye = np.eye(NSA_TQ, dtype=np.float32)
    return jnp.asarray(agg.T, BF16), jnp.asarray(expand, BF16), jnp.asarray(eye, BF16)


def _gdn_constants():
    idx = np.arange(GDN_TS)
    same = (idx[:, None] // GDN_CHUNK) == (idx[None, :] // GDN_CHUNK)
    ltri = (same & (idx[:, None] >= idx[None, :])).astype(np.float32)
    lall = same.astype(np.float32)
    return jnp.asarray(ltri, BF16), jnp.asarray(lall, BF16), jnp.asarray(np.eye(HEAD_DIM, dtype=np.float32), BF16)


def _layer(x, attn_norm, w_in, nsa_q_norm, nsa_k_norm, cmp_pos, w_cmp, gdn_conv, gdn_a_log, gdn_dt_bias,
           gdn_out_norm, w_branch_a, w_branch_b, w_out, ffn_norm, w_gate, w_up, w_down):
    b, seq, d = x.shape
    t = b * seq
    x2 = x.reshape(t, d)

    w_big, w_small = _pack_in_weights(w_in)
    p2, ps2 = _in_proj(x2, attn_norm.reshape(1, d), w_big, w_small)
    p3 = p2.reshape(b, seq, NP_BIG)
    ps3 = ps2.reshape(b, seq, NP_SMALL)

    nc = seq // CMP_STRIDE
    kv0 = p3[:, :, OFF_KV:OFF_KV + 2 * NSA_GROUPS * HEAD_DIM]
    ckv = kv0.reshape(b, nc, CMP_STRIDE, 2 * NSA_GROUPS, HEAD_DIM).transpose(0, 3, 1, 2, 4)
    ckv = ckv.reshape(b, 2 * NSA_GROUPS, nc, CMP_STRIDE * HEAD_DIM)
    w_cmp2 = w_cmp.reshape(2, CMP_BLOCK * HEAD_DIM, HEAD_DIM).astype(BF16)
    pos2 = jnp.broadcast_to(cmp_pos.reshape(2, 1, CMP_BLOCK * HEAD_DIM), (2, 8, CMP_BLOCK * HEAD_DIM)).astype(BF16)
    kc, vc = _nsa_compress(ckv, w_cmp2, pos2, nsa_k_norm[0:1])

    agg_t, expand, eye_q = _nsa_constants(seq)
    o_a = _nsa_attention(p3, ps3, kc, vc, agg_t, expand, eye_q, nsa_q_norm.reshape(1, HEAD_DIM), nsa_k_norm)

    ltri, lall, eye_h = _gdn_constants()
    conv_w3 = gdn_conv.reshape(GDN_CONV, 3, GDN_DIM).transpose(1, 0, 2)
    o_b = _gdn(p3, ps3, conv_w3, _lane_rows(gdn_a_log), _lane_rows(gdn_dt_bias),
               gdn_out_norm.reshape(1, HEAD_DIM), ltri, lall, eye_h)

    mix = _merge(o_a.reshape(t, NSA_Q_DIM), o_b.reshape(t, GDN_DIM), w_branch_a.astype(BF16),
                 w_branch_b.astype(BF16), p2)
    x1 = _resid_matmul(mix, w_out.astype(BF16), x2, bm=1024, bn=1024)

    act = _ffn_up(x1, ffn_norm.reshape(1, d), w_gate.astype(BF16), w_up.astype(BF16))
    out = _resid_matmul(act, w_down.astype(BF16), x1, bm=512, bn=1024)
    return out.reshape(b, seq, d)


def kernel(x, attn_norm, w_in, nsa_q_norm, nsa_k_norm, cmp_pos, w_cmp, gdn_conv, gdn_a_log, gdn_dt_bias,
           gdn_out_norm, w_branch_a, w_branch_b, w_out, ffn_norm, w_gate, w_up, w_down):
    for l in range(attn_norm.shape[0]):
        x = _layer(x, attn_norm[l], w_in[l], nsa_q_norm[l], nsa_k_norm[l], cmp_pos[l], w_cmp[l], gdn_conv[l],
                   gdn_a_log[l], gdn_dt_bias[l], gdn_out_norm[l], w_branch_a[l], w_branch_b[l], w_out[l],
                   ffn_norm[l], w_gate[l], w_up[l], w_down[l])
    return x
```

```python
import functools

import numpy as np
import jax
import jax.numpy as jnp
from jax import lax
from jax.experimental import pallas as pl
from jax.experimental.pallas import tpu as pltpu

F32 = jnp.float32
BF16 = jnp.bfloat16

D_MODEL = 2048
EPS = 1e-6
NEG_INF = -1e30
TINY = 1e-30
FORCE_SCORE = 1e4
LOG2E = 1.4426950408889634

NSA_HEADS = 16
NSA_GROUPS = 4
NSA_REP = NSA_HEADS // NSA_GROUPS
HEAD_DIM = 128
N_BRANCH = 3
CMP_BLOCK = 32
CMP_STRIDE = 16
SLC_BLOCK = 64
SLC_TOPK = 16
WINDOW = 512

GDN_HEADS = 16
GDN_CONV = 4
GDN_CHUNK = 64

D_FF = 5632

NSA_Q_DIM = NSA_HEADS * HEAD_DIM
NSA_KV_DIM = N_BRANCH * 2 * NSA_GROUPS * HEAD_DIM
GDN_DIM = GDN_HEADS * HEAD_DIM
OFF_Q = 0
OFF_KV = OFF_Q + NSA_Q_DIM
OFF_GQKV = OFF_KV + NSA_KV_DIM
OFF_Z = OFF_GQKV + 3 * GDN_DIM
OFF_MA = OFF_Z + GDN_DIM
OFF_MB = OFF_MA + D_MODEL
NP_BIG = OFF_MB + D_MODEL

GDN_HB = 4
GDN_NHG = GDN_HEADS // GDN_HB
NP_SMALL = (NSA_GROUPS + GDN_NHG) * 128

VMEM_LIMIT = 56 * 1024 * 1024

NT_DIMS = (((1,), (1,)), ((), ()))


def _dot(a, b):
    return jnp.dot(a, b, preferred_element_type=F32)


def _dot_nt(a, b):
    return lax.dot_general(a, b, NT_DIMS, preferred_element_type=F32)


def _cparams(sem):
    return pltpu.CompilerParams(dimension_semantics=sem, vmem_limit_bytes=VMEM_LIMIT)


def _in_proj_kernel(x_ref, g_ref, w_ref, ws_ref, o_ref, os_ref, hn_ref):
    @pl.when(pl.program_id(1) == 0)
    def _():
        x = x_ref[...]
        y = x * lax.rsqrt(jnp.mean(x * x, axis=-1, keepdims=True) + EPS) * g_ref[...]
        hn = y.astype(BF16)
        hn_ref[...] = hn
        os_ref[...] = _dot(hn, ws_ref[...])

    o_ref[...] = _dot(hn_ref[...], w_ref[...]).astype(o_ref.dtype)


def _in_proj(x2, gain, w_big, w_small, bm=1024, bn=512):
    t, d = x2.shape
    n = w_big.shape[1]
    ns = w_small.shape[1]
    return pl.pallas_call(
        _in_proj_kernel,
        out_shape=(jax.ShapeDtypeStruct((t, n), BF16), jax.ShapeDtypeStruct((t, ns), F32)),
        grid=(t // bm, n // bn),
        in_specs=[
            pl.BlockSpec((bm, d), lambda i, j: (i, 0)),
            pl.BlockSpec((1, d), lambda i, j: (0, 0)),
            pl.BlockSpec((d, bn), lambda i, j: (0, j)),
            pl.BlockSpec((d, ns), lambda i, j: (0, 0)),
        ],
        out_specs=(
            pl.BlockSpec((bm, bn), lambda i, j: (i, j)),
            pl.BlockSpec((bm, ns), lambda i, j: (i, 0)),
        ),
        scratch_shapes=[pltpu.VMEM((bm, d), BF16)],
        compiler_params=_cparams(("parallel", "arbitrary")),
        name="in_proj",
    )(x2, gain, w_big, w_small)


def _cmp_kernel(ck_ref, cv_ref, w_ref, pos_ref, kg_ref, kc_ref, vc_ref):
    half = CMP_STRIDE * HEAD_DIM

    def compress(c_ref, idx):
        c = c_ref[...]
        w = w_ref[idx]
        lo = _dot(c, w[:half])
        hi = _dot(c, w[half:])
        bias = _dot(pos_ref[idx], w)[0:1]
        n = lo.shape[0]
        return lo + pltpu.roll(hi, n - 1, 0) + bias

    kc = compress(ck_ref, 0)
    kc = kc * lax.rsqrt(jnp.mean(kc * kc, axis=-1, keepdims=True) + EPS) * kg_ref[...]
    kc_ref[...] = kc.astype(BF16)
    vc_ref[...] = compress(cv_ref, 1).astype(BF16)


def _nsa_compress(ckv, w_cmp2, pos2, k_gain0):
    b, _, nc, width = ckv.shape
    g = NSA_GROUPS
    out = jax.ShapeDtypeStruct((b, g, nc, HEAD_DIM), BF16)
    return pl.pallas_call(
        _cmp_kernel,
        out_shape=(out, out),
        grid=(b, g),
        in_specs=[
            pl.BlockSpec((None, None, nc, width), lambda i, j: (i, j, 0, 0)),
            pl.BlockSpec((None, None, nc, width), lambda i, j: (i, j + NSA_GROUPS, 0, 0)),
            pl.BlockSpec((2, 2 * width, HEAD_DIM), lambda i, j: (0, 0, 0)),
            pl.BlockSpec((2, 8, 2 * width), lambda i, j: (0, 0, 0)),
            pl.BlockSpec((1, HEAD_DIM), lambda i, j: (0, 0)),
        ],
        out_specs=(
            pl.BlockSpec((None, None, nc, HEAD_DIM), lambda i, j: (i, j, 0, 0)),
            pl.BlockSpec((None, None, nc, HEAD_DIM), lambda i, j: (i, j, 0, 0)),
        ),
        compiler_params=_cparams(("parallel", "parallel")),
        name="nsa_compress",
    )(ckv, ckv, w_cmp2, pos2, k_gain0)


NSA_TQ = 128
NSA_TK = 512
NORM_ROWS = 512
RANK_UNROLL = 4


def _rms(xf, gain):
    return xf * lax.rsqrt(jnp.mean(xf * xf, axis=-1, keepdims=True) + EPS) * gain


def _lane_fold(p):
    acc = p[:, 0:128]
    for c in range(1, p.shape[1] // 128):
        acc = acc + p[:, c * 128:(c + 1) * 128]
    return acc


def _nsa_kernel(q_ref, gl_ref, ks_ref, vs_ref, kw_ref, vw_ref, kc_ref, vc_ref, aggt_ref, eye_ref, wb_ref,
                qg_ref, kg_ref, o_ref, ksa_ref, kwa_ref, vwp_ref, sc_ref, *, seq):
    tq, tk, rep, dh = NSA_TQ, NSA_TK, NSA_REP, HEAD_DIM
    rows = rep * tq
    n_sel = seq // SLC_BLOCK
    n_topk = min(SLC_TOPK, n_sel)
    qi = pl.program_id(2)
    q0 = qi * tq

    @pl.when(qi == 0)
    def _():
        kwa_ref[0:WINDOW, 0:dh] = jnp.zeros((WINDOW, dh), BF16)
        kwa_ref[0:WINDOW, dh:2 * dh] = jnp.where(
            lax.broadcasted_iota(jnp.int32, (WINDOW, dh), 1) == 0, 1.0, 0.0).astype(BF16)
        vwp_ref[0:WINDOW, :] = jnp.zeros((WINDOW, dh), BF16)
        lane = lax.broadcasted_iota(jnp.int32, (NORM_ROWS, dh), 1)
        sub = lax.broadcasted_iota(jnp.int32, (NORM_ROWS, dh), 0)

        def body(c, carry):
            r0 = pl.multiple_of(c * NORM_ROWS, NORM_ROWS)
            r = pl.ds(r0, NORM_ROWS)
            rw = pl.ds(r0 + WINDOW, NORM_ROWS)
            ksa_ref[r, 0:dh] = _rms(ks_ref[r, :].astype(F32), kg_ref[1:2, :]).astype(BF16)
            ksa_ref[r, dh:2 * dh] = jnp.where((r0 + sub) // SLC_BLOCK == lane, 1.0, 0.0).astype(BF16)
            kwa_ref[rw, 0:dh] = _rms(kw_ref[r, :].astype(F32), kg_ref[2:3, :]).astype(BF16)
            kwa_ref[rw, dh:2 * dh] = jnp.zeros((NORM_ROWS, dh), BF16)
            vwp_ref[rw, :] = vw_ref[r, :]
            return carry
        lax.fori_loop(0, seq // NORM_ROWS, body, 0)

    qscale = dh ** -0.5 * LOG2E
    q = q_ref[...].astype(F32)
    qs = jnp.concatenate(
        [(_rms(q[:, r * dh:(r + 1) * dh], qg_ref[...]) * qscale).astype(BF16) for r in range(rep)], axis=0)

    n_cmp = kc_ref.shape[0]
    s = _dot_nt(qs, kc_ref[...])
    t_c = q0 + (lax.broadcasted_iota(jnp.int32, (rows, n_cmp), 0) & (tq - 1))
    cmp_end = lax.broadcasted_iota(jnp.int32, (rows, n_cmp), 1) * CMP_STRIDE + (CMP_BLOCK - 1)
    mask = cmp_end <= t_c
    s = jnp.where(mask, s, NEG_INF)
    e = jnp.where(mask, jnp.exp2(s - jnp.max(s, axis=-1, keepdims=True)), 0.0)
    p = e / jnp.maximum(jnp.sum(e, axis=-1, keepdims=True), TINY)
    o_cmp = _dot(p.astype(BF16), vc_ref[...])
    p_sum = p[0:tq]
    for r in range(1, rep):
        p_sum = p_sum + p[r * tq:(r + 1) * tq]
    imp_t = _dot_nt(aggt_ref[...], p_sum.astype(BF16))

    jidx = lax.broadcasted_iota(jnp.int32, (n_sel, tq), 0)
    t_l = q0 + lax.broadcasted_iota(jnp.int32, (n_sel, tq), 1)
    cur = t_l // SLC_BLOCK
    forced = (jidx == 0) | (jidx == cur) | (jidx == cur - 1)
    causal = jidx * SLC_BLOCK <= t_l
    score = jnp.where(forced, FORCE_SCORE, jnp.where(causal, imp_t, -FORCE_SCORE))
    sc_ref[...] = score

    def rank_body(it, rank):
        for u in range(RANK_UNROLL):
            i = it * RANK_UNROLL + u
            row = sc_ref[pl.ds(i, 1), :]
            ge = jnp.where(row >= score, 1.0, 0.0)
            gt = jnp.where(row > score, 1.0, 0.0)
            rank = rank + jnp.where(jidx > i, ge, gt)
        return rank

    n_blk = (q0 + tq - 1) // SLC_BLOCK + 1
    n_it = (n_blk + RANK_UNROLL - 1) // RANK_UNROLL
    rank = lax.fori_loop(0, n_it, rank_body, jnp.zeros((n_sel, tq), F32))
    selneg_t = jnp.where(rank < n_topk, 0.0, NEG_INF)
    if n_sel < dh:
        selneg_t = jnp.concatenate([selneg_t, jnp.zeros((dh - n_sel, tq), F32)], axis=0)
    selneg = _dot_nt(eye_ref[...], selneg_t.astype(BF16)).astype(BF16)
    qa_sel = jnp.concatenate([qs, jnp.concatenate([selneg] * rep, axis=0)], axis=1)

    def sel_tile(kt, carry, diagonal):
        m, l, acc = carry
        k0 = pl.multiple_of(kt * tk, tk)
        s = _dot_nt(qa_sel, ksa_ref[pl.ds(k0, tk), :])
        if diagonal:
            t_r = q0 + (lax.broadcasted_iota(jnp.int32, (rows, tk), 0) & (tq - 1))
            kpos = k0 + lax.broadcasted_iota(jnp.int32, (rows, tk), 1)
            s = jnp.where(kpos <= t_r, s, NEG_INF)
        m_new = jnp.maximum(m, jnp.max(s, axis=-1, keepdims=True))
        p = jnp.exp2(s - m_new)
        alpha = jnp.exp2(m - m_new)
        l_new = alpha * l + _lane_fold(p)
        acc_new = alpha * acc + _dot(p.astype(BF16), vs_ref[pl.ds(k0, tk), :])
        return m_new, l_new, acc_new

    init = (jnp.full((rows, 1), NEG_INF, F32), jnp.zeros((rows, 128), F32), jnp.zeros((rows, dh), F32))
    n_full = q0 // tk
    carry = lax.fori_loop(0, n_full, lambda kt, c: sel_tile(kt, c, False), init)
    _, l_s, acc_s = sel_tile(n_full, carry, True)
    o_slc = acc_s / jnp.sum(l_s, axis=-1, keepdims=True)

    wk = WINDOW + tq
    padneg = jnp.where(lax.broadcasted_iota(jnp.int32, (rows, dh), 1) == 0, NEG_INF, 0.0).astype(BF16)
    qa_win = jnp.concatenate([qs, padneg], axis=1)
    kr = pl.ds(pl.multiple_of(q0, tq), wk)
    s = _dot_nt(qa_win, kwa_ref[kr, :]) + jnp.concatenate([wb_ref[...]] * rep, axis=0)
    p = jnp.exp2(s - jnp.max(s, axis=-1, keepdims=True))
    l_w = jnp.sum(_lane_fold(p), axis=-1, keepdims=True)
    o_win = _dot(p.astype(BF16), vwp_ref[kr, :]) / l_w

    gates = jax.nn.sigmoid(gl_ref[...])
    for r in range(rep):
        rs = slice(r * tq, (r + 1) * tq)
        o = (gates[:, 3 * r:3 * r + 1] * o_cmp[rs] + gates[:, 3 * r + 1:3 * r + 2] * o_slc[rs]
             + gates[:, 3 * r + 2:3 * r + 3] * o_win[rs])
        o_ref[:, r * dh:(r + 1) * dh] = o.astype(o_ref.dtype)


def _nsa_attention(p3, ps3, kc, vc, agg_t, eye, win_bias, q_gain, k_gain):
    b, seq, _ = p3.shape
    g, tq, dh = NSA_GROUPS, NSA_TQ, HEAD_DIM
    n_sel = seq // SLC_BLOCK
    assert n_sel <= dh and n_sel % 8 == 0 and seq % NSA_TK == 0
    nc = kc.shape[2]
    kvb = OFF_KV // dh

    def kv_spec(branch, is_v):
        base = kvb + (branch * 2 + is_v) * g
        return pl.BlockSpec((None, seq, dh), lambda i, j, k: (i, 0, base + j))

    qw = NSA_REP * dh
    return pl.pallas_call(
        functools.partial(_nsa_kernel, seq=seq),
        out_shape=jax.ShapeDtypeStruct((b, seq, NSA_Q_DIM), BF16),
        grid=(b, g, seq // tq),
        in_specs=[
            pl.BlockSpec((None, tq, qw), lambda i, j, k: (i, k, j)),
            pl.BlockSpec((None, tq, 128), lambda i, j, k: (i, k, j)),
            kv_spec(1, 0), kv_spec(1, 1), kv_spec(2, 0), kv_spec(2, 1),
            pl.BlockSpec((None, None, nc, dh), lambda i, j, k: (i, j, 0, 0)),
            pl.BlockSpec((None, None, nc, dh), lambda i, j, k: (i, j, 0, 0)),
            pl.BlockSpec((n_sel, nc), lambda i, j, k: (0, 0)),
            pl.BlockSpec((tq, tq), lambda i, j, k: (0, 0)),
            pl.BlockSpec((tq, WINDOW + tq), lambda i, j, k: (0, 0)),
            pl.BlockSpec((1, dh), lambda i, j, k: (0, 0)),
            pl.BlockSpec((N_BRANCH, dh), lambda i, j, k: (0, 0)),
        ],
        out_specs=pl.BlockSpec((None, tq, qw), lambda i, j, k: (i, k, j)),
        scratch_shapes=[
            pltpu.VMEM((seq, 2 * dh), BF16),
            pltpu.VMEM((seq + WINDOW, 2 * dh), BF16),
            pltpu.VMEM((seq + WINDOW, dh), BF16),
            pltpu.VMEM((n_sel, tq), F32),
        ],
        compiler_params=_cparams(("parallel", "parallel", "arbitrary")),
        name="nsa_attention",
    )(p3, ps3, p3, p3, p3, p3, kc, vc, agg_t, eye, win_bias, q_gain, k_gain)


GDN_TS = 256
GDN_INV = 128
HALO = 8


def _split3(x):
    hi = x.astype(BF16)
    r1 = x - hi.astype(F32)
    mid = r1.astype(BF16)
    lo = (r1 - mid.astype(F32)).astype(BF16)
    return hi, mid, lo


def _gdn_kernel(q_ref, k_ref, v_ref, z_ref, ab_ref, cw_ref, alog_ref, dtb_ref, og_ref, ltri_ref, lall_ref,
                eye_ref, o_ref, xe_ref, st_ref):
    ts, hb, dh, ch = GDN_TS, GDN_HB, HEAD_DIM, GDN_CHUNK
    ti = pl.program_id(2)

    @pl.when(ti == 0)
    def _():
        xe_ref[:, 0:HALO, :] = jnp.zeros((3, HALO, hb * dh), F32)
        st_ref[...] = jnp.zeros_like(st_ref)

    conv = []
    for part, ref in enumerate((q_ref, k_ref, v_ref)):
        xe_ref[part, HALO:HALO + ts, :] = ref[...].astype(F32)
        acc = None
        for i in range(GDN_CONV):
            d = GDN_CONV - 1 - i
            term = cw_ref[part, i:i + 1, :] * xe_ref[part, HALO - d:HALO - d + ts, :]
            acc = term if acc is None else acc + term
        conv.append(acc * jax.nn.sigmoid(acc))
        xe_ref[part, 0:HALO, :] = xe_ref[part, ts:ts + HALO, :]

    ab = ab_ref[...]
    xg = ab + dtb_ref[...]
    softplus = jnp.maximum(xg, 0.0) + jnp.log(1.0 + jnp.exp(-jnp.abs(xg)))
    g_tok = -jnp.exp(alog_ref[...]) * softplus
    beta = jax.nn.sigmoid(ab)
    parts = _split3(g_tok)
    ltri, lall = ltri_ref[...], lall_ref[...]
    g_cum = _dot(ltri, parts[0]) + _dot(ltri, parts[1]) + _dot(ltri, parts[2])
    g_end = _dot(lall, parts[0]) + _dot(lall, parts[1]) + _dot(lall, parts[2])
    e_cum = jnp.exp(g_cum)
    e_rel = jnp.exp(g_end - g_cum)
    e_end = jnp.exp(g_end)
    cum3 = jnp.concatenate(_split3(g_cum), axis=1)

    ri = lax.broadcasted_iota(jnp.int32, (ts, ts), 0)
    ci = lax.broadcasted_iota(jnp.int32, (ts, ts), 1)
    same = (ri // ch) == (ci // ch)
    tril = same & (ri >= ci)
    strict = same & (ri > ci)
    lane3 = lax.broadcasted_iota(jnp.int32, (ts, 3 * dh), 1) & (dh - 1)
    ri_b = lax.broadcasted_iota(jnp.int32, (GDN_INV, GDN_INV), 0)
    ci_b = lax.broadcasted_iota(jnp.int32, (GDN_INV, GDN_INV), 1)
    ident = jnp.where(ri_b == ci_b, 1.0, 0.0)
    eye_bf = eye_ref[...]
    n_inv = ts // GDN_INV
    n_ch = ts // ch

    heads = []
    for hh in range(hb):
        cs = slice(hh * dh, (hh + 1) * dh)
        qh, kh, vh = conv[0][:, cs], conv[1][:, cs], conv[2][:, cs]
        qh = qh * lax.rsqrt(jnp.sum(qh * qh, axis=-1, keepdims=True) + EPS) * (dh ** -0.5)
        kh = kh * lax.rsqrt(jnp.sum(kh * kh, axis=-1, keepdims=True) + EPS)
        bt = beta[:, hb + hh:hb + hh + 1]
        pick = jnp.where(lane3 == hh, 1.0, 0.0).astype(BF16)
        gc_row = _dot_nt(pick, cum3)
        diff = g_cum[:, hh:hh + 1] - gc_row
        e = jnp.exp(jnp.minimum(diff, 0.0))
        kb = kh * bt
        k_bf = kh.astype(BF16)
        a_mat = _dot_nt(kb.astype(BF16), k_bf) * jnp.where(strict, e, 0.0)
        qk = (_dot_nt(qh.astype(BF16), k_bf) * jnp.where(tril, e, 0.0)).astype(BF16)
        heads.append(dict(
            a=a_mat, qk=qk,
            vb=(vh * bt).astype(BF16),
            kbg=(kb * e_cum[:, hh:hh + 1]).astype(BF16),
            qg=(qh * e_cum[:, hh:hh + 1]).astype(BF16),
            kd=(kh * e_rel[:, hh:hh + 1]).astype(BF16),
        ))

    probs = []
    for hh in range(hb):
        for blk in range(n_inv):
            bs = slice(blk * GDN_INV, (blk + 1) * GDN_INV)
            pw = -heads[hh]["a"][bs, bs]
            probs.append([pw, ident + pw])
    for _ in range(5):
        for pr in probs:
            pw_bf = pr[0].astype(BF16)
            pr[0] = _dot(pw_bf, pw_bf)
        for pr in probs:
            pr[1] = pr[1] + _dot(pr[1].astype(BF16), pr[0].astype(BF16))
    for hh in range(hb):
        u_parts, w_parts = [], []
        for blk in range(n_inv):
            bs = slice(blk * GDN_INV, (blk + 1) * GDN_INV)
            t_bf = probs[hh * n_inv + blk][1].astype(BF16)
            u_parts.append(_dot(t_bf, heads[hh]["vb"][bs]))
            w_parts.append(_dot(t_bf, heads[hh]["kbg"][bs]))
        heads[hh]["u"] = jnp.concatenate(u_parts, axis=0)
        heads[hh]["w"] = jnp.concatenate(w_parts, axis=0).astype(BF16)

    states = [st_ref[hh] for hh in range(hb)]
    v_new = [[] for _ in range(hb)]
    o_inter = [[] for _ in range(hb)]
    for c in range(n_ch):
        rs = slice(c * ch, (c + 1) * ch)
        for hh in range(hb):
            hd = heads[hh]
            ws = _dot(jnp.concatenate([hd["w"][rs], hd["qg"][rs]], axis=0), states[hh].astype(BF16))
            vn = hd["u"][rs] - ws[:ch]
            o_inter[hh].append(ws[ch:])
            v_new[hh].append(vn)
            kd_t = _dot_nt(eye_bf, hd["kd"][rs]).astype(BF16)
            states[hh] = states[hh] * e_end[c * ch:c * ch + 1, hh:hh + 1] + _dot(kd_t, vn.astype(BF16))

    for hh in range(hb):
        cs = slice(hh * dh, (hh + 1) * dh)
        st_ref[hh] = states[hh]
        o = jnp.concatenate(o_inter[hh], axis=0) + _dot(heads[hh]["qk"],
                                                        jnp.concatenate(v_new[hh], axis=0).astype(BF16))
        o = o * lax.rsqrt(jnp.mean(o * o, axis=-1, keepdims=True) + EPS) * og_ref[...]
        z = z_ref[:, cs].astype(F32)
        o_ref[:, cs] = (o * (z * jax.nn.sigmoid(z))).astype(o_ref.dtype)


def _gdn(p3, ps3, conv_w3, alog_l, dtb_l, out_gain, ltri, lall, eye):
    b, seq, _ = p3.shape
    ts, hb, dh = GDN_TS, GDN_HB, HEAD_DIM
    wb = hb * dh
    qb, zb = OFF_GQKV // wb, OFF_Z // wb
    gw = GDN_DIM // wb
    return pl.pallas_call(
        _gdn_kernel,
        out_shape=jax.ShapeDtypeStruct((b, seq, GDN_DIM), BF16),
        grid=(b, GDN_NHG, seq // ts),
        in_specs=[
            pl.BlockSpec((None, ts, wb), lambda i, j, k: (i, k, qb + j)),
            pl.BlockSpec((None, ts, wb), lambda i, j, k: (i, k, qb + gw + j)),
            pl.BlockSpec((None, ts, wb), lambda i, j, k: (i, k, qb + 2 * gw + j)),
            pl.BlockSpec((None, ts, wb), lambda i, j, k: (i, k, zb + j)),
            pl.BlockSpec((None, ts, 128), lambda i, j, k: (i, k, NSA_GROUPS + j)),
            pl.BlockSpec((3, GDN_CONV, wb), lambda i, j, k: (0, 0, j)),
            pl.BlockSpec((None, 1, 128), lambda i, j, k: (j, 0, 0)),
            pl.BlockSpec((None, 1, 128), lambda i, j, k: (j, 0, 0)),
            pl.BlockSpec((1, dh), lambda i, j, k: (0, 0)),
            pl.BlockSpec((ts, ts), lambda i, j, k: (0, 0)),
            pl.BlockSpec((ts, ts), lambda i, j, k: (0, 0)),
            pl.BlockSpec((dh, dh), lambda i, j, k: (0, 0)),
        ],
        out_specs=pl.BlockSpec((None, ts, wb), lambda i, j, k: (i, k, j)),
        scratch_shapes=[
            pltpu.VMEM((3, HALO + ts, wb), F32),
            pltpu.VMEM((hb, dh, dh), F32),
        ],
        compiler_params=_cparams(("parallel", "parallel", "arbitrary")),
        name="gdn",
    )(p3, p3, p3, p3, ps3, conv_w3, alog_l, dtb_l, out_gain, ltri, lall, eye)


def _merge_kernel(oa_ref, ob_ref, wa_ref, wb_ref, ma_ref, mb_ref, o_ref):
    ya = _dot(oa_ref[...], wa_ref[...])
    yb = _dot(ob_ref[...], wb_ref[...])
    mix = jax.nn.sigmoid(ma_ref[...].astype(F32)) * ya + jax.nn.sigmoid(mb_ref[...].astype(F32)) * yb
    o_ref[...] = mix.astype(o_ref.dtype)


def _merge(oa, ob, wa, wb, p2, bm=512, bn=1024):
    t, d = oa.shape
    n = wa.shape[1]
    ma_b, mb_b = OFF_MA // bn, OFF_MB // bn
    return pl.pallas_call(
        _merge_kernel,
        out_shape=jax.ShapeDtypeStruct((t, n), BF16),
        grid=(n // bn, t // bm),
        in_specs=[
            pl.BlockSpec((bm, d), lambda j, i: (i, 0)),
            pl.BlockSpec((bm, d), lambda j, i: (i, 0)),
            pl.BlockSpec((d, bn), lambda j, i: (0, j)),
            pl.BlockSpec((d, bn), lambda j, i: (0, j)),
            pl.BlockSpec((bm, bn), lambda j, i: (i, ma_b + j)),
            pl.BlockSpec((bm, bn), lambda j, i: (i, mb_b + j)),
        ],
        out_specs=pl.BlockSpec((bm, bn), lambda j, i: (i, j)),
        compiler_params=_cparams(("parallel", "parallel")),
        name="merge",
    )(oa, ob, wa, wb, p2, p2)


def _resid_matmul_kernel(a_ref, w_ref, r_ref, o_ref):
    o_ref[...] = r_ref[...] + _dot(a_ref[...], w_ref[...])


def _resid_matmul(a, w, resid, bm, bn):
    t, k = a.shape
    n = w.shape[1]
    return pl.pallas_call(
        _resid_matmul_kernel,
        out_shape=jax.ShapeDtypeStruct((t, n), F32),
        grid=(n // bn, t // bm),
        in_specs=[
            pl.BlockSpec((bm, k), lambda j, i: (i, 0)),
            pl.BlockSpec((k, bn), lambda j, i: (0, j)),
            pl.BlockSpec((bm, bn), lambda j, i: (i, j)),
        ],
        out_specs=pl.BlockSpec((bm, bn), lambda j, i: (i, j)),
        compiler_params=_cparams(("parallel", "parallel")),
        name="resid_matmul",
    )(a, w, resid)


def _ffn_up_kernel(x_ref, g_ref, wg_ref, wu_ref, o_ref, hn_ref):
    @pl.when(pl.program_id(1) == 0)
    def _():
        x = x_ref[...]
        hn_ref[...] = (x * lax.rsqrt(jnp.mean(x * x, axis=-1, keepdims=True) + EPS) * g_ref[...]).astype(BF16)

    hn = hn_ref[...]
    gate = _dot(hn, wg_ref[...])
    up = _dot(hn, wu_ref[...])
    o_ref[...] = (gate * jax.nn.sigmoid(gate) * up).astype(o_ref.dtype)


def _ffn_up(x1, gain, wg, wu, bm=1024, bn=512):
    t, d = x1.shape
    n = wg.shape[1]
    return pl.pallas_call(
        _ffn_up_kernel,
        out_shape=jax.ShapeDtypeStruct((t, n), BF16),
        grid=(t // bm, n // bn),
        in_specs=[
            pl.BlockSpec((bm, d), lambda i, j: (i, 0)),
            pl.BlockSpec((1, d), lambda i, j: (0, 0)),
            pl.BlockSpec((d, bn), lambda i, j: (0, j)),
            pl.BlockSpec((d, bn), lambda i, j: (0, j)),
        ],
        out_specs=pl.BlockSpec((bm, bn), lambda i, j: (i, j)),
        scratch_shapes=[pltpu.VMEM((bm, d), BF16)],
        compiler_params=_cparams(("parallel", "arbitrary")),
        name="ffn_up",
    )(x1, gain, wg, wu)


def _pack_in_weights(w):
    sizes = (NSA_Q_DIM, NSA_KV_DIM, NSA_HEADS * N_BRANCH, 3 * GDN_DIM, GDN_DIM, GDN_HEADS, GDN_HEADS, D_MODEL, D_MODEL)
    offs = np.concatenate([[0], np.cumsum(sizes)])
    seg = [w[:, offs[i]:offs[i + 1]] for i in range(len(sizes))]
    q_a, kv_a, gate_a, qkv_b, z_b, a_b, b_b, m_a, m_b = seg
    w_big = jnp.concatenate([q_a, kv_a, qkv_b, z_b, m_a, m_b], axis=1).astype(BF16)
    d = w.shape[0]
    small = []
    per_group = NSA_REP * N_BRANCH
    for g in range(NSA_GROUPS):
        small.append(jnp.pad(gate_a[:, g * per_group:(g + 1) * per_group], ((0, 0), (0, 128 - per_group))))
    for hg in range(GDN_NHG):
        hs = slice(hg * GDN_HB, (hg + 1) * GDN_HB)
        small.append(jnp.pad(jnp.concatenate([a_b[:, hs], b_b[:, hs]], axis=1), ((0, 0), (0, 128 - 2 * GDN_HB))))
    w_small = jnp.concatenate(small, axis=1).astype(BF16)
    assert w_big.shape == (d, NP_BIG) and w_small.shape == (d, NP_SMALL)
    return w_big, w_small


def _lane_rows(v):
    return jnp.pad(v.reshape(GDN_NHG, 1, GDN_HB).astype(F32), ((0, 0), (0, 0), (0, 128 - GDN_HB)))


def _nsa_constants(seq):
    n_cmp = (seq - CMP_BLOCK) // CMP_STRIDE + 1
    nc = seq // CMP_STRIDE
    n_sel = seq // SLC_BLOCK
    cmp_start = np.arange(n_cmp) * CMP_STRIDE
    sel_start = np.arange(n_sel) * SLC_BLOCK
    overlap = (np.minimum(cmp_start[:, None] + CMP_BLOCK, sel_start[None, :] + SLC_BLOCK)
               - np.maximum(cmp_start[:, None], sel_start[None, :]))
    agg = np.zeros((nc, n_sel), np.float32)
    agg[:n_cmp] = np.clip(overlap, 0, None) / CMP_BLOCK
    eye = np.eye(NSA_TQ, dtype=np.float32)
    i = np.arange(NSA_TQ)[:, None]
    c = np.arange(WINDOW + NSA_TQ)[None, :]
    win_bias = np.where((c > i) & (c <= i + WINDOW), 0.0, NEG_INF).astype(np.float32)
    return jnp.asarray(agg.T, BF16), jnp.asarray(eye, BF16), jnp.asarray(win_bias)


def _gdn_constants():
    idx = np.arange(GDN_TS)
    same = (idx[:, None] // GDN_CHUNK) == (idx[None, :] // GDN_CHUNK)
    ltri = (same & (idx[:, None] >= idx[None, :])).astype(np.float32)
    lall = same.astype(np.float32)
    return jnp.asarray(ltri, BF16), jnp.asarray(lall, BF16), jnp.asarray(np.eye(HEAD_DIM, dtype=np.float32), BF16)


def _layer(x, attn_norm, w_in, nsa_q_norm, nsa_k_norm, cmp_pos, w_cmp, gdn_conv, gdn_a_log, gdn_dt_bias,
           gdn_out_norm, w_branch_a, w_branch_b, w_out, ffn_norm, w_gate, w_up, w_down):
    b, seq, d = x.shape
    t = b * seq
    x2 = x.reshape(t, d)

    w_big, w_small = _pack_in_weights(w_in)
    p2, ps2 = _in_proj(x2, attn_norm.reshape(1, d), w_big, w_small)
    p3 = p2.reshape(b, seq, NP_BIG)
    ps3 = ps2.reshape(b, seq, NP_SMALL)

    nc = seq // CMP_STRIDE
    kv0 = p3[:, :, OFF_KV:OFF_KV + 2 * NSA_GROUPS * HEAD_DIM]
    ckv = kv0.reshape(b, nc, CMP_STRIDE, 2 * NSA_GROUPS, HEAD_DIM).transpose(0, 3, 1, 2, 4)
    ckv = ckv.reshape(b, 2 * NSA_GROUPS, nc, CMP_STRIDE * HEAD_DIM)
    w_cmp2 = w_cmp.reshape(2, CMP_BLOCK * HEAD_DIM, HEAD_DIM).astype(BF16)
    pos2 = jnp.broadcast_to(cmp_pos.reshape(2, 1, CMP_BLOCK * HEAD_DIM), (2, 8, CMP_BLOCK * HEAD_DIM)).astype(BF16)
    kc, vc = _nsa_compress(ckv, w_cmp2, pos2, nsa_k_norm[0:1])

    agg_t, eye_q, win_bias = _nsa_constants(seq)
    o_a = _nsa_attention(p3, ps3, kc, vc, agg_t, eye_q, win_bias, nsa_q_norm.reshape(1, HEAD_DIM), nsa_k_norm)

    ltri, lall, eye_h = _gdn_constants()
    conv_w3 = gdn_conv.reshape(GDN_CONV, 3, GDN_DIM).transpose(1, 0, 2)
    o_b = _gdn(p3, ps3, conv_w3, _lane_rows(gdn_a_log), _lane_rows(gdn_dt_bias),
               gdn_out_norm.reshape(1, HEAD_DIM), ltri, lall, eye_h)

    mix = _merge(o_a.reshape(t, NSA_Q_DIM), o_b.reshape(t, GDN_DIM), w_branch_a.astype(BF16),
                 w_branch_b.astype(BF16), p2)
    x1 = _resid_matmul(mix, w_out.astype(BF16), x2, bm=1024, bn=1024)

    act = _ffn_up(x1, ffn_norm.reshape(1, d), w_gate.astype(BF16), w_up.astype(BF16))
    out = _resid_matmul(act, w_down.astype(BF16), x1, bm=512, bn=1024)
    return out.reshape(b, seq, d)


def kernel(x, attn_norm, w_in, nsa_q_norm, nsa_k_norm, cmp_pos, w_cmp, gdn_conv, gdn_a_log, gdn_dt_bias,
           gdn_out_norm, w_branch_a, w_branch_b, w_out, ffn_norm, w_gate, w_up, w_down):
    for l in range(attn_norm.shape[0]):
        x = _layer(x, attn_norm[l], w_in[l], nsa_q_norm[l], nsa_k_norm[l], cmp_pos[l], w_cmp[l], gdn_conv[l],
                   gdn_a_log[l], gdn_dt_bias[l], gdn_out_norm[l], w_branch_a[l], w_branch_b[l], w_out[l],
                   ffn_norm[l], w_gate[l], w_up[l], w_down[l])
    return x
```

```python
import functools

import numpy as np
import jax
import jax.numpy as jnp
from jax import lax
from jax.experimental import pallas as pl
from jax.experimental.pallas import tpu as pltpu

F32 = jnp.float32
BF16 = jnp.bfloat16

D_MODEL = 2048
EPS = 1e-6
NEG_INF = -1e30
TINY = 1e-30
FORCE_SCORE = 1e4
LOG2E = 1.4426950408889634

NSA_HEADS = 16
NSA_GROUPS = 4
NSA_REP = NSA_HEADS // NSA_GROUPS
HEAD_DIM = 128
N_BRANCH = 3
CMP_BLOCK = 32
CMP_STRIDE = 16
SLC_BLOCK = 64
SLC_TOPK = 16
WINDOW = 512

GDN_HEADS = 16
GDN_CONV = 4
GDN_CHUNK = 64

D_FF = 5632

NSA_Q_DIM = NSA_HEADS * HEAD_DIM
NSA_KV_DIM = N_BRANCH * 2 * NSA_GROUPS * HEAD_DIM
GDN_DIM = GDN_HEADS * HEAD_DIM
OFF_Q = 0
OFF_KV = OFF_Q + NSA_Q_DIM
OFF_GQKV = OFF_KV + NSA_KV_DIM
OFF_Z = OFF_GQKV + 3 * GDN_DIM
OFF_MA = OFF_Z + GDN_DIM
OFF_MB = OFF_MA + D_MODEL
NP_BIG = OFF_MB + D_MODEL

GDN_HB = 8
GDN_NHG = GDN_HEADS // GDN_HB
NP_SMALL = (NSA_GROUPS + GDN_NHG) * 128

VMEM_LIMIT = 56 * 1024 * 1024

NT_DIMS = (((1,), (1,)), ((), ()))


def _dot(a, b):
    return jnp.dot(a, b, preferred_element_type=F32)


def _dot_nt(a, b):
    return lax.dot_general(a, b, NT_DIMS, preferred_element_type=F32)


def _cparams(sem):
    return pltpu.CompilerParams(dimension_semantics=sem, vmem_limit_bytes=VMEM_LIMIT)


def _in_proj_kernel(x_ref, g_ref, w_ref, ws_ref, o_ref, os_ref, hn_ref):
    @pl.when(pl.program_id(1) == 0)
    def _():
        x = x_ref[...]
        y = x * lax.rsqrt(jnp.mean(x * x, axis=-1, keepdims=True) + EPS) * g_ref[...]
        hn = y.astype(BF16)
        hn_ref[...] = hn
        os_ref[...] = _dot(hn, ws_ref[...])

    o_ref[...] = _dot(hn_ref[...], w_ref[...]).astype(o_ref.dtype)


def _in_proj(x2, gain, w_big, w_small, bm=1024, bn=512):
    t, d = x2.shape
    n = w_big.shape[1]
    ns = w_small.shape[1]
    return pl.pallas_call(
        _in_proj_kernel,
        out_shape=(jax.ShapeDtypeStruct((t, n), BF16), jax.ShapeDtypeStruct((t, ns), F32)),
        grid=(t // bm, n // bn),
        in_specs=[
            pl.BlockSpec((bm, d), lambda i, j: (i, 0)),
            pl.BlockSpec((1, d), lambda i, j: (0, 0)),
            pl.BlockSpec((d, bn), lambda i, j: (0, j)),
            pl.BlockSpec((d, ns), lambda i, j: (0, 0)),
        ],
        out_specs=(
            pl.BlockSpec((bm, bn), lambda i, j: (i, j)),
            pl.BlockSpec((bm, ns), lambda i, j: (i, 0)),
        ),
        scratch_shapes=[pltpu.VMEM((bm, d), BF16)],
        compiler_params=_cparams(("parallel", "arbitrary")),
        name="in_proj",
    )(x2, gain, w_big, w_small)


def _cmp_kernel(ck_ref, cv_ref, w_ref, pos_ref, kg_ref, kc_ref, vc_ref):
    half = CMP_STRIDE * HEAD_DIM

    def compress(c_ref, idx):
        c = c_ref[...]
        w = w_ref[idx]
        lo = _dot(c, w[:half])
        hi = _dot(c, w[half:])
        bias = _dot(pos_ref[idx], w)[0:1]
        n = lo.shape[0]
        return lo + pltpu.roll(hi, n - 1, 0) + bias

    kc = compress(ck_ref, 0)
    kc = kc * lax.rsqrt(jnp.mean(kc * kc, axis=-1, keepdims=True) + EPS) * kg_ref[...]
    kc_ref[...] = kc.astype(BF16)
    vc_ref[...] = compress(cv_ref, 1).astype(BF16)


def _nsa_compress(ckv, w_cmp2, pos2, k_gain0):
    b, _, nc, width = ckv.shape
    g = NSA_GROUPS
    out = jax.ShapeDtypeStruct((b, g, nc, HEAD_DIM), BF16)
    return pl.pallas_call(
        _cmp_kernel,
        out_shape=(out, out),
        grid=(b, g),
        in_specs=[
            pl.BlockSpec((None, None, nc, width), lambda i, j: (i, j, 0, 0)),
            pl.BlockSpec((None, None, nc, width), lambda i, j: (i, j + NSA_GROUPS, 0, 0)),
            pl.BlockSpec((2, 2 * width, HEAD_DIM), lambda i, j: (0, 0, 0)),
            pl.BlockSpec((2, 8, 2 * width), lambda i, j: (0, 0, 0)),
            pl.BlockSpec((1, HEAD_DIM), lambda i, j: (0, 0)),
        ],
        out_specs=(
            pl.BlockSpec((None, None, nc, HEAD_DIM), lambda i, j: (i, j, 0, 0)),
            pl.BlockSpec((None, None, nc, HEAD_DIM), lambda i, j: (i, j, 0, 0)),
        ),
        compiler_params=_cparams(("parallel", "parallel")),
        name="nsa_compress",
    )(ckv, ckv, w_cmp2, pos2, k_gain0)


NSA_TQ = 128
NSA_TK = 512
NORM_ROWS = 512
RANK_UNROLL = 4


def _rms(xf, gain):
    return xf * lax.rsqrt(jnp.mean(xf * xf, axis=-1, keepdims=True) + EPS) * gain


def _lane_fold(p):
    acc = p[:, 0:128]
    for c in range(1, p.shape[1] // 128):
        acc = acc + p[:, c * 128:(c + 1) * 128]
    return acc


def _nsa_kernel(q_ref, gl_ref, ks_ref, vs_ref, kw_ref, vw_ref, kc_ref, vc_ref, aggt_ref, eye_ref, wb_ref,
                qg_ref, kg_ref, o_ref, ksa_ref, kwa_ref, vwp_ref, sc_ref, s0_ref, s1_ref, m_ref, l_ref, acc_ref, *, seq):
    tq, tk, rep, dh = NSA_TQ, NSA_TK, NSA_REP, HEAD_DIM
    rows = rep * tq
    n_sel = seq // SLC_BLOCK
    n_topk = min(SLC_TOPK, n_sel)
    qi = pl.program_id(2)
    q0 = qi * tq

    @pl.when(qi == 0)
    def _():
        kwa_ref[0:WINDOW, 0:dh] = jnp.zeros((WINDOW, dh), BF16)
        kwa_ref[0:WINDOW, dh:2 * dh] = jnp.where(
            lax.broadcasted_iota(jnp.int32, (WINDOW, dh), 1) == 0, 1.0, 0.0).astype(BF16)
        vwp_ref[0:WINDOW, :] = jnp.zeros((WINDOW, dh), BF16)
        lane = lax.broadcasted_iota(jnp.int32, (NORM_ROWS, dh), 1)
        sub = lax.broadcasted_iota(jnp.int32, (NORM_ROWS, dh), 0)

        def body(c, carry):
            r0 = pl.multiple_of(c * NORM_ROWS, NORM_ROWS)
            r = pl.ds(r0, NORM_ROWS)
            rw = pl.ds(r0 + WINDOW, NORM_ROWS)
            ksa_ref[r, 0:dh] = _rms(ks_ref[r, :].astype(F32), kg_ref[1:2, :]).astype(BF16)
            ksa_ref[r, dh:2 * dh] = jnp.where((r0 + sub) // SLC_BLOCK == lane, 1.0, 0.0).astype(BF16)
            kwa_ref[rw, 0:dh] = _rms(kw_ref[r, :].astype(F32), kg_ref[2:3, :]).astype(BF16)
            kwa_ref[rw, dh:2 * dh] = jnp.zeros((NORM_ROWS, dh), BF16)
            vwp_ref[rw, :] = vw_ref[r, :]
            return carry
        lax.fori_loop(0, seq // NORM_ROWS, body, 0)

    qscale = dh ** -0.5 * LOG2E
    q = q_ref[...].astype(F32)
    qs = jnp.concatenate(
        [(_rms(q[:, r * dh:(r + 1) * dh], qg_ref[...]) * qscale).astype(BF16) for r in range(rep)], axis=0)

    wk = WINDOW + tq
    padneg = jnp.where(lax.broadcasted_iota(jnp.int32, (rows, dh), 1) == 0, NEG_INF, 0.0).astype(BF16)
    qa_win = jnp.concatenate([qs, padneg], axis=1)
    kr = pl.ds(pl.multiple_of(q0, tq), wk)
    s_w = _dot_nt(qa_win, kwa_ref[kr, :]) + jnp.concatenate([wb_ref[...]] * rep, axis=0)

    n_cmp = kc_ref.shape[0]
    s = _dot_nt(qs, kc_ref[...])
    p_w = jnp.exp2(s_w - jnp.max(s_w, axis=-1, keepdims=True))
    l_w = jnp.sum(_lane_fold(p_w), axis=-1, keepdims=True)
    o_win = _dot(p_w.astype(BF16), vwp_ref[kr, :]) / l_w
    t_c = q0 + (lax.broadcasted_iota(jnp.int32, (rows, n_cmp), 0) & (tq - 1))
    cmp_end = lax.broadcasted_iota(jnp.int32, (rows, n_cmp), 1) * CMP_STRIDE + (CMP_BLOCK - 1)
    mask = cmp_end <= t_c
    s = jnp.where(mask, s, NEG_INF)
    e = jnp.where(mask, jnp.exp2(s - jnp.max(s, axis=-1, keepdims=True)), 0.0)
    p = e / jnp.maximum(jnp.sum(e, axis=-1, keepdims=True), TINY)
    o_cmp = _dot(p.astype(BF16), vc_ref[...])
    p_sum = p[0:tq]
    for r in range(1, rep):
        p_sum = p_sum + p[r * tq:(r + 1) * tq]
    imp_t = _dot_nt(aggt_ref[...], p_sum.astype(BF16))

    jidx = lax.broadcasted_iota(jnp.int32, (n_sel, tq), 0)
    t_l = q0 + lax.broadcasted_iota(jnp.int32, (n_sel, tq), 1)
    cur = t_l // SLC_BLOCK
    forced = (jidx == 0) | (jidx == cur) | (jidx == cur - 1)
    causal = jidx * SLC_BLOCK <= t_l
    score = jnp.where(forced, FORCE_SCORE, jnp.where(causal, imp_t, -FORCE_SCORE))
    sc_ref[...] = score

    def rank_body(it, rank):
        for u in range(RANK_UNROLL):
            i = it * RANK_UNROLL + u
            row = sc_ref[pl.ds(i, 1), :]
            ge = jnp.where(row >= score, 1.0, 0.0)
            gt = jnp.where(row > score, 1.0, 0.0)
            rank = rank + jnp.where(jidx > i, ge, gt)
        return rank

    n_blk = (q0 + tq - 1) // SLC_BLOCK + 1
    n_it = (n_blk + RANK_UNROLL - 1) // RANK_UNROLL
    rank = lax.fori_loop(0, n_it, rank_body, jnp.zeros((n_sel, tq), F32))
    selneg_t = jnp.where(rank < n_topk, 0.0, NEG_INF)
    if n_sel < dh:
        selneg_t = jnp.concatenate([selneg_t, jnp.zeros((dh - n_sel, tq), F32)], axis=0)
    selneg = _dot_nt(eye_ref[...], selneg_t.astype(BF16)).astype(BF16)
    qa_sel = jnp.concatenate([qs, jnp.concatenate([selneg] * rep, axis=0)], axis=1)

    def qk_tile(kt):
        return _dot_nt(qa_sel, ksa_ref[pl.ds(pl.multiple_of(kt * tk, tk), tk), :])

    def consume(kt, buf, diagonal):
        k0 = pl.multiple_of(kt * tk, tk)
        s = buf[...]
        if diagonal:
            t_r = q0 + (lax.broadcasted_iota(jnp.int32, (rows, tk), 0) & (tq - 1))
            kpos = k0 + lax.broadcasted_iota(jnp.int32, (rows, tk), 1)
            s = jnp.where(kpos <= t_r, s, NEG_INF)
        m_prev = m_ref[...]
        m_new = jnp.maximum(m_prev, jnp.max(s, axis=-1, keepdims=True))
        p = jnp.exp2(s - jnp.concatenate([m_new] * (tk // 128), axis=1))
        alpha = jnp.exp2(m_prev - m_new)
        m_ref[...] = m_new
        l_ref[...] = alpha * l_ref[...] + _lane_fold(p)
        acc_ref[...] = alpha * acc_ref[...] + _dot(p.astype(BF16), vs_ref[pl.ds(k0, tk), :])

    m_ref[...] = jnp.full((rows, 128), NEG_INF, F32)
    l_ref[...] = jnp.zeros((rows, 128), F32)
    acc_ref[...] = jnp.zeros((rows, dh), F32)
    s0_ref[...] = qk_tile(0)
    n_full = q0 // tk

    def sel_body(it, carry):
        kt = 2 * it
        s1_ref[...] = qk_tile(kt + 1)
        consume(kt, s0_ref, False)
        s0_ref[...] = qk_tile(kt + 2)
        consume(kt + 1, s1_ref, False)
        return carry

    lax.fori_loop(0, n_full // 2, sel_body, 0)

    @pl.when(n_full % 2 == 1)
    def _():
        s1_ref[...] = qk_tile(n_full)
        consume(n_full - 1, s0_ref, False)
        consume(n_full, s1_ref, True)

    @pl.when(n_full % 2 == 0)
    def _():
        consume(n_full, s0_ref, True)

    o_slc = acc_ref[...] / jnp.sum(l_ref[...], axis=-1, keepdims=True)

    gates = jax.nn.sigmoid(gl_ref[...])
    for r in range(rep):
        rs = slice(r * tq, (r + 1) * tq)
        o = (gates[:, 3 * r:3 * r + 1] * o_cmp[rs] + gates[:, 3 * r + 1:3 * r + 2] * o_slc[rs]
             + gates[:, 3 * r + 2:3 * r + 3] * o_win[rs])
        o_ref[:, r * dh:(r + 1) * dh] = o.astype(o_ref.dtype)


def _nsa_attention(p3, ps3, kc, vc, agg_t, eye, win_bias, q_gain, k_gain):
    b, seq, _ = p3.shape
    g, tq, dh = NSA_GROUPS, NSA_TQ, HEAD_DIM
    n_sel = seq // SLC_BLOCK
    assert n_sel <= dh and n_sel % 8 == 0 and seq % NSA_TK == 0
    nc = kc.shape[2]
    kvb = OFF_KV // dh

    def kv_spec(branch, is_v):
        base = kvb + (branch * 2 + is_v) * g
        return pl.BlockSpec((None, seq, dh), lambda i, j, k: (i, 0, base + j))

    qw = NSA_REP * dh
    return pl.pallas_call(
        functools.partial(_nsa_kernel, seq=seq),
        out_shape=jax.ShapeDtypeStruct((b, seq, NSA_Q_DIM), BF16),
        grid=(b, g, seq // tq),
        in_specs=[
            pl.BlockSpec((None, tq, qw), lambda i, j, k: (i, k, j)),
            pl.BlockSpec((None, tq, 128), lambda i, j, k: (i, k, j)),
            kv_spec(1, 0), kv_spec(1, 1), kv_spec(2, 0), kv_spec(2, 1),
            pl.BlockSpec((None, None, nc, dh), lambda i, j, k: (i, j, 0, 0)),
            pl.BlockSpec((None, None, nc, dh), lambda i, j, k: (i, j, 0, 0)),
            pl.BlockSpec((n_sel, nc), lambda i, j, k: (0, 0)),
            pl.BlockSpec((tq, tq), lambda i, j, k: (0, 0)),
            pl.BlockSpec((tq, WINDOW + tq), lambda i, j, k: (0, 0)),
            pl.BlockSpec((1, dh), lambda i, j, k: (0, 0)),
            pl.BlockSpec((N_BRANCH, dh), lambda i, j, k: (0, 0)),
        ],
        out_specs=pl.BlockSpec((None, tq, qw), lambda i, j, k: (i, k, j)),
        scratch_shapes=[
            pltpu.VMEM((seq, 2 * dh), BF16),
            pltpu.VMEM((seq + WINDOW, 2 * dh), BF16),
            pltpu.VMEM((seq + WINDOW, dh), BF16),
            pltpu.VMEM((n_sel, tq), F32),
            pltpu.VMEM((NSA_REP * tq, NSA_TK), F32),
            pltpu.VMEM((NSA_REP * tq, NSA_TK), F32),
            pltpu.VMEM((NSA_REP * tq, 128), F32),
            pltpu.VMEM((NSA_REP * tq, 128), F32),
            pltpu.VMEM((NSA_REP * tq, dh), F32),
        ],
        compiler_params=_cparams(("parallel", "parallel", "arbitrary")),
        name="nsa_attention",
    )(p3, ps3, p3, p3, p3, p3, kc, vc, agg_t, eye, win_bias, q_gain, k_gain)


GDN_TS = 256
GDN_INV = 128
HALO = 8


def _gdn_kernel(q_ref, k_ref, v_ref, z_ref, ab_ref, cw_ref, alog_ref, dtb_ref, og_ref, sh_ref, eye_ref,
                o_ref, xe_ref, st_ref):
    ts, hb, dh, ch = GDN_TS, GDN_HB, HEAD_DIM, GDN_CHUNK
    ti = pl.program_id(2)

    n_inv = ts // GDN_INV
    n_ch = ts // ch

    @pl.when(ti == 0)
    def _():
        xe_ref[...] = jnp.zeros_like(xe_ref)
        st_ref[...] = jnp.zeros_like(st_ref)

    row8 = lax.broadcasted_iota(jnp.int32, (HALO, hb * dh), 0)
    conv = []
    for part, ref in enumerate((q_ref, k_ref, v_ref)):
        xb = ref[...]
        xf = xb.astype(F32)
        prev = xe_ref[part]
        acc = cw_ref[part, GDN_CONV - 1:GDN_CONV, :] * xf
        corr = jnp.zeros((HALO, hb * dh), F32)
        for d in range(1, GDN_CONV):
            wd = cw_ref[part, GDN_CONV - 1 - d:GDN_CONV - d, :]
            acc = acc + wd * _dot(sh_ref[d - 1], xb)
            corr = corr + wd * jnp.where(row8 < d, pltpu.roll(prev, d, 0), 0.0)
        acc = jnp.concatenate([acc[0:HALO] + corr, acc[HALO:]], axis=0)
        conv.append(acc * jax.nn.sigmoid(acc))
        xe_ref[part] = xf[ts - HALO:ts]

    ab = ab_ref[...]
    xg = ab + dtb_ref[...]
    softplus = jnp.maximum(xg, 0.0) + jnp.log(1.0 + jnp.exp(-jnp.abs(xg)))
    g_cum = -jnp.exp(alog_ref[...]) * softplus
    beta = jax.nn.sigmoid(ab)
    row_in_chunk = lax.broadcasted_iota(jnp.int32, (ts, 128), 0) & (ch - 1)
    step = 1
    while step < ch:
        g_cum = g_cum + jnp.where(row_in_chunk >= step, pltpu.roll(g_cum, step, 0), 0.0)
        step *= 2
    g_end = jnp.concatenate(
        [jnp.broadcast_to(g_cum[(c + 1) * ch - 1:(c + 1) * ch, :], (ch, 128)) for c in range(n_ch)], axis=0)
    e_cum = jnp.exp(g_cum)
    e_rel = jnp.exp(g_end - g_cum)
    e_end = jnp.exp(g_end)
    g_cum_t = jnp.transpose(g_cum)

    ri = lax.broadcasted_iota(jnp.int32, (ts, ts), 0)
    ci = lax.broadcasted_iota(jnp.int32, (ts, ts), 1)
    same = (ri // ch) == (ci // ch)
    tril = same & (ri >= ci)
    strict = same & (ri > ci)
    ri_b = lax.broadcasted_iota(jnp.int32, (GDN_INV, GDN_INV), 0)
    ci_b = lax.broadcasted_iota(jnp.int32, (GDN_INV, GDN_INV), 1)
    ident = jnp.where(ri_b == ci_b, 1.0, 0.0)
    eye_bf = eye_ref[...]

    heads = []
    for hh in range(hb):
        cs = slice(hh * dh, (hh + 1) * dh)
        qh, kh, vh = conv[0][:, cs], conv[1][:, cs], conv[2][:, cs]
        qh = qh * lax.rsqrt(jnp.sum(qh * qh, axis=-1, keepdims=True) + EPS) * (dh ** -0.5)
        kh = kh * lax.rsqrt(jnp.sum(kh * kh, axis=-1, keepdims=True) + EPS)
        bt = beta[:, hb + hh:hb + hh + 1]
        diff = g_cum[:, hh:hh + 1] - g_cum_t[hh:hh + 1, :]
        e = jnp.exp(jnp.minimum(diff, 0.0))
        kb = kh * bt
        k_bf = kh.astype(BF16)
        a_mat = _dot_nt(kb.astype(BF16), k_bf) * jnp.where(strict, e, 0.0)
        qk = (_dot_nt(qh.astype(BF16), k_bf) * jnp.where(tril, e, 0.0)).astype(BF16)
        heads.append(dict(
            a=a_mat, qk=qk,
            vb=(vh * bt).astype(BF16),
            kbg=(kb * e_cum[:, hh:hh + 1]).astype(BF16),
            qg=qh * e_cum[:, hh:hh + 1],
        ))
        kd = (kh * e_rel[:, hh:hh + 1]).astype(BF16)
        heads[-1]["kdt"] = [_dot_nt(eye_bf, kd[c * ch:(c + 1) * ch]).astype(BF16)
                            for c in range(n_ch)]

    probs = []
    for hh in range(hb):
        for blk in range(n_inv):
            bs = slice(blk * GDN_INV, (blk + 1) * GDN_INV)
            pw = -heads[hh]["a"][bs, bs]
            probs.append([pw, ident + pw])
    for _ in range(5):
        for pr in probs:
            pw_bf = pr[0].astype(BF16)
            pr[0] = _dot(pw_bf, pw_bf)
        for pr in probs:
            pr[1] = pr[1] + _dot(pr[1].astype(BF16), pr[0].astype(BF16))
    for hh in range(hb):
        u_parts, w_parts = [], []
        for blk in range(n_inv):
            bs = slice(blk * GDN_INV, (blk + 1) * GDN_INV)
            t_bf = probs[hh * n_inv + blk][1].astype(BF16)
            u_parts.append(_dot(t_bf, heads[hh]["vb"][bs]))
            w_parts.append(_dot(t_bf, heads[hh]["kbg"][bs]))
        heads[hh]["u"] = jnp.concatenate(u_parts, axis=0).astype(BF16)
        heads[hh]["w"] = jnp.concatenate(w_parts, axis=0).astype(BF16)
    for hd in heads:
        hd["qkw"] = _dot(hd["qk"], hd["w"])
        hd["qku"] = _dot(hd["qk"], hd["u"])
        hd["mw"] = [_dot(hd["kdt"][c], jnp.concatenate([hd["w"][c * ch:(c + 1) * ch],
                                                        hd["u"][c * ch:(c + 1) * ch]], axis=1))
                    for c in range(n_ch)]
    for hd in heads:
        hd["qp"] = (hd["qg"] - hd["qkw"]).astype(BF16)
        hd["mc"] = [mw[:, :dh].astype(BF16) for mw in hd["mw"]]
        hd["bc"] = [mw[:, dh:] for mw in hd["mw"]]

    states = [st_ref[hh] for hh in range(hb)]
    outs = [[] for _ in range(hb)]
    for c in range(n_ch):
        rs = slice(c * ch, (c + 1) * ch)
        for hh in range(hb):
            hd = heads[hh]
            r = _dot(jnp.concatenate([hd["qp"][rs], hd["mc"][c]], axis=0), states[hh].astype(BF16))
            outs[hh].append(r[:ch] + hd["qku"][rs])
            states[hh] = states[hh] * e_end[c * ch:c * ch + 1, hh:hh + 1] - r[ch:] + hd["bc"][c]

    for hh in range(hb):
        cs = slice(hh * dh, (hh + 1) * dh)
        st_ref[hh] = states[hh]
        o = jnp.concatenate(outs[hh], axis=0)
        o = o * lax.rsqrt(jnp.mean(o * o, axis=-1, keepdims=True) + EPS) * og_ref[...]
        z = z_ref[:, cs].astype(F32)
        o_ref[:, cs] = (o * (z * jax.nn.sigmoid(z))).astype(o_ref.dtype)


def _gdn(p3, ps3, conv_w3, alog_l, dtb_l, out_gain, shifts, eye):
    b, seq, _ = p3.shape
    ts, hb, dh = GDN_TS, GDN_HB, HEAD_DIM
    wb = hb * dh
    qb, zb = OFF_GQKV // wb, OFF_Z // wb
    gw = GDN_DIM // wb
    return pl.pallas_call(
        _gdn_kernel,
        out_shape=jax.ShapeDtypeStruct((b, seq, GDN_DIM), BF16),
        grid=(b, GDN_NHG, seq // ts),
        in_specs=[
            pl.BlockSpec((None, ts, wb), lambda i, j, k: (i, k, qb + j)),
            pl.BlockSpec((None, ts, wb), lambda i, j, k: (i, k, qb + gw + j)),
            pl.BlockSpec((None, ts, wb), lambda i, j, k: (i, k, qb + 2 * gw + j)),
            pl.BlockSpec((None, ts, wb), lambda i, j, k: (i, k, zb + j)),
            pl.BlockSpec((None, ts, 128), lambda i, j, k: (i, k, NSA_GROUPS + j)),
            pl.BlockSpec((3, GDN_CONV, wb), lambda i, j, k: (0, 0, j)),
            pl.BlockSpec((None, 1, 128), lambda i, j, k: (j, 0, 0)),
            pl.BlockSpec((None, 1, 128), lambda i, j, k: (j, 0, 0)),
            pl.BlockSpec((1, dh), lambda i, j, k: (0, 0)),
            pl.BlockSpec((GDN_CONV - 1, ts, ts), lambda i, j, k: (0, 0, 0)),
            pl.BlockSpec((dh, dh), lambda i, j, k: (0, 0)),
        ],
        out_specs=pl.BlockSpec((None, ts, wb), lambda i, j, k: (i, k, j)),
        scratch_shapes=[
            pltpu.VMEM((3, HALO, wb), F32),
            pltpu.VMEM((hb, dh, dh), F32),
        ],
        compiler_params=_cparams(("parallel", "parallel", "arbitrary")),
        name="gdn",
    )(p3, p3, p3, p3, ps3, conv_w3, alog_l, dtb_l, out_gain, shifts, eye)


def _merge_kernel(oa_ref, ob_ref, wa_ref, wb_ref, ma_ref, mb_ref, o_ref):
    ya = _dot(oa_ref[...], wa_ref[...])
    yb = _dot(ob_ref[...], wb_ref[...])
    mix = jax.nn.sigmoid(ma_ref[...].astype(F32)) * ya + jax.nn.sigmoid(mb_ref[...].astype(F32)) * yb
    o_ref[...] = mix.astype(o_ref.dtype)


def _merge(oa, ob, wa, wb, p2, bm=512, bn=1024):
    t, d = oa.shape
    n = wa.shape[1]
    ma_b, mb_b = OFF_MA // bn, OFF_MB // bn
    return pl.pallas_call(
        _merge_kernel,
        out_shape=jax.ShapeDtypeStruct((t, n), BF16),
        grid=(n // bn, t // bm),
        in_specs=[
            pl.BlockSpec((bm, d), lambda j, i: (i, 0)),
            pl.BlockSpec((bm, d), lambda j, i: (i, 0)),
            pl.BlockSpec((d, bn), lambda j, i: (0, j)),
            pl.BlockSpec((d, bn), lambda j, i: (0, j)),
            pl.BlockSpec((bm, bn), lambda j, i: (i, ma_b + j)),
            pl.BlockSpec((bm, bn), lambda j, i: (i, mb_b + j)),
        ],
        out_specs=pl.BlockSpec((bm, bn), lambda j, i: (i, j)),
        compiler_params=_cparams(("parallel", "parallel")),
        name="merge",
    )(oa, ob, wa, wb, p2, p2)


def _resid_matmul_kernel(a_ref, w_ref, r_ref, o_ref):
    o_ref[...] = r_ref[...] + _dot(a_ref[...], w_ref[...])


def _resid_matmul(a, w, resid, bm, bn):
    t, k = a.shape
    n = w.shape[1]
    return pl.pallas_call(
        _resid_matmul_kernel,
        out_shape=jax.ShapeDtypeStruct((t, n), F32),
        grid=(n // bn, t // bm),
        in_specs=[
            pl.BlockSpec((bm, k), lambda j, i: (i, 0)),
            pl.BlockSpec((k, bn), lambda j, i: (0, j)),
            pl.BlockSpec((bm, bn), lambda j, i: (i, j)),
        ],
        out_specs=pl.BlockSpec((bm, bn), lambda j, i: (i, j)),
        compiler_params=_cparams(("parallel", "parallel")),
        name="resid_matmul",
    )(a, w, resid)


def _ffn_up_kernel(x_ref, g_ref, wg_ref, wu_ref, o_ref, hn_ref):
    @pl.when(pl.program_id(1) == 0)
    def _():
        x = x_ref[...]
        hn_ref[...] = (x * lax.rsqrt(jnp.mean(x * x, axis=-1, keepdims=True) + EPS) * g_ref[...]).astype(BF16)

    hn = hn_ref[...]
    gate = _dot(hn, wg_ref[...])
    up = _dot(hn, wu_ref[...])
    o_ref[...] = (gate * jax.nn.sigmoid(gate) * up).astype(o_ref.dtype)


def _ffn_up(x1, gain, wg, wu, bm=1024, bn=512):
    t, d = x1.shape
    n = wg.shape[1]
    return pl.pallas_call(
        _ffn_up_kernel,
        out_shape=jax.ShapeDtypeStruct((t, n), BF16),
        grid=(t // bm, n // bn),
        in_specs=[
            pl.BlockSpec((bm, d), lambda i, j: (i, 0)),
            pl.BlockSpec((1, d), lambda i, j: (0, 0)),
            pl.BlockSpec((d, bn), lambda i, j: (0, j)),
            pl.BlockSpec((d, bn), lambda i, j: (0, j)),
        ],
        out_specs=pl.BlockSpec((bm, bn), lambda i, j: (i, j)),
        scratch_shapes=[pltpu.VMEM((bm, d), BF16)],
        compiler_params=_cparams(("parallel", "arbitrary")),
        name="ffn_up",
    )(x1, gain, wg, wu)


def _pack_in_weights(w):
    sizes = (NSA_Q_DIM, NSA_KV_DIM, NSA_HEADS * N_BRANCH, 3 * GDN_DIM, GDN_DIM, GDN_HEADS, GDN_HEADS, D_MODEL, D_MODEL)
    offs = np.concatenate([[0], np.cumsum(sizes)])
    seg = [w[:, offs[i]:offs[i + 1]] for i in range(len(sizes))]
    q_a, kv_a, gate_a, qkv_b, z_b, a_b, b_b, m_a, m_b = seg
    w_big = jnp.concatenate([q_a, kv_a, qkv_b, z_b, m_a, m_b], axis=1).astype(BF16)
    d = w.shape[0]
    small = []
    per_group = NSA_REP * N_BRANCH
    for g in range(NSA_GROUPS):
        small.append(jnp.pad(gate_a[:, g * per_group:(g + 1) * per_group], ((0, 0), (0, 128 - per_group))))
    for hg in range(GDN_NHG):
        hs = slice(hg * GDN_HB, (hg + 1) * GDN_HB)
        small.append(jnp.pad(jnp.concatenate([a_b[:, hs], b_b[:, hs]], axis=1), ((0, 0), (0, 128 - 2 * GDN_HB))))
    w_small = jnp.concatenate(small, axis=1).astype(BF16)
    assert w_big.shape == (d, NP_BIG) and w_small.shape == (d, NP_SMALL)
    return w_big, w_small


def _lane_rows(v):
    return jnp.pad(v.reshape(GDN_NHG, 1, GDN_HB).astype(F32), ((0, 0), (0, 0), (0, 128 - GDN_HB)))


def _nsa_constants(seq):
    n_cmp = (seq - CMP_BLOCK) // CMP_STRIDE + 1
    nc = seq // CMP_STRIDE
    n_sel = seq // SLC_BLOCK
    cmp_start = np.arange(n_cmp) * CMP_STRIDE
    sel_start = np.arange(n_sel) * SLC_BLOCK
    overlap = (np.minimum(cmp_start[:, None] + CMP_BLOCK, sel_start[None, :] + SLC_BLOCK)
               - np.maximum(cmp_start[:, None], sel_start[None, :]))
    agg = np.zeros((nc, n_sel), np.float32)
    agg[:n_cmp] = np.clip(overlap, 0, None) / CMP_BLOCK
    eye = np.eye(NSA_TQ, dtype=np.float32)
    i = np.arange(NSA_TQ)[:, None]
    c = np.arange(WINDOW + NSA_TQ)[None, :]
    win_bias = np.where((c > i) & (c <= i + WINDOW), 0.0, NEG_INF).astype(np.float32)
    return jnp.asarray(agg.T, BF16), jnp.asarray(eye, BF16), jnp.asarray(win_bias)


def _gdn_constants():
    idx = np.arange(GDN_TS)
    shifts = np.stack([(idx[None, :] == idx[:, None] - d) for d in range(1, GDN_CONV)]).astype(np.float32)
    return jnp.asarray(shifts, BF16), jnp.asarray(np.eye(HEAD_DIM, dtype=np.float32), BF16)


def _layer(x, attn_norm, w_in, nsa_q_norm, nsa_k_norm, cmp_pos, w_cmp, gdn_conv, gdn_a_log, gdn_dt_bias,
           gdn_out_norm, w_branch_a, w_branch_b, w_out, ffn_norm, w_gate, w_up, w_down):
    b, seq, d = x.shape
    t = b * seq
    x2 = x.reshape(t, d)

    w_big, w_small = _pack_in_weights(w_in)
    p2, ps2 = _in_proj(x2, attn_norm.reshape(1, d), w_big, w_small)
    p3 = p2.reshape(b, seq, NP_BIG)
    ps3 = ps2.reshape(b, seq, NP_SMALL)

    nc = seq // CMP_STRIDE
    kv0 = p3[:, :, OFF_KV:OFF_KV + 2 * NSA_GROUPS * HEAD_DIM]
    ckv = kv0.reshape(b, nc, CMP_STRIDE, 2 * NSA_GROUPS, HEAD_DIM).transpose(0, 3, 1, 2, 4)
    ckv = ckv.reshape(b, 2 * NSA_GROUPS, nc, CMP_STRIDE * HEAD_DIM)
    w_cmp2 = w_cmp.reshape(2, CMP_BLOCK * HEAD_DIM, HEAD_DIM).astype(BF16)
    pos2 = jnp.broadcast_to(cmp_pos.reshape(2, 1, CMP_BLOCK * HEAD_DIM), (2, 8, CMP_BLOCK * HEAD_DIM)).astype(BF16)
    kc, vc = _nsa_compress(ckv, w_cmp2, pos2, nsa_k_norm[0:1])

    agg_t, eye_q, win_bias = _nsa_constants(seq)
    o_a = _nsa_attention(p3, ps3, kc, vc, agg_t, eye_q, win_bias, nsa_q_norm.reshape(1, HEAD_DIM), nsa_k_norm)

    shifts, eye_h = _gdn_constants()
    conv_w3 = gdn_conv.reshape(GDN_CONV, 3, GDN_DIM).transpose(1, 0, 2)
    o_b = _gdn(p3, ps3, conv_w3, _lane_rows(gdn_a_log), _lane_rows(gdn_dt_bias),
               gdn_out_norm.reshape(1, HEAD_DIM), shifts, eye_h)

    mix = _merge(o_a.reshape(t, NSA_Q_DIM), o_b.reshape(t, GDN_DIM), w_branch_a.astype(BF16),
                 w_branch_b.astype(BF16), p2)
    x1 = _resid_matmul(mix, w_out.astype(BF16), x2, bm=1024, bn=1024)

    act = _ffn_up(x1, ffn_norm.reshape(1, d), w_gate.astype(BF16), w_up.astype(BF16))
    out = _resid_matmul(act, w_down.astype(BF16), x1, bm=512, bn=1024)
    return out.reshape(b, seq, d)


def kernel(x, attn_norm, w_in, nsa_q_norm, nsa_k_norm, cmp_pos, w_cmp, gdn_conv, gdn_a_log, gdn_dt_bias,
           gdn_out_norm, w_branch_a, w_branch_b, w_out, ffn_norm, w_gate, w_up, w_down):
    for l in range(attn_norm.shape[0]):
        x = _layer(x, attn_norm[l], w_in[l], nsa_q_norm[l], nsa_k_norm[l], cmp_pos[l], w_cmp[l], gdn_conv[l],
                   gdn_a_log[l], gdn_dt_bias[l], gdn_out_norm[l], w_branch_a[l], w_branch_b[l], w_out[l],
                   ffn_norm[l], w_gate[l], w_up[l], w_down[l])
    return x
```

```python
import functools

import numpy as np
import jax
import jax.numpy as jnp
from jax import lax
from jax.experimental import pallas as pl
from jax.experimental.pallas import tpu as pltpu

F32 = jnp.float32
BF16 = jnp.bfloat16

D_MODEL = 2048
EPS = 1e-6
NEG_INF = -1e30
TINY = 1e-30
FORCE_SCORE = 1e4
LOG2E = 1.4426950408889634

NSA_HEADS = 16
NSA_GROUPS = 4
NSA_REP = NSA_HEADS // NSA_GROUPS
HEAD_DIM = 128
N_BRANCH = 3
CMP_BLOCK = 32
CMP_STRIDE = 16
SLC_BLOCK = 64
SLC_TOPK = 16
WINDOW = 512

GDN_HEADS = 16
GDN_CONV = 4
GDN_CHUNK = 64

D_FF = 5632

NSA_Q_DIM = NSA_HEADS * HEAD_DIM
NSA_KV_DIM = N_BRANCH * 2 * NSA_GROUPS * HEAD_DIM
GDN_DIM = GDN_HEADS * HEAD_DIM
OFF_Q = 0
OFF_KV = OFF_Q + NSA_Q_DIM
OFF_GQKV = OFF_KV + NSA_KV_DIM
OFF_Z = OFF_GQKV + 3 * GDN_DIM
OFF_MA = OFF_Z + GDN_DIM
OFF_MB = OFF_MA + D_MODEL
NP_BIG = OFF_MB + D_MODEL

GDN_HB = 8
GDN_NHG = GDN_HEADS // GDN_HB
NP_SMALL = (NSA_GROUPS + GDN_NHG) * 128

VMEM_LIMIT = 56 * 1024 * 1024

NT_DIMS = (((1,), (1,)), ((), ()))


def _dot(a, b):
    return jnp.dot(a, b, preferred_element_type=F32)


def _dot_nt(a, b):
    return lax.dot_general(a, b, NT_DIMS, preferred_element_type=F32)


def _cparams(sem):
    return pltpu.CompilerParams(dimension_semantics=sem, vmem_limit_bytes=VMEM_LIMIT)


CMP_KV_COLS = 2 * NSA_GROUPS * HEAD_DIM


def _in_proj_kernel(x_ref, g_ref, w_ref, ws_ref, o_ref, os_ref, okv_ref, hn_ref, *, j_kv, n_kv):
    j = pl.program_id(1)

    @pl.when(j == 0)
    def _():
        x = x_ref[...]
        y = x * lax.rsqrt(jnp.mean(x * x, axis=-1, keepdims=True) + EPS) * g_ref[...]
        hn = y.astype(BF16)
        hn_ref[...] = hn
        os_ref[...] = _dot(hn, ws_ref[...])

    res = _dot(hn_ref[...], w_ref[...]).astype(o_ref.dtype)
    o_ref[...] = res

    @pl.when((j >= j_kv) & (j < j_kv + n_kv))
    def _():
        for c in range(okv_ref.shape[0]):
            okv_ref[c] = res[:, c * HEAD_DIM:(c + 1) * HEAD_DIM]


def _in_proj(x2, gain, w_big, w_small, bm=1024, bn=512):
    t, d = x2.shape
    n = w_big.shape[1]
    ns = w_small.shape[1]
    gpb = bn // HEAD_DIM
    j_kv, n_kv = OFF_KV // bn, CMP_KV_COLS // bn
    assert OFF_KV % bn == 0 and CMP_KV_COLS % bn == 0
    return pl.pallas_call(
        functools.partial(_in_proj_kernel, j_kv=j_kv, n_kv=n_kv),
        out_shape=(jax.ShapeDtypeStruct((t, n), BF16), jax.ShapeDtypeStruct((t, ns), F32),
                   jax.ShapeDtypeStruct((CMP_KV_COLS // HEAD_DIM, t, HEAD_DIM), BF16)),
        grid=(t // bm, n // bn),
        in_specs=[
            pl.BlockSpec((bm, d), lambda i, j: (i, 0)),
            pl.BlockSpec((1, d), lambda i, j: (0, 0)),
            pl.BlockSpec((d, bn), lambda i, j: (0, j)),
            pl.BlockSpec((d, ns), lambda i, j: (0, 0)),
        ],
        out_specs=(
            pl.BlockSpec((bm, bn), lambda i, j: (i, j)),
            pl.BlockSpec((bm, ns), lambda i, j: (i, 0)),
            pl.BlockSpec((gpb, bm, HEAD_DIM), lambda i, j: (jnp.clip(j - j_kv, 0, n_kv - 1), i, 0)),
        ),
        scratch_shapes=[pltpu.VMEM((bm, d), BF16)],
        compiler_params=_cparams(("parallel", "arbitrary")),
        name="in_proj",
    )(x2, gain, w_big, w_small)


def _cmp_kernel(ck_ref, cv_ref, w_ref, pos_ref, kg_ref, kc_ref, vc_ref):
    half = CMP_STRIDE * HEAD_DIM

    def compress(c_ref, idx):
        c = c_ref[...]
        w = w_ref[idx]
        lo = _dot(c, w[:half])
        hi = _dot(c, w[half:])
        bias = _dot(pos_ref[idx], w)[0:1]
        n = lo.shape[0]
        return lo + pltpu.roll(hi, n - 1, 0) + bias

    kc = compress(ck_ref, 0)
    kc = kc * lax.rsqrt(jnp.mean(kc * kc, axis=-1, keepdims=True) + EPS) * kg_ref[...]
    kc_ref[...] = kc.astype(BF16)
    vc_ref[...] = compress(cv_ref, 1).astype(BF16)


def _nsa_compress(ckv, w_cmp2, pos2, k_gain0):
    _, b, nc, width = ckv.shape
    g = NSA_GROUPS
    out = jax.ShapeDtypeStruct((b, g, nc, HEAD_DIM), BF16)
    return pl.pallas_call(
        _cmp_kernel,
        out_shape=(out, out),
        grid=(b, g),
        in_specs=[
            pl.BlockSpec((None, None, nc, width), lambda i, j: (j, i, 0, 0)),
            pl.BlockSpec((None, None, nc, width), lambda i, j: (j + NSA_GROUPS, i, 0, 0)),
            pl.BlockSpec((2, 2 * width, HEAD_DIM), lambda i, j: (0, 0, 0)),
            pl.BlockSpec((2, 8, 2 * width), lambda i, j: (0, 0, 0)),
            pl.BlockSpec((1, HEAD_DIM), lambda i, j: (0, 0)),
        ],
        out_specs=(
            pl.BlockSpec((None, None, nc, HEAD_DIM), lambda i, j: (i, j, 0, 0)),
            pl.BlockSpec((None, None, nc, HEAD_DIM), lambda i, j: (i, j, 0, 0)),
        ),
        compiler_params=_cparams(("parallel", "parallel")),
        name="nsa_compress",
    )(ckv, ckv, w_cmp2, pos2, k_gain0)


NSA_TQ = 256
NSA_TK = 512
NORM_ROWS = 512
RANK_UNROLL = 4


def _rms(xf, gain):
    return xf * lax.rsqrt(jnp.mean(xf * xf, axis=-1, keepdims=True) + EPS) * gain


def _nsa_kernel(q_ref, gl_ref, ks_ref, vs_ref, kw_ref, vw_ref, kc_ref, vc_ref, aggt_ref, eye_ref, wb_ref,
                qg_ref, kg_ref, o_ref, ksa_ref, kwa_ref, vsa_ref, vwp_ref, sc_ref, s0_ref, s1_ref, m_ref, acc_ref,
                *, seq):
    tq, tk, rep, dh = NSA_TQ, NSA_TK, NSA_REP, HEAD_DIM
    rows = rep * tq
    n_sel = seq // SLC_BLOCK
    n_topk = min(SLC_TOPK, n_sel)
    qi = pl.program_id(2)
    q0 = qi * tq

    @pl.when(qi == 0)
    def _():
        kwa_ref[0:WINDOW, 0:dh] = jnp.zeros((WINDOW, dh), BF16)
        kwa_ref[0:WINDOW, dh:2 * dh] = jnp.where(
            lax.broadcasted_iota(jnp.int32, (WINDOW, dh), 1) == 0, 1.0, 0.0).astype(BF16)
        ones = jnp.ones((NORM_ROWS, dh), BF16)
        vwp_ref[0:WINDOW, 0:dh] = jnp.zeros((WINDOW, dh), BF16)
        vwp_ref[0:WINDOW, dh:2 * dh] = jnp.ones((WINDOW, dh), BF16)
        lane = lax.broadcasted_iota(jnp.int32, (NORM_ROWS, dh), 1)
        sub = lax.broadcasted_iota(jnp.int32, (NORM_ROWS, dh), 0)

        def body(c, carry):
            r0 = pl.multiple_of(c * NORM_ROWS, NORM_ROWS)
            r = pl.ds(r0, NORM_ROWS)
            rw = pl.ds(r0 + WINDOW, NORM_ROWS)
            ksa_ref[r, 0:dh] = _rms(ks_ref[r, :].astype(F32), kg_ref[1:2, :]).astype(BF16)
            ksa_ref[r, dh:2 * dh] = jnp.where((r0 + sub) // SLC_BLOCK == lane, 1.0, 0.0).astype(BF16)
            kwa_ref[rw, 0:dh] = _rms(kw_ref[r, :].astype(F32), kg_ref[2:3, :]).astype(BF16)
            kwa_ref[rw, dh:2 * dh] = jnp.zeros((NORM_ROWS, dh), BF16)
            vsa_ref[r, 0:dh] = vs_ref[r, :]
            vsa_ref[r, dh:2 * dh] = ones
            vwp_ref[rw, 0:dh] = vw_ref[r, :]
            vwp_ref[rw, dh:2 * dh] = ones
            return carry
        lax.fori_loop(0, seq // NORM_ROWS, body, 0)

    qscale = dh ** -0.5 * LOG2E
    q = q_ref[...].astype(F32)
    qs = jnp.concatenate(
        [(_rms(q[:, r * dh:(r + 1) * dh], qg_ref[...]) * qscale).astype(BF16) for r in range(rep)], axis=0)

    wk = WINDOW + tq
    padneg = jnp.where(lax.broadcasted_iota(jnp.int32, (rows, dh), 1) == 0, NEG_INF, 0.0).astype(BF16)
    qa_win = jnp.concatenate([qs, padneg], axis=1)
    kr = pl.ds(pl.multiple_of(q0, tq), wk)
    s_w = _dot_nt(qa_win, kwa_ref[kr, :]) + jnp.concatenate([wb_ref[...]] * rep, axis=0)

    n_cmp = kc_ref.shape[0]
    s = _dot_nt(qs, kc_ref[...])
    p_w = jnp.exp2(s_w - jnp.max(s_w, axis=-1, keepdims=True))
    ov_w = _dot(p_w.astype(BF16), vwp_ref[kr, :])
    o_win = ov_w[:, :dh] / ov_w[:, dh:]
    t_c = q0 + (lax.broadcasted_iota(jnp.int32, (rows, n_cmp), 0) & (tq - 1))
    cmp_end = lax.broadcasted_iota(jnp.int32, (rows, n_cmp), 1) * CMP_STRIDE + (CMP_BLOCK - 1)
    mask = cmp_end <= t_c
    s = jnp.where(mask, s, NEG_INF)
    e = jnp.where(mask, jnp.exp2(s - jnp.max(s, axis=-1, keepdims=True)), 0.0)
    vc_aug = jnp.concatenate([vc_ref[...], jnp.ones((n_cmp, dh), BF16)], axis=1)
    ov = _dot(e.astype(BF16), vc_aug)
    den = jnp.maximum(ov[:, dh:], TINY)
    o_cmp = ov[:, :dh] / den
    p = e / jnp.concatenate([den] * (n_cmp // dh), axis=1)
    p_sum = p[0:tq]
    for r in range(1, rep):
        p_sum = p_sum + p[r * tq:(r + 1) * tq]
    imp_t = _dot_nt(aggt_ref[...], p_sum.astype(BF16))

    jidx = lax.broadcasted_iota(jnp.int32, (n_sel, tq), 0)
    t_l = q0 + lax.broadcasted_iota(jnp.int32, (n_sel, tq), 1)
    cur = t_l // SLC_BLOCK
    forced = (jidx == 0) | (jidx == cur) | (jidx == cur - 1)
    causal = jidx * SLC_BLOCK <= t_l
    score = jnp.where(forced, FORCE_SCORE, jnp.where(causal, imp_t, -FORCE_SCORE))
    sc_ref[...] = score

    def rank_body(it, rank):
        for u in range(RANK_UNROLL):
            i = it * RANK_UNROLL + u
            row = sc_ref[pl.ds(i, 1), :]
            ge = jnp.where(row >= score, 1.0, 0.0)
            gt = jnp.where(row > score, 1.0, 0.0)
            rank = rank + jnp.where(jidx > i, ge, gt)
        return rank

    n_blk = (q0 + tq - 1) // SLC_BLOCK + 1
    n_it = (n_blk + RANK_UNROLL - 1) // RANK_UNROLL
    rank = lax.fori_loop(0, n_it, rank_body, jnp.zeros((n_sel, tq), F32))
    selneg_t = jnp.where(rank < n_topk, 0.0, NEG_INF)
    if n_sel < dh:
        selneg_t = jnp.concatenate([selneg_t, jnp.zeros((dh - n_sel, tq), F32)], axis=0)
    selneg = _dot_nt(eye_ref[...], selneg_t.astype(BF16)).astype(BF16)
    qa_sel = jnp.concatenate([qs, jnp.concatenate([selneg] * rep, axis=0)], axis=1)

    def qk_tile(kt):
        return _dot_nt(qa_sel, ksa_ref[pl.ds(pl.multiple_of(kt * tk, tk), tk), :])

    def consume(kt, buf, diagonal):
        k0 = pl.multiple_of(kt * tk, tk)
        s = buf[...]
        if diagonal:
            t_r = q0 + (lax.broadcasted_iota(jnp.int32, (rows, tk), 0) & (tq - 1))
            kpos = k0 + lax.broadcasted_iota(jnp.int32, (rows, tk), 1)
            s = jnp.where(kpos <= t_r, s, NEG_INF)
        m_prev = m_ref[...]
        m_new = jnp.maximum(m_prev, jnp.max(s, axis=-1, keepdims=True))
        p = jnp.exp2(s - jnp.concatenate([m_new] * (tk // 128), axis=1))
        alpha = jnp.exp2(m_prev - m_new)
        m_ref[...] = m_new
        acc_ref[...] = (jnp.concatenate([alpha, alpha], axis=1) * acc_ref[...]
                        + _dot(p.astype(BF16), vsa_ref[pl.ds(k0, tk), :]))

    m_ref[...] = jnp.full((rows, 128), NEG_INF, F32)
    acc_ref[...] = jnp.zeros((rows, 2 * dh), F32)
    s0_ref[...] = qk_tile(0)
    n_full = q0 // tk

    def sel_body(it, carry):
        kt = 2 * it
        s1_ref[...] = qk_tile(kt + 1)
        consume(kt, s0_ref, False)
        s0_ref[...] = qk_tile(kt + 2)
        consume(kt + 1, s1_ref, False)
        return carry

    lax.fori_loop(0, n_full // 2, sel_body, 0)

    @pl.when(n_full % 2 == 1)
    def _():
        s1_ref[...] = qk_tile(n_full)
        consume(n_full - 1, s0_ref, False)
        consume(n_full, s1_ref, True)

    @pl.when(n_full % 2 == 0)
    def _():
        consume(n_full, s0_ref, True)

    o_slc = acc_ref[:, 0:dh] / acc_ref[:, dh:2 * dh]

    gates = jax.nn.sigmoid(gl_ref[...])
    for r in range(rep):
        rs = slice(r * tq, (r + 1) * tq)
        o = (gates[:, 3 * r:3 * r + 1] * o_cmp[rs] + gates[:, 3 * r + 1:3 * r + 2] * o_slc[rs]
             + gates[:, 3 * r + 2:3 * r + 3] * o_win[rs])
        o_ref[:, r * dh:(r + 1) * dh] = o.astype(o_ref.dtype)


def _nsa_attention(p3, ps3, kc, vc, agg_t, eye, win_bias, q_gain, k_gain):
    b, seq, _ = p3.shape
    g, tq, dh = NSA_GROUPS, NSA_TQ, HEAD_DIM
    n_sel = seq // SLC_BLOCK
    assert n_sel <= dh and n_sel % 8 == 0 and seq % NSA_TK == 0
    nc = kc.shape[2]
    kvb = OFF_KV // dh

    def kv_spec(branch, is_v):
        base = kvb + (branch * 2 + is_v) * g
        return pl.BlockSpec((None, seq, dh), lambda i, j, k: (i, 0, base + j))

    qw = NSA_REP * dh
    return pl.pallas_call(
        functools.partial(_nsa_kernel, seq=seq),
        out_shape=jax.ShapeDtypeStruct((b, seq, NSA_Q_DIM), BF16),
        grid=(b, g, seq // tq),
        in_specs=[
            pl.BlockSpec((None, tq, qw), lambda i, j, k: (i, k, j)),
            pl.BlockSpec((None, tq, 128), lambda i, j, k: (i, k, j)),
            kv_spec(1, 0), kv_spec(1, 1), kv_spec(2, 0), kv_spec(2, 1),
            pl.BlockSpec((None, None, nc, dh), lambda i, j, k: (i, j, 0, 0)),
            pl.BlockSpec((None, None, nc, dh), lambda i, j, k: (i, j, 0, 0)),
            pl.BlockSpec((n_sel, nc), lambda i, j, k: (0, 0)),
            pl.BlockSpec((tq, tq), lambda i, j, k: (0, 0)),
            pl.BlockSpec((tq, WINDOW + tq), lambda i, j, k: (0, 0)),
            pl.BlockSpec((1, dh), lambda i, j, k: (0, 0)),
            pl.BlockSpec((N_BRANCH, dh), lambda i, j, k: (0, 0)),
        ],
        out_specs=pl.BlockSpec((None, tq, qw), lambda i, j, k: (i, k, j)),
        scratch_shapes=[
            pltpu.VMEM((seq, 2 * dh), BF16),
            pltpu.VMEM((seq + WINDOW, 2 * dh), BF16),
            pltpu.VMEM((seq, 2 * dh), BF16),
            pltpu.VMEM((seq + WINDOW, 2 * dh), BF16),
            pltpu.VMEM((n_sel, tq), F32),
            pltpu.VMEM((NSA_REP * tq, NSA_TK), F32),
            pltpu.VMEM((NSA_REP * tq, NSA_TK), F32),
            pltpu.VMEM((NSA_REP * tq, 128), F32),
            pltpu.VMEM((NSA_REP * tq, 2 * dh), F32),
        ],
        compiler_params=_cparams(("parallel", "parallel", "arbitrary")),
        name="nsa_attention",
    )(p3, ps3, p3, p3, p3, p3, kc, vc, agg_t, eye, win_bias, q_gain, k_gain)


GDN_TS = 256
GDN_INV = 128
HALO = 8


def _gdn_kernel(q_ref, k_ref, v_ref, z_ref, ab_ref, cw_ref, alog_ref, dtb_ref, og_ref, sh_ref, eye_ref,
                o_ref, xe_ref, st_ref):
    ts, hb, dh, ch = GDN_TS, GDN_HB, HEAD_DIM, GDN_CHUNK
    ti = pl.program_id(2)

    n_inv = ts // GDN_INV
    n_ch = ts // ch

    @pl.when(ti == 0)
    def _():
        xe_ref[...] = jnp.zeros_like(xe_ref)
        st_ref[...] = jnp.zeros_like(st_ref)

    row8 = lax.broadcasted_iota(jnp.int32, (HALO, hb * dh), 0)
    conv = []
    for part, ref in enumerate((q_ref, k_ref, v_ref)):
        xb = ref[...]
        xf = xb.astype(F32)
        prev = xe_ref[part]
        acc = cw_ref[part, GDN_CONV - 1:GDN_CONV, :] * xf
        corr = jnp.zeros((HALO, hb * dh), F32)
        for d in range(1, GDN_CONV):
            wd = cw_ref[part, GDN_CONV - 1 - d:GDN_CONV - d, :]
            acc = acc + wd * _dot(sh_ref[d - 1], xb)
            corr = corr + wd * jnp.where(row8 < d, pltpu.roll(prev, d, 0), 0.0)
        acc = jnp.concatenate([acc[0:HALO] + corr, acc[HALO:]], axis=0)
        conv.append(acc * jax.nn.sigmoid(acc))
        xe_ref[part] = xf[ts - HALO:ts]

    ab = ab_ref[...]
    xg = ab + dtb_ref[...]
    softplus = jnp.maximum(xg, 0.0) + jnp.log(1.0 + jnp.exp(-jnp.abs(xg)))
    g_cum = -jnp.exp(alog_ref[...]) * softplus
    beta = jax.nn.sigmoid(ab)
    row_in_chunk = lax.broadcasted_iota(jnp.int32, (ts, 128), 0) & (ch - 1)
    step = 1
    while step < ch:
        g_cum = g_cum + jnp.where(row_in_chunk >= step, pltpu.roll(g_cum, step, 0), 0.0)
        step *= 2
    g_end = jnp.concatenate(
        [jnp.broadcast_to(g_cum[(c + 1) * ch - 1:(c + 1) * ch, :], (ch, 128)) for c in range(n_ch)], axis=0)
    e_cum = jnp.exp(g_cum)
    e_rel = jnp.exp(g_end - g_cum)
    e_end = jnp.exp(g_end)
    g_cum_t = jnp.transpose(g_cum)

    ri = lax.broadcasted_iota(jnp.int32, (ts, ts), 0)
    ci = lax.broadcasted_iota(jnp.int32, (ts, ts), 1)
    same = (ri // ch) == (ci // ch)
    tril = same & (ri >= ci)
    strict = same & (ri > ci)
    ri_b = lax.broadcasted_iota(jnp.int32, (GDN_INV, GDN_INV), 0)
    ci_b = lax.broadcasted_iota(jnp.int32, (GDN_INV, GDN_INV), 1)
    ident = jnp.where(ri_b == ci_b, 1.0, 0.0)
    eye_bf = eye_ref[...]

    heads = []
    for hh in range(hb):
        cs = slice(hh * dh, (hh + 1) * dh)
        qh, kh, vh = conv[0][:, cs], conv[1][:, cs], conv[2][:, cs]
        qh = qh * lax.rsqrt(jnp.sum(qh * qh, axis=-1, keepdims=True) + EPS) * (dh ** -0.5)
        kh = kh * lax.rsqrt(jnp.sum(kh * kh, axis=-1, keepdims=True) + EPS)
        bt = beta[:, hb + hh:hb + hh + 1]
        diff = g_cum[:, hh:hh + 1] - g_cum_t[hh:hh + 1, :]
        e = jnp.exp(jnp.minimum(diff, 0.0))
        kb = kh * bt
        k_bf = kh.astype(BF16)
        a_mat = _dot_nt(kb.astype(BF16), k_bf) * jnp.where(strict, e, 0.0)
        qk = (_dot_nt(qh.astype(BF16), k_bf) * jnp.where(tril, e, 0.0)).astype(BF16)
        heads.append(dict(
            a=a_mat, qk=qk,
            vb=(vh * bt).astype(BF16),
            kbg=(kb * e_cum[:, hh:hh + 1]).astype(BF16),
            qg=qh * e_cum[:, hh:hh + 1],
        ))
        kd = (kh * e_rel[:, hh:hh + 1]).astype(BF16)
        heads[-1]["kdt"] = [_dot_nt(eye_bf, kd[c * ch:(c + 1) * ch]).astype(BF16)
                            for c in range(n_ch)]

    probs = []
    for hh in range(hb):
        for blk in range(n_inv):
            bs = slice(blk * GDN_INV, (blk + 1) * GDN_INV)
            pw = -heads[hh]["a"][bs, bs]
            probs.append([pw, ident + pw])
    for _ in range(5):
        for pr in probs:
            pw_bf = pr[0].astype(BF16)
            pr[0] = _dot(pw_bf, pw_bf)
        for pr in probs:
            pr[1] = pr[1] + _dot(pr[1].astype(BF16), pr[0].astype(BF16))
    for hh in range(hb):
        u_parts, w_parts = [], []
        for blk in range(n_inv):
            bs = slice(blk * GDN_INV, (blk + 1) * GDN_INV)
            t_bf = probs[hh * n_inv + blk][1].astype(BF16)
            u_parts.append(_dot(t_bf, heads[hh]["vb"][bs]))
            w_parts.append(_dot(t_bf, heads[hh]["kbg"][bs]))
        heads[hh]["u"] = jnp.concatenate(u_parts, axis=0).astype(BF16)
        heads[hh]["w"] = jnp.concatenate(w_parts, axis=0).astype(BF16)
    for hd in heads:
        hd["qkw"] = _dot(hd["qk"], hd["w"])
        hd["qku"] = _dot(hd["qk"], hd["u"])
        hd["mw"] = [_dot(hd["kdt"][c], jnp.concatenate([hd["w"][c * ch:(c + 1) * ch],
                                                        hd["u"][c * ch:(c + 1) * ch]], axis=1))
                    for c in range(n_ch)]
    for hd in heads:
        hd["qp"] = (hd["qg"] - hd["qkw"]).astype(BF16)
        hd["mc"] = [mw[:, :dh].astype(BF16) for mw in hd["mw"]]
        hd["bc"] = [mw[:, dh:] for mw in hd["mw"]]

    states = [st_ref[hh] for hh in range(hb)]
    outs = [[] for _ in range(hb)]
    for c in range(n_ch):
        rs = slice(c * ch, (c + 1) * ch)
        for hh in range(hb):
            hd = heads[hh]
            r = _dot(jnp.concatenate([hd["qp"][rs], hd["mc"][c]], axis=0), states[hh].astype(BF16))
            outs[hh].append(r[:ch] + hd["qku"][rs])
            states[hh] = states[hh] * e_end[c * ch:c * ch + 1, hh:hh + 1] - r[ch:] + hd["bc"][c]

    for hh in range(hb):
        cs = slice(hh * dh, (hh + 1) * dh)
        st_ref[hh] = states[hh]
        o = jnp.concatenate(outs[hh], axis=0)
        o = o * lax.rsqrt(jnp.mean(o * o, axis=-1, keepdims=True) + EPS) * og_ref[...]
        z = z_ref[:, cs].astype(F32)
        o_ref[:, cs] = (o * (z * jax.nn.sigmoid(z))).astype(o_ref.dtype)


def _gdn(p3, ps3, conv_w3, alog_l, dtb_l, out_gain, shifts, eye):
    b, seq, _ = p3.shape
    ts, hb, dh = GDN_TS, GDN_HB, HEAD_DIM
    wb = hb * dh
    qb, zb = OFF_GQKV // wb, OFF_Z // wb
    gw = GDN_DIM // wb
    return pl.pallas_call(
        _gdn_kernel,
        out_shape=jax.ShapeDtypeStruct((b, seq, GDN_DIM), BF16),
        grid=(b, GDN_NHG, seq // ts),
        in_specs=[
            pl.BlockSpec((None, ts, wb), lambda i, j, k: (i, k, qb + j)),
            pl.BlockSpec((None, ts, wb), lambda i, j, k: (i, k, qb + gw + j)),
            pl.BlockSpec((None, ts, wb), lambda i, j, k: (i, k, qb + 2 * gw + j)),
            pl.BlockSpec((None, ts, wb), lambda i, j, k: (i, k, zb + j)),
            pl.BlockSpec((None, ts, 128), lambda i, j, k: (i, k, NSA_GROUPS + j)),
            pl.BlockSpec((3, GDN_CONV, wb), lambda i, j, k: (0, 0, j)),
            pl.BlockSpec((None, 1, 128), lambda i, j, k: (j, 0, 0)),
            pl.BlockSpec((None, 1, 128), lambda i, j, k: (j, 0, 0)),
            pl.BlockSpec((1, dh), lambda i, j, k: (0, 0)),
            pl.BlockSpec((GDN_CONV - 1, ts, ts), lambda i, j, k: (0, 0, 0)),
            pl.BlockSpec((dh, dh), lambda i, j, k: (0, 0)),
        ],
        out_specs=pl.BlockSpec((None, ts, wb), lambda i, j, k: (i, k, j)),
        scratch_shapes=[
            pltpu.VMEM((3, HALO, wb), F32),
            pltpu.VMEM((hb, dh, dh), F32),
        ],
        compiler_params=_cparams(("parallel", "parallel", "arbitrary")),
        name="gdn",
    )(p3, p3, p3, p3, ps3, conv_w3, alog_l, dtb_l, out_gain, shifts, eye)


def _merge_kernel(oa_ref, ob_ref, wa_ref, wb_ref, ma_ref, mb_ref, o_ref):
    ya = _dot(oa_ref[...], wa_ref[...])
    yb = _dot(ob_ref[...], wb_ref[...])
    mix = jax.nn.sigmoid(ma_ref[...].astype(F32)) * ya + jax.nn.sigmoid(mb_ref[...].astype(F32)) * yb
    o_ref[...] = mix.astype(o_ref.dtype)


def _merge(oa, ob, wa, wb, p2, bm=512, bn=1024):
    t, d = oa.shape
    n = wa.shape[1]
    ma_b, mb_b = OFF_MA // bn, OFF_MB // bn
    return pl.pallas_call(
        _merge_kernel,
        out_shape=jax.ShapeDtypeStruct((t, n), BF16),
        grid=(n // bn, t // bm),
        in_specs=[
            pl.BlockSpec((bm, d), lambda j, i: (i, 0)),
            pl.BlockSpec((bm, d), lambda j, i: (i, 0)),
            pl.BlockSpec((d, bn), lambda j, i: (0, j)),
            pl.BlockSpec((d, bn), lambda j, i: (0, j)),
            pl.BlockSpec((bm, bn), lambda j, i: (i, ma_b + j)),
            pl.BlockSpec((bm, bn), lambda j, i: (i, mb_b + j)),
        ],
        out_specs=pl.BlockSpec((bm, bn), lambda j, i: (i, j)),
        compiler_params=_cparams(("parallel", "parallel")),
        name="merge",
    )(oa, ob, wa, wb, p2, p2)


def _resid_matmul_kernel(a_ref, w_ref, r_ref, o_ref):
    o_ref[...] = r_ref[...] + _dot(a_ref[...], w_ref[...])


def _resid_matmul(a, w, resid, bm, bn):
    t, k = a.shape
    n = w.shape[1]
    return pl.pallas_call(
        _resid_matmul_kernel,
        out_shape=jax.ShapeDtypeStruct((t, n), F32),
        grid=(n // bn, t // bm),
        in_specs=[
            pl.BlockSpec((bm, k), lambda j, i: (i, 0)),
            pl.BlockSpec((k, bn), lambda j, i: (0, j)),
            pl.BlockSpec((bm, bn), lambda j, i: (i, j)),
        ],
        out_specs=pl.BlockSpec((bm, bn), lambda j, i: (i, j)),
        compiler_params=_cparams(("parallel", "parallel")),
        name="resid_matmul",
    )(a, w, resid)


def _ffn_up_kernel(x_ref, g_ref, wg_ref, wu_ref, o_ref, hn_ref):
    @pl.when(pl.program_id(1) == 0)
    def _():
        x = x_ref[...]
        hn_ref[...] = (x * lax.rsqrt(jnp.mean(x * x, axis=-1, keepdims=True) + EPS) * g_ref[...]).astype(BF16)

    hn = hn_ref[...]
    gate = _dot(hn, wg_ref[...])
    up = _dot(hn, wu_ref[...])
    o_ref[...] = (gate * jax.nn.sigmoid(gate) * up).astype(o_ref.dtype)


def _ffn_up(x1, gain, wg, wu, bm=1024, bn=512):
    t, d = x1.shape
    n = wg.shape[1]
    return pl.pallas_call(
        _ffn_up_kernel,
        out_shape=jax.ShapeDtypeStruct((t, n), BF16),
        grid=(t // bm, n // bn),
        in_specs=[
            pl.BlockSpec((bm, d), lambda i, j: (i, 0)),
            pl.BlockSpec((1, d), lambda i, j: (0, 0)),
            pl.BlockSpec((d, bn), lambda i, j: (0, j)),
            pl.BlockSpec((d, bn), lambda i, j: (0, j)),
        ],
        out_specs=pl.BlockSpec((bm, bn), lambda i, j: (i, j)),
        scratch_shapes=[pltpu.VMEM((bm, d), BF16)],
        compiler_params=_cparams(("parallel", "arbitrary")),
        name="ffn_up",
    )(x1, gain, wg, wu)


def _pack_in_weights(w):
    sizes = (NSA_Q_DIM, NSA_KV_DIM, NSA_HEADS * N_BRANCH, 3 * GDN_DIM, GDN_DIM, GDN_HEADS, GDN_HEADS, D_MODEL, D_MODEL)
    offs = np.concatenate([[0], np.cumsum(sizes)])
    w = w.astype(BF16)
    seg = [w[:, offs[i]:offs[i + 1]] for i in range(len(sizes))]
    q_a, kv_a, gate_a, qkv_b, z_b, a_b, b_b, m_a, m_b = seg
    w_big = jnp.concatenate([q_a, kv_a, qkv_b, z_b, m_a, m_b], axis=1)
    d = w.shape[0]
    small = []
    per_group = NSA_REP * N_BRANCH
    for g in range(NSA_GROUPS):
        small.append(jnp.pad(gate_a[:, g * per_group:(g + 1) * per_group], ((0, 0), (0, 128 - per_group))))
    for hg in range(GDN_NHG):
        hs = slice(hg * GDN_HB, (hg + 1) * GDN_HB)
        small.append(jnp.pad(jnp.concatenate([a_b[:, hs], b_b[:, hs]], axis=1), ((0, 0), (0, 128 - 2 * GDN_HB))))
    w_small = jnp.concatenate(small, axis=1).astype(BF16)
    assert w_big.shape == (d, NP_BIG) and w_small.shape == (d, NP_SMALL)
    return w_big, w_small


def _lane_rows(v):
    return jnp.pad(v.reshape(GDN_NHG, 1, GDN_HB).astype(F32), ((0, 0), (0, 0), (0, 128 - GDN_HB)))


def _nsa_constants(seq):
    n_cmp = (seq - CMP_BLOCK) // CMP_STRIDE + 1
    nc = seq // CMP_STRIDE
    n_sel = seq // SLC_BLOCK
    cmp_start = np.arange(n_cmp) * CMP_STRIDE
    sel_start = np.arange(n_sel) * SLC_BLOCK
    overlap = (np.minimum(cmp_start[:, None] + CMP_BLOCK, sel_start[None, :] + SLC_BLOCK)
               - np.maximum(cmp_start[:, None], sel_start[None, :]))
    agg = np.zeros((nc, n_sel), np.float32)
    agg[:n_cmp] = np.clip(overlap, 0, None) / CMP_BLOCK
    eye = np.eye(NSA_TQ, dtype=np.float32)
    i = np.arange(NSA_TQ)[:, None]
    c = np.arange(WINDOW + NSA_TQ)[None, :]
    win_bias = np.where((c > i) & (c <= i + WINDOW), 0.0, NEG_INF).astype(np.float32)
    return jnp.asarray(agg.T, BF16), jnp.asarray(eye, BF16), jnp.asarray(win_bias)


def _gdn_constants():
    idx = np.arange(GDN_TS)
    shifts = np.stack([(idx[None, :] == idx[:, None] - d) for d in range(1, GDN_CONV)]).astype(np.float32)
    return jnp.asarray(shifts, BF16), jnp.asarray(np.eye(HEAD_DIM, dtype=np.float32), BF16)


def _layer(x, attn_norm, w_in, nsa_q_norm, nsa_k_norm, cmp_pos, w_cmp, gdn_conv, gdn_a_log, gdn_dt_bias,
           gdn_out_norm, w_branch_a, w_branch_b, w_out, ffn_norm, w_gate, w_up, w_down):
    b, seq, d = x.shape
    t = b * seq
    x2 = x.reshape(t, d)

    w_big, w_small = _pack_in_weights(w_in)
    p2, ps2, kvh = _in_proj(x2, attn_norm.reshape(1, d), w_big, w_small)
    p3 = p2.reshape(b, seq, NP_BIG)
    ps3 = ps2.reshape(b, seq, NP_SMALL)

    nc = seq // CMP_STRIDE
    ckv = kvh.reshape(2 * NSA_GROUPS, b, nc, CMP_STRIDE * HEAD_DIM)
    w_cmp2 = w_cmp.reshape(2, CMP_BLOCK * HEAD_DIM, HEAD_DIM).astype(BF16)
    pos2 = jnp.broadcast_to(cmp_pos.reshape(2, 1, CMP_BLOCK * HEAD_DIM), (2, 8, CMP_BLOCK * HEAD_DIM)).astype(BF16)
    kc, vc = _nsa_compress(ckv, w_cmp2, pos2, nsa_k_norm[0:1])

    agg_t, eye_q, win_bias = _nsa_constants(seq)
    o_a = _nsa_attention(p3, ps3, kc, vc, agg_t, eye_q, win_bias, nsa_q_norm.reshape(1, HEAD_DIM), nsa_k_norm)

    shifts, eye_h = _gdn_constants()
    conv_w3 = gdn_conv.reshape(GDN_CONV, 3, GDN_DIM).transpose(1, 0, 2)
    o_b = _gdn(p3, ps3, conv_w3, _lane_rows(gdn_a_log), _lane_rows(gdn_dt_bias),
               gdn_out_norm.reshape(1, HEAD_DIM), shifts, eye_h)

    mix = _merge(o_a.reshape(t, NSA_Q_DIM), o_b.reshape(t, GDN_DIM), w_branch_a.astype(BF16),
                 w_branch_b.astype(BF16), p2)
    x1 = _resid_matmul(mix, w_out.astype(BF16), x2, bm=1024, bn=1024)

    act = _ffn_up(x1, ffn_norm.reshape(1, d), w_gate.astype(BF16), w_up.astype(BF16))
    out = _resid_matmul(act, w_down.astype(BF16), x1, bm=512, bn=1024)
    return out.reshape(b, seq, d)


def kernel(x, attn_norm, w_in, nsa_q_norm, nsa_k_norm, cmp_pos, w_cmp, gdn_conv, gdn_a_log, gdn_dt_bias,
           gdn_out_norm, w_branch_a, w_branch_b, w_out, ffn_norm, w_gate, w_up, w_down):
    for l in range(attn_norm.shape[0]):
        x = _layer(x, attn_norm[l], w_in[l], nsa_q_norm[l], nsa_k_norm[l], cmp_pos[l], w_cmp[l], gdn_conv[l],
                   gdn_a_log[l], gdn_dt_bias[l], gdn_out_norm[l], w_branch_a[l], w_branch_b[l], w_out[l],
                   ffn_norm[l], w_gate[l], w_up[l], w_down[l])
    return x
```

```python
import functools

import numpy as np
import jax
import jax.numpy as jnp
from jax import lax
from jax.experimental import pallas as pl
from jax.experimental.pallas import tpu as pltpu

F32 = jnp.float32
BF16 = jnp.bfloat16

D_MODEL = 2048
EPS = 1e-6
NEG_INF = -1e30
TINY = 1e-30
FORCE_SCORE = 1e4
LOG2E = 1.4426950408889634

NSA_HEADS = 16
NSA_GROUPS = 4
NSA_REP = NSA_HEADS // NSA_GROUPS
HEAD_DIM = 128
N_BRANCH = 3
CMP_BLOCK = 32
CMP_STRIDE = 16
SLC_BLOCK = 64
SLC_TOPK = 16
WINDOW = 512

GDN_HEADS = 16
GDN_CONV = 4
GDN_CHUNK = 64

D_FF = 5632

NSA_Q_DIM = NSA_HEADS * HEAD_DIM
NSA_KV_DIM = N_BRANCH * 2 * NSA_GROUPS * HEAD_DIM
GDN_DIM = GDN_HEADS * HEAD_DIM
OFF_Q = 0
OFF_KV = OFF_Q + NSA_Q_DIM
OFF_GQKV = OFF_KV + NSA_KV_DIM
OFF_Z = OFF_GQKV + 3 * GDN_DIM
OFF_MA = OFF_Z + GDN_DIM
OFF_MB = OFF_MA + D_MODEL
NP_BIG = OFF_MB + D_MODEL

GDN_HB = 8
GDN_NHG = GDN_HEADS // GDN_HB
NP_SMALL = (NSA_GROUPS + GDN_NHG) * 128

VMEM_LIMIT = 56 * 1024 * 1024

NT_DIMS = (((1,), (1,)), ((), ()))


def _dot(a, b):
    return jnp.dot(a, b, preferred_element_type=F32)


def _dot_nt(a, b):
    return lax.dot_general(a, b, NT_DIMS, preferred_element_type=F32)


def _cparams(sem):
    return pltpu.CompilerParams(dimension_semantics=sem, vmem_limit_bytes=VMEM_LIMIT)


def _in_proj_kernel(x_ref, g_ref, w_ref, ws_ref, o_ref, os_ref, hn_ref):
    @pl.when(pl.program_id(1) == 0)
    def _():
        x = x_ref[...]
        y = x * lax.rsqrt(jnp.mean(x * x, axis=-1, keepdims=True) + EPS) * g_ref[...]
        hn = y.astype(BF16)
        hn_ref[...] = hn
        os_ref[...] = _dot(hn, ws_ref[...])

    o_ref[...] = _dot(hn_ref[...], w_ref[...]).astype(o_ref.dtype)


def _in_proj(x2, gain, w_big, w_small, bm=1024, bn=1024):
    t, d = x2.shape
    n = w_big.shape[1]
    ns = w_small.shape[1]
    return pl.pallas_call(
        _in_proj_kernel,
        out_shape=(jax.ShapeDtypeStruct((t, n), BF16), jax.ShapeDtypeStruct((t, ns), F32)),
        grid=(t // bm, n // bn),
        in_specs=[
            pl.BlockSpec((bm, d), lambda i, j: (i, 0)),
            pl.BlockSpec((1, d), lambda i, j: (0, 0)),
            pl.BlockSpec((d, bn), lambda i, j: (0, j)),
            pl.BlockSpec((d, ns), lambda i, j: (0, 0)),
        ],
        out_specs=(
            pl.BlockSpec((bm, bn), lambda i, j: (i, j)),
            pl.BlockSpec((bm, ns), lambda i, j: (i, 0)),
        ),
        scratch_shapes=[pltpu.VMEM((bm, d), BF16)],
        compiler_params=_cparams(("parallel", "arbitrary")),
        name="in_proj",
    )(x2, gain, w_big, w_small)


def _cmp_kernel(ck_ref, cv_ref, w_ref, pos_ref, kg_ref, kc_ref, vc_ref, xs_ref):
    dh = HEAD_DIM
    n = kc_ref.shape[0]

    def compress(x_ref, idx):
        xs_ref[...] = x_ref[...].astype(F32)
        lo = hi = None
        for l in range(CMP_STRIDE):
            xl = xs_ref[pl.ds(l, n, stride=CMP_STRIDE), :].astype(BF16)
            lo_l = _dot(xl, w_ref[idx, l * dh:(l + 1) * dh, :])
            hi_l = _dot(xl, w_ref[idx, (CMP_STRIDE + l) * dh:(CMP_STRIDE + l + 1) * dh, :])
            lo = lo_l if lo is None else lo + lo_l
            hi = hi_l if hi is None else hi + hi_l
        bias = _dot(pos_ref[idx], w_ref[idx])[0:1]
        return lo + pltpu.roll(hi, n - 1, 0) + bias

    kc = compress(ck_ref, 0)
    kc = kc * lax.rsqrt(jnp.mean(kc * kc, axis=-1, keepdims=True) + EPS) * kg_ref[...]
    kc_ref[...] = kc.astype(BF16)
    vc_ref[...] = compress(cv_ref, 1).astype(BF16)


def _nsa_compress(p3, w_cmp2, pos2, k_gain0):
    b, seq, _ = p3.shape
    g, dh = NSA_GROUPS, HEAD_DIM
    nc = seq // CMP_STRIDE
    width = CMP_BLOCK * dh
    kvb = OFF_KV // dh
    out = jax.ShapeDtypeStruct((b, g, nc, dh), BF16)
    return pl.pallas_call(
        _cmp_kernel,
        out_shape=(out, out),
        grid=(b, g),
        in_specs=[
            pl.BlockSpec((None, seq, dh), lambda i, j: (i, 0, kvb + j)),
            pl.BlockSpec((None, seq, dh), lambda i, j: (i, 0, kvb + g + j)),
            pl.BlockSpec((2, width, dh), lambda i, j: (0, 0, 0)),
            pl.BlockSpec((2, 8, width), lambda i, j: (0, 0, 0)),
            pl.BlockSpec((1, dh), lambda i, j: (0, 0)),
        ],
        out_specs=(
            pl.BlockSpec((None, None, nc, dh), lambda i, j: (i, j, 0, 0)),
            pl.BlockSpec((None, None, nc, dh), lambda i, j: (i, j, 0, 0)),
        ),
        scratch_shapes=[pltpu.VMEM((seq, dh), F32)],
        compiler_params=_cparams(("parallel", "parallel")),
        name="nsa_compress",
    )(p3, p3, w_cmp2, pos2, k_gain0)


NSA_TQ = 256
NSA_TK = 512
NORM_ROWS = 512
RANK_UNROLL = 4


def _rms(xf, gain):
    return xf * lax.rsqrt(jnp.mean(xf * xf, axis=-1, keepdims=True) + EPS) * gain


def _nsa_kernel(q_ref, gl_ref, ks_ref, vs_ref, kw_ref, vw_ref, kc_ref, vc_ref, aggt_ref, eye_ref, wb_ref,
                qg_ref, kg_ref, o_ref, ksa_ref, kwa_ref, vsa_ref, vwp_ref, sc_ref, s0_ref, s1_ref, m_ref, acc_ref,
                *, seq):
    tq, tk, rep, dh = NSA_TQ, NSA_TK, NSA_REP, HEAD_DIM
    rows = rep * tq
    n_sel = seq // SLC_BLOCK
    n_topk = min(SLC_TOPK, n_sel)
    qi = pl.program_id(2)
    q0 = qi * tq

    @pl.when(qi == 0)
    def _():
        kwa_ref[0:WINDOW, 0:dh] = jnp.zeros((WINDOW, dh), BF16)
        kwa_ref[0:WINDOW, dh:2 * dh] = jnp.where(
            lax.broadcasted_iota(jnp.int32, (WINDOW, dh), 1) == 0, 1.0, 0.0).astype(BF16)
        ones = jnp.ones((NORM_ROWS, dh), BF16)
        vwp_ref[0:WINDOW, 0:dh] = jnp.zeros((WINDOW, dh), BF16)
        vwp_ref[0:WINDOW, dh:2 * dh] = jnp.ones((WINDOW, dh), BF16)
        lane = lax.broadcasted_iota(jnp.int32, (NORM_ROWS, dh), 1)
        sub = lax.broadcasted_iota(jnp.int32, (NORM_ROWS, dh), 0)

        def body(c, carry):
            r0 = pl.multiple_of(c * NORM_ROWS, NORM_ROWS)
            r = pl.ds(r0, NORM_ROWS)
            rw = pl.ds(r0 + WINDOW, NORM_ROWS)
            ksa_ref[r, 0:dh] = _rms(ks_ref[r, :].astype(F32), kg_ref[1:2, :]).astype(BF16)
            ksa_ref[r, dh:2 * dh] = jnp.where((r0 + sub) // SLC_BLOCK == lane, 1.0, 0.0).astype(BF16)
            kwa_ref[rw, 0:dh] = _rms(kw_ref[r, :].astype(F32), kg_ref[2:3, :]).astype(BF16)
            kwa_ref[rw, dh:2 * dh] = jnp.zeros((NORM_ROWS, dh), BF16)
            vsa_ref[r, 0:dh] = vs_ref[r, :]
            vsa_ref[r, dh:2 * dh] = ones
            vwp_ref[rw, 0:dh] = vw_ref[r, :]
            vwp_ref[rw, dh:2 * dh] = ones
            return carry
        lax.fori_loop(0, seq // NORM_ROWS, body, 0)

    qscale = dh ** -0.5 * LOG2E
    q = q_ref[...].astype(F32)
    qs = jnp.concatenate(
        [(_rms(q[:, r * dh:(r + 1) * dh], qg_ref[...]) * qscale).astype(BF16) for r in range(rep)], axis=0)

    wk = WINDOW + tq
    padneg = jnp.where(lax.broadcasted_iota(jnp.int32, (rows, dh), 1) == 0, NEG_INF, 0.0).astype(BF16)
    qa_win = jnp.concatenate([qs, padneg], axis=1)
    kr = pl.ds(pl.multiple_of(q0, tq), wk)
    s_w = _dot_nt(qa_win, kwa_ref[kr, :]) + jnp.concatenate([wb_ref[...]] * rep, axis=0)

    n_cmp = kc_ref.shape[0]
    s = _dot_nt(qs, kc_ref[...])
    p_w = jnp.exp2(s_w - jnp.max(s_w, axis=-1, keepdims=True))
    ov_w = _dot(p_w.astype(BF16), vwp_ref[kr, :])
    o_win = ov_w[:, :dh] / ov_w[:, dh:]
    t_c = q0 + (lax.broadcasted_iota(jnp.int32, (rows, n_cmp), 0) & (tq - 1))
    cmp_end = lax.broadcasted_iota(jnp.int32, (rows, n_cmp), 1) * CMP_STRIDE + (CMP_BLOCK - 1)
    mask = cmp_end <= t_c
    s = jnp.where(mask, s, NEG_INF)
    e = jnp.where(mask, jnp.exp2(s - jnp.max(s, axis=-1, keepdims=True)), 0.0)
    vc_aug = jnp.concatenate([vc_ref[...], jnp.ones((n_cmp, dh), BF16)], axis=1)
    ov = _dot(e.astype(BF16), vc_aug)
    den = jnp.maximum(ov[:, dh:], TINY)
    o_cmp = ov[:, :dh] / den
    p = e / jnp.concatenate([den] * (n_cmp // dh), axis=1)
    p_sum = p[0:tq]
    for r in range(1, rep):
        p_sum = p_sum + p[r * tq:(r + 1) * tq]
    imp_t = _dot_nt(aggt_ref[...], p_sum.astype(BF16))

    jidx = lax.broadcasted_iota(jnp.int32, (n_sel, tq), 0)
    t_l = q0 + lax.broadcasted_iota(jnp.int32, (n_sel, tq), 1)
    cur = t_l // SLC_BLOCK
    forced = (jidx == 0) | (jidx == cur) | (jidx == cur - 1)
    causal = jidx * SLC_BLOCK <= t_l
    score = jnp.where(forced, FORCE_SCORE, jnp.where(causal, imp_t, -FORCE_SCORE))
    sc_ref[...] = score

    def rank_body(it, rank):
        for u in range(RANK_UNROLL):
            i = it * RANK_UNROLL + u
            row = sc_ref[pl.ds(i, 1), :]
            ge = jnp.where(row >= score, 1.0, 0.0)
            gt = jnp.where(row > score, 1.0, 0.0)
            rank = rank + jnp.where(jidx > i, ge, gt)
        return rank

    n_blk = (q0 + tq - 1) // SLC_BLOCK + 1
    n_it = (n_blk + RANK_UNROLL - 1) // RANK_UNROLL
    rank = lax.fori_loop(0, n_it, rank_body, jnp.zeros((n_sel, tq), F32))
    selneg_t = jnp.where(rank < n_topk, 0.0, NEG_INF)
    if n_sel < dh:
        selneg_t = jnp.concatenate([selneg_t, jnp.zeros((dh - n_sel, tq), F32)], axis=0)
    selneg = _dot_nt(eye_ref[...], selneg_t.astype(BF16)).astype(BF16)
    qa_sel = jnp.concatenate([qs, jnp.concatenate([selneg] * rep, axis=0)], axis=1)

    def qk_tile(kt):
        return _dot_nt(qa_sel, ksa_ref[pl.ds(pl.multiple_of(kt * tk, tk), tk), :])

    def consume(kt, buf, diagonal):
        k0 = pl.multiple_of(kt * tk, tk)
        s = buf[...]
        if diagonal:
            t_r = q0 + (lax.broadcasted_iota(jnp.int32, (rows, tk), 0) & (tq - 1))
            kpos = k0 + lax.broadcasted_iota(jnp.int32, (rows, tk), 1)
            s = jnp.where(kpos <= t_r, s, NEG_INF)
        m_prev = m_ref[...]
        m_new = jnp.maximum(m_prev, jnp.max(s, axis=-1, keepdims=True))
        p = jnp.exp2(s - jnp.concatenate([m_new] * (tk // 128), axis=1))
        alpha = jnp.exp2(m_prev - m_new)
        m_ref[...] = m_new
        acc_ref[...] = (jnp.concatenate([alpha, alpha], axis=1) * acc_ref[...]
                        + _dot(p.astype(BF16), vsa_ref[pl.ds(k0, tk), :]))

    m_ref[...] = jnp.full((rows, 128), NEG_INF, F32)
    acc_ref[...] = jnp.zeros((rows, 2 * dh), F32)
    s0_ref[...] = qk_tile(0)
    n_full = q0 // tk

    def sel_body(it, carry):
        kt = 2 * it
        s1_ref[...] = qk_tile(kt + 1)
        consume(kt, s0_ref, False)
        s0_ref[...] = qk_tile(kt + 2)
        consume(kt + 1, s1_ref, False)
        return carry

    lax.fori_loop(0, n_full // 2, sel_body, 0)

    @pl.when(n_full % 2 == 1)
    def _():
        s1_ref[...] = qk_tile(n_full)
        consume(n_full - 1, s0_ref, False)
        consume(n_full, s1_ref, True)

    @pl.when(n_full % 2 == 0)
    def _():
        consume(n_full, s0_ref, True)

    o_slc = acc_ref[:, 0:dh] / acc_ref[:, dh:2 * dh]

    gates = jax.nn.sigmoid(gl_ref[...])
    for r in range(rep):
        rs = slice(r * tq, (r + 1) * tq)
        o = (gates[:, 3 * r:3 * r + 1] * o_cmp[rs] + gates[:, 3 * r + 1:3 * r + 2] * o_slc[rs]
             + gates[:, 3 * r + 2:3 * r + 3] * o_win[rs])
        o_ref[:, r * dh:(r + 1) * dh] = o.astype(o_ref.dtype)


def _nsa_attention(p3, ps3, kc, vc, agg_t, eye, win_bias, q_gain, k_gain):
    b, seq, _ = p3.shape
    g, tq, dh = NSA_GROUPS, NSA_TQ, HEAD_DIM
    n_sel = seq // SLC_BLOCK
    assert n_sel <= dh and n_sel % 8 == 0 and seq % NSA_TK == 0
    nc = kc.shape[2]
    kvb = OFF_KV // dh

    def kv_spec(branch, is_v):
        base = kvb + (branch * 2 + is_v) * g
        return pl.BlockSpec((None, seq, dh), lambda i, j, k: (i, 0, base + j))

    qw = NSA_REP * dh
    return pl.pallas_call(
        functools.partial(_nsa_kernel, seq=seq),
        out_shape=jax.ShapeDtypeStruct((b, seq, NSA_Q_DIM), BF16),
        grid=(b, g, seq // tq),
        in_specs=[
            pl.BlockSpec((None, tq, qw), lambda i, j, k: (i, k, j)),
            pl.BlockSpec((None, tq, 128), lambda i, j, k: (i, k, j)),
            kv_spec(1, 0), kv_spec(1, 1), kv_spec(2, 0), kv_spec(2, 1),
            pl.BlockSpec((None, None, nc, dh), lambda i, j, k: (i, j, 0, 0)),
            pl.BlockSpec((None, None, nc, dh), lambda i, j, k: (i, j, 0, 0)),
            pl.BlockSpec((n_sel, nc), lambda i, j, k: (0, 0)),
            pl.BlockSpec((tq, tq), lambda i, j, k: (0, 0)),
            pl.BlockSpec((tq, WINDOW + tq), lambda i, j, k: (0, 0)),
            pl.BlockSpec((1, dh), lambda i, j, k: (0, 0)),
            pl.BlockSpec((N_BRANCH, dh), lambda i, j, k: (0, 0)),
        ],
        out_specs=pl.BlockSpec((None, tq, qw), lambda i, j, k: (i, k, j)),
        scratch_shapes=[
            pltpu.VMEM((seq, 2 * dh), BF16),
            pltpu.VMEM((seq + WINDOW, 2 * dh), BF16),
            pltpu.VMEM((seq, 2 * dh), BF16),
            pltpu.VMEM((seq + WINDOW, 2 * dh), BF16),
            pltpu.VMEM((n_sel, tq), F32),
            pltpu.VMEM((NSA_REP * tq, NSA_TK), F32),
            pltpu.VMEM((NSA_REP * tq, NSA_TK), F32),
            pltpu.VMEM((NSA_REP * tq, 128), F32),
            pltpu.VMEM((NSA_REP * tq, 2 * dh), F32),
        ],
        compiler_params=_cparams(("parallel", "parallel", "arbitrary")),
        name="nsa_attention",
    )(p3, ps3, p3, p3, p3, p3, kc, vc, agg_t, eye, win_bias, q_gain, k_gain)


GDN_TS = 256
GDN_INV = 128
HALO = 8


def _gdn_kernel(q_ref, k_ref, v_ref, z_ref, ab_ref, cw_ref, alog_ref, dtb_ref, og_ref, sh_ref, eye_ref,
                o_ref, xe_ref, st_ref):
    ts, hb, dh, ch = GDN_TS, GDN_HB, HEAD_DIM, GDN_CHUNK
    ti = pl.program_id(2)

    n_inv = ts // GDN_INV
    n_ch = ts // ch

    @pl.when(ti == 0)
    def _():
        xe_ref[...] = jnp.zeros_like(xe_ref)
        st_ref[...] = jnp.zeros_like(st_ref)

    row8 = lax.broadcasted_iota(jnp.int32, (HALO, hb * dh), 0)
    conv = []
    for part, ref in enumerate((q_ref, k_ref, v_ref)):
        xb = ref[...]
        xf = xb.astype(F32)
        prev = xe_ref[part]
        acc = cw_ref[part, GDN_CONV - 1:GDN_CONV, :] * xf
        corr = jnp.zeros((HALO, hb * dh), F32)
        for d in range(1, GDN_CONV):
            wd = cw_ref[part, GDN_CONV - 1 - d:GDN_CONV - d, :]
            acc = acc + wd * _dot(sh_ref[d - 1], xb)
            corr = corr + wd * jnp.where(row8 < d, pltpu.roll(prev, d, 0), 0.0)
        acc = jnp.concatenate([acc[0:HALO] + corr, acc[HALO:]], axis=0)
        conv.append(acc * jax.nn.sigmoid(acc))
        xe_ref[part] = xf[ts - HALO:ts]

    ab = ab_ref[...]
    xg = ab + dtb_ref[...]
    softplus = jnp.maximum(xg, 0.0) + jnp.log(1.0 + jnp.exp(-jnp.abs(xg)))
    g_cum = -jnp.exp(alog_ref[...]) * softplus
    beta = jax.nn.sigmoid(ab)
    row_in_chunk = lax.broadcasted_iota(jnp.int32, (ts, 128), 0) & (ch - 1)
    step = 1
    while step < ch:
        g_cum = g_cum + jnp.where(row_in_chunk >= step, pltpu.roll(g_cum, step, 0), 0.0)
        step *= 2
    g_end = jnp.concatenate(
        [jnp.broadcast_to(g_cum[(c + 1) * ch - 1:(c + 1) * ch, :], (ch, 128)) for c in range(n_ch)], axis=0)
    e_cum = jnp.exp(g_cum)
    e_rel = jnp.exp(g_end - g_cum)
    e_end = jnp.exp(g_end)
    g_cum_t = jnp.transpose(g_cum)

    ri = lax.broadcasted_iota(jnp.int32, (ts, ts), 0)
    ci = lax.broadcasted_iota(jnp.int32, (ts, ts), 1)
    same = (ri // ch) == (ci // ch)
    tril = same & (ri >= ci)
    strict = same & (ri > ci)
    ri_b = lax.broadcasted_iota(jnp.int32, (GDN_INV, GDN_INV), 0)
    ci_b = lax.broadcasted_iota(jnp.int32, (GDN_INV, GDN_INV), 1)
    ident = jnp.where(ri_b == ci_b, 1.0, 0.0)
    eye_bf = eye_ref[...]

    heads = []
    for hh in range(hb):
        cs = slice(hh * dh, (hh + 1) * dh)
        qh, kh, vh = conv[0][:, cs], conv[1][:, cs], conv[2][:, cs]
        qh = qh * lax.rsqrt(jnp.sum(qh * qh, axis=-1, keepdims=True) + EPS) * (dh ** -0.5)
        kh = kh * lax.rsqrt(jnp.sum(kh * kh, axis=-1, keepdims=True) + EPS)
        bt = beta[:, hb + hh:hb + hh + 1]
        diff = g_cum[:, hh:hh + 1] - g_cum_t[hh:hh + 1, :]
        e = jnp.exp(jnp.minimum(diff, 0.0))
        kb = kh * bt
        k_bf = kh.astype(BF16)
        a_mat = _dot_nt(kb.astype(BF16), k_bf) * jnp.where(strict, e, 0.0)
        qk = (_dot_nt(qh.astype(BF16), k_bf) * jnp.where(tril, e, 0.0)).astype(BF16)
        heads.append(dict(
            a=a_mat, qk=qk,
            vb=(vh * bt).astype(BF16),
            kbg=(kb * e_cum[:, hh:hh + 1]).astype(BF16),
            qg=qh * e_cum[:, hh:hh + 1],
        ))
        kd = (kh * e_rel[:, hh:hh + 1]).astype(BF16)
        heads[-1]["kdt"] = [_dot_nt(eye_bf, kd[c * ch:(c + 1) * ch]).astype(BF16)
                            for c in range(n_ch)]

    probs = []
    for hh in range(hb):
        for blk in range(n_inv):
            bs = slice(blk * GDN_INV, (blk + 1) * GDN_INV)
            pw = -heads[hh]["a"][bs, bs]
            probs.append([pw, ident + pw])
    for _ in range(5):
        for pr in probs:
            pw_bf = pr[0].astype(BF16)
            pr[0] = _dot(pw_bf, pw_bf)
        for pr in probs:
            pr[1] = pr[1] + _dot(pr[1].astype(BF16), pr[0].astype(BF16))
    for hh in range(hb):
        u_parts, w_parts = [], []
        for blk in range(n_inv):
            bs = slice(blk * GDN_INV, (blk + 1) * GDN_INV)
            t_bf = probs[hh * n_inv + blk][1].astype(BF16)
            u_parts.append(_dot(t_bf, heads[hh]["vb"][bs]))
            w_parts.append(_dot(t_bf, heads[hh]["kbg"][bs]))
        heads[hh]["u"] = jnp.concatenate(u_parts, axis=0).astype(BF16)
        heads[hh]["w"] = jnp.concatenate(w_parts, axis=0).astype(BF16)
    for hd in heads:
        hd["qkw"] = _dot(hd["qk"], hd["w"])
        hd["qku"] = _dot(hd["qk"], hd["u"])
        hd["mw"] = [_dot(hd["kdt"][c], jnp.concatenate([hd["w"][c * ch:(c + 1) * ch],
                                                        hd["u"][c * ch:(c + 1) * ch]], axis=1))
                    for c in range(n_ch)]
    for hd in heads:
        hd["qp"] = (hd["qg"] - hd["qkw"]).astype(BF16)
        hd["mc"] = [mw[:, :dh].astype(BF16) for mw in hd["mw"]]
        hd["bc"] = [mw[:, dh:] for mw in hd["mw"]]

    states = [st_ref[hh] for hh in range(hb)]
    outs = [[] for _ in range(hb)]
    for c in range(n_ch):
        rs = slice(c * ch, (c + 1) * ch)
        for hh in range(hb):
            hd = heads[hh]
            r = _dot(jnp.concatenate([hd["qp"][rs], hd["mc"][c]], axis=0), states[hh].astype(BF16))
            outs[hh].append(r[:ch] + hd["qku"][rs])
            states[hh] = states[hh] * e_end[c * ch:c * ch + 1, hh:hh + 1] - r[ch:] + hd["bc"][c]

    for hh in range(hb):
        cs = slice(hh * dh, (hh + 1) * dh)
        st_ref[hh] = states[hh]
        o = jnp.concatenate(outs[hh], axis=0)
        o = o * lax.rsqrt(jnp.mean(o * o, axis=-1, keepdims=True) + EPS) * og_ref[...]
        z = z_ref[:, cs].astype(F32)
        o_ref[:, cs] = (o * (z * jax.nn.sigmoid(z))).astype(o_ref.dtype)


def _gdn(p3, ps3, conv_w3, alog_l, dtb_l, out_gain, shifts, eye):
    b, seq, _ = p3.shape
    ts, hb, dh = GDN_TS, GDN_HB, HEAD_DIM
    wb = hb * dh
    qb, zb = OFF_GQKV // wb, OFF_Z // wb
    gw = GDN_DIM // wb
    return pl.pallas_call(
        _gdn_kernel,
        out_shape=jax.ShapeDtypeStruct((b, seq, GDN_DIM), BF16),
        grid=(b, GDN_NHG, seq // ts),
        in_specs=[
            pl.BlockSpec((None, ts, wb), lambda i, j, k: (i, k, qb + j)),
            pl.BlockSpec((None, ts, wb), lambda i, j, k: (i, k, qb + gw + j)),
            pl.BlockSpec((None, ts, wb), lambda i, j, k: (i, k, qb + 2 * gw + j)),
            pl.BlockSpec((None, ts, wb), lambda i, j, k: (i, k, zb + j)),
            pl.BlockSpec((None, ts, 128), lambda i, j, k: (i, k, NSA_GROUPS + j)),
            pl.BlockSpec((3, GDN_CONV, wb), lambda i, j, k: (0, 0, j)),
            pl.BlockSpec((None, 1, 128), lambda i, j, k: (j, 0, 0)),
            pl.BlockSpec((None, 1, 128), lambda i, j, k: (j, 0, 0)),
            pl.BlockSpec((1, dh), lambda i, j, k: (0, 0)),
            pl.BlockSpec((GDN_CONV - 1, ts, ts), lambda i, j, k: (0, 0, 0)),
            pl.BlockSpec((dh, dh), lambda i, j, k: (0, 0)),
        ],
        out_specs=pl.BlockSpec((None, ts, wb), lambda i, j, k: (i, k, j)),
        scratch_shapes=[
            pltpu.VMEM((3, HALO, wb), F32),
            pltpu.VMEM((hb, dh, dh), F32),
        ],
        compiler_params=_cparams(("parallel", "parallel", "arbitrary")),
        name="gdn",
    )(p3, p3, p3, p3, ps3, conv_w3, alog_l, dtb_l, out_gain, shifts, eye)


def _merge_kernel(oa_ref, ob_ref, wa_ref, wb_ref, ma_ref, mb_ref, o_ref):
    ya = _dot(oa_ref[...], wa_ref[...])
    yb = _dot(ob_ref[...], wb_ref[...])
    mix = jax.nn.sigmoid(ma_ref[...].astype(F32)) * ya + jax.nn.sigmoid(mb_ref[...].astype(F32)) * yb
    o_ref[...] = mix.astype(o_ref.dtype)


def _merge(oa, ob, wa, wb, p2, bm=512, bn=1024):
    t, d = oa.shape
    n = wa.shape[1]
    ma_b, mb_b = OFF_MA // bn, OFF_MB // bn
    return pl.pallas_call(
        _merge_kernel,
        out_shape=jax.ShapeDtypeStruct((t, n), BF16),
        grid=(n // bn, t // bm),
        in_specs=[
            pl.BlockSpec((bm, d), lambda j, i: (i, 0)),
            pl.BlockSpec((bm, d), lambda j, i: (i, 0)),
            pl.BlockSpec((d, bn), lambda j, i: (0, j)),
            pl.BlockSpec((d, bn), lambda j, i: (0, j)),
            pl.BlockSpec((bm, bn), lambda j, i: (i, ma_b + j)),
            pl.BlockSpec((bm, bn), lambda j, i: (i, mb_b + j)),
        ],
        out_specs=pl.BlockSpec((bm, bn), lambda j, i: (i, j)),
        compiler_params=_cparams(("parallel", "parallel")),
        name="merge",
    )(oa, ob, wa, wb, p2, p2)


def _resid_matmul_kernel(a_ref, w_ref, r_ref, o_ref):
    o_ref[...] = r_ref[...] + _dot(a_ref[...], w_ref[...])


def _resid_matmul(a, w, resid, bm, bn):
    t, k = a.shape
    n = w.shape[1]
    return pl.pallas_call(
        _resid_matmul_kernel,
        out_shape=jax.ShapeDtypeStruct((t, n), F32),
        grid=(n // bn, t // bm),
        in_specs=[
            pl.BlockSpec((bm, k), lambda j, i: (i, 0)),
            pl.BlockSpec((k, bn), lambda j, i: (0, j)),
            pl.BlockSpec((bm, bn), lambda j, i: (i, j)),
        ],
        out_specs=pl.BlockSpec((bm, bn), lambda j, i: (i, j)),
        compiler_params=_cparams(("parallel", "parallel")),
        name="resid_matmul",
    )(a, w, resid)


def _ffn_up_kernel(x_ref, g_ref, wg_ref, wu_ref, o_ref, hn_ref):
    @pl.when(pl.program_id(1) == 0)
    def _():
        x = x_ref[...]
        hn_ref[...] = (x * lax.rsqrt(jnp.mean(x * x, axis=-1, keepdims=True) + EPS) * g_ref[...]).astype(BF16)

    hn = hn_ref[...]
    gate = _dot(hn, wg_ref[...])
    up = _dot(hn, wu_ref[...])
    o_ref[...] = (gate * jax.nn.sigmoid(gate) * up).astype(o_ref.dtype)


def _ffn_up(x1, gain, wg, wu, bm=1024, bn=512):
    t, d = x1.shape
    n = wg.shape[1]
    return pl.pallas_call(
        _ffn_up_kernel,
        out_shape=jax.ShapeDtypeStruct((t, n), BF16),
        grid=(t // bm, n // bn),
        in_specs=[
            pl.BlockSpec((bm, d), lambda i, j: (i, 0)),
            pl.BlockSpec((1, d), lambda i, j: (0, 0)),
            pl.BlockSpec((d, bn), lambda i, j: (0, j)),
            pl.BlockSpec((d, bn), lambda i, j: (0, j)),
        ],
        out_specs=pl.BlockSpec((bm, bn), lambda i, j: (i, j)),
        scratch_shapes=[pltpu.VMEM((bm, d), BF16)],
        compiler_params=_cparams(("parallel", "arbitrary")),
        name="ffn_up",
    )(x1, gain, wg, wu)


def _pack_in_weights(w):
    sizes = (NSA_Q_DIM, NSA_KV_DIM, NSA_HEADS * N_BRANCH, 3 * GDN_DIM, GDN_DIM, GDN_HEADS, GDN_HEADS, D_MODEL, D_MODEL)
    offs = np.concatenate([[0], np.cumsum(sizes)])
    w = w.astype(BF16)
    seg = [w[:, offs[i]:offs[i + 1]] for i in range(len(sizes))]
    q_a, kv_a, gate_a, qkv_b, z_b, a_b, b_b, m_a, m_b = seg
    w_big = jnp.concatenate([q_a, kv_a, qkv_b, z_b, m_a, m_b], axis=1)
    d = w.shape[0]
    small = []
    per_group = NSA_REP * N_BRANCH
    for g in range(NSA_GROUPS):
        small.append(jnp.pad(gate_a[:, g * per_group:(g + 1) * per_group], ((0, 0), (0, 128 - per_group))))
    for hg in range(GDN_NHG):
        hs = slice(hg * GDN_HB, (hg + 1) * GDN_HB)
        small.append(jnp.pad(jnp.concatenate([a_b[:, hs], b_b[:, hs]], axis=1), ((0, 0), (0, 128 - 2 * GDN_HB))))
    w_small = jnp.concatenate(small, axis=1).astype(BF16)
    assert w_big.shape == (d, NP_BIG) and w_small.shape == (d, NP_SMALL)
    return w_big, w_small


def _lane_rows(v):
    return jnp.pad(v.reshape(GDN_NHG, 1, GDN_HB).astype(F32), ((0, 0), (0, 0), (0, 128 - GDN_HB)))


def _nsa_constants(seq):
    n_cmp = (seq - CMP_BLOCK) // CMP_STRIDE + 1
    nc = seq // CMP_STRIDE
    n_sel = seq // SLC_BLOCK
    cmp_start = np.arange(n_cmp) * CMP_STRIDE
    sel_start = np.arange(n_sel) * SLC_BLOCK
    overlap = (np.minimum(cmp_start[:, None] + CMP_BLOCK, sel_start[None, :] + SLC_BLOCK)
               - np.maximum(cmp_start[:, None], sel_start[None, :]))
    agg = np.zeros((nc, n_sel), np.float32)
    agg[:n_cmp] = np.clip(overlap, 0, None) / CMP_BLOCK
    eye = np.eye(NSA_TQ, dtype=np.float32)
    i = np.arange(NSA_TQ)[:, None]
    c = np.arange(WINDOW + NSA_TQ)[None, :]
    win_bias = np.where((c > i) & (c <= i + WINDOW), 0.0, NEG_INF).astype(np.float32)
    return jnp.asarray(agg.T, BF16), jnp.asarray(eye, BF16), jnp.asarray(win_bias)


def _gdn_constants():
    idx = np.arange(GDN_TS)
    shifts = np.stack([(idx[None, :] == idx[:, None] - d) for d in range(1, GDN_CONV)]).astype(np.float32)
    return jnp.asarray(shifts, BF16), jnp.asarray(np.eye(HEAD_DIM, dtype=np.float32), BF16)


def _layer(x, attn_norm, w_in, nsa_q_norm, nsa_k_norm, cmp_pos, w_cmp, gdn_conv, gdn_a_log, gdn_dt_bias,
           gdn_out_norm, w_branch_a, w_branch_b, w_out, ffn_norm, w_gate, w_up, w_down):
    b, seq, d = x.shape
    t = b * seq
    x2 = x.reshape(t, d)

    w_big, w_small = _pack_in_weights(w_in)
    p2, ps2 = _in_proj(x2, attn_norm.reshape(1, d), w_big, w_small)
    p3 = p2.reshape(b, seq, NP_BIG)
    ps3 = ps2.reshape(b, seq, NP_SMALL)

    w_cmp2 = w_cmp.reshape(2, CMP_BLOCK * HEAD_DIM, HEAD_DIM).astype(BF16)
    pos2 = jnp.broadcast_to(cmp_pos.reshape(2, 1, CMP_BLOCK * HEAD_DIM), (2, 8, CMP_BLOCK * HEAD_DIM)).astype(BF16)
    kc, vc = _nsa_compress(p3, w_cmp2, pos2, nsa_k_norm[0:1])

    agg_t, eye_q, win_bias = _nsa_constants(seq)
    o_a = _nsa_attention(p3, ps3, kc, vc, agg_t, eye_q, win_bias, nsa_q_norm.reshape(1, HEAD_DIM), nsa_k_norm)

    shifts, eye_h = _gdn_constants()
    conv_w3 = gdn_conv.reshape(GDN_CONV, 3, GDN_DIM).transpose(1, 0, 2)
    o_b = _gdn(p3, ps3, conv_w3, _lane_rows(gdn_a_log), _lane_rows(gdn_dt_bias),
               gdn_out_norm.reshape(1, HEAD_DIM), shifts, eye_h)

    mix = _merge(o_a.reshape(t, NSA_Q_DIM), o_b.reshape(t, GDN_DIM), w_branch_a.astype(BF16),
                 w_branch_b.astype(BF16), p2)
    x1 = _resid_matmul(mix, w_out.astype(BF16), x2, bm=1024, bn=1024)

    act = _ffn_up(x1, ffn_norm.reshape(1, d), w_gate.astype(BF16), w_up.astype(BF16))
    out = _resid_matmul(act, w_down.astype(BF16), x1, bm=512, bn=1024)
    return out.reshape(b, seq, d)


def kernel(x, attn_norm, w_in, nsa_q_norm, nsa_k_norm, cmp_pos, w_cmp, gdn_conv, gdn_a_log, gdn_dt_bias,
           gdn_out_norm, w_branch_a, w_branch_b, w_out, ffn_norm, w_gate, w_up, w_down):
    for l in range(attn_norm.shape[0]):
        x = _layer(x, attn_norm[l], w_in[l], nsa_q_norm[l], nsa_k_norm[l], cmp_pos[l], w_cmp[l], gdn_conv[l],
                   gdn_a_log[l], gdn_dt_bias[l], gdn_out_norm[l], w_branch_a[l], w_branch_b[l], w_out[l],
                   ffn_norm[l], w_gate[l], w_up[l], w_down[l])
    return x
```

```python
import functools

import numpy as np
import jax
import jax.numpy as jnp
from jax import lax
from jax.experimental import pallas as pl
from jax.experimental.pallas import tpu as pltpu

F32 = jnp.float32
BF16 = jnp.bfloat16

D_MODEL = 2048
EPS = 1e-6
NEG_INF = -1e30
TINY = 1e-30
FORCE_SCORE = 1e4
LOG2E = 1.4426950408889634

NSA_HEADS = 16
NSA_GROUPS = 4
NSA_REP = NSA_HEADS // NSA_GROUPS
HEAD_DIM = 128
N_BRANCH = 3
CMP_BLOCK = 32
CMP_STRIDE = 16
SLC_BLOCK = 64
SLC_TOPK = 16
WINDOW = 512

GDN_HEADS = 16
GDN_CONV = 4
GDN_CHUNK = 64

D_FF = 5632

NSA_Q_DIM = NSA_HEADS * HEAD_DIM
NSA_KV_DIM = N_BRANCH * 2 * NSA_GROUPS * HEAD_DIM
GDN_DIM = GDN_HEADS * HEAD_DIM
OFF_Q = 0
OFF_KV = OFF_Q + NSA_Q_DIM
OFF_GQKV = OFF_KV + NSA_KV_DIM
OFF_Z = OFF_GQKV + 3 * GDN_DIM
OFF_MA = OFF_Z + GDN_DIM
OFF_MB = OFF_MA + D_MODEL
NP_BIG = OFF_MB + D_MODEL

GDN_HB = 8
GDN_NHG = GDN_HEADS // GDN_HB
NP_SMALL = (NSA_GROUPS + GDN_NHG) * 128

VMEM_LIMIT = 56 * 1024 * 1024

NT_DIMS = (((1,), (1,)), ((), ()))


def _dot(a, b):
    return jnp.dot(a, b, preferred_element_type=F32)


def _dot_nt(a, b):
    return lax.dot_general(a, b, NT_DIMS, preferred_element_type=F32)


def _cparams(sem):
    return pltpu.CompilerParams(dimension_semantics=sem, vmem_limit_bytes=VMEM_LIMIT)


def _in_proj_kernel(x_ref, g_ref, wt_ref, wst_ref, o_ref, os_ref, hn_ref):
    @pl.when(pl.program_id(1) == 0)
    def _():
        x = x_ref[...]
        y = x * lax.rsqrt(jnp.mean(x * x, axis=-1, keepdims=True) + EPS) * g_ref[...]
        hn = y.astype(BF16)
        hn_ref[...] = hn
        os_ref[...] = _dot_nt(hn, wst_ref[...])

    o_ref[...] = _dot_nt(hn_ref[...], wt_ref[...]).astype(o_ref.dtype)


SKIP_GATE = NSA_HEADS * N_BRANCH
SKIP_AB = 2 * GDN_HEADS


def _in_proj(x2, gain, w_t, w_small_t, bm=1024, bn=1024):
    t, d = x2.shape
    n = NP_BIG
    ns = w_small_t.shape[0]
    assert OFF_GQKV % bn == 0 and OFF_MA % bn == 0 and n % bn == 0

    def w_row(i, j):
        skip = jnp.where(j >= OFF_MA // bn, SKIP_GATE + SKIP_AB, jnp.where(j >= OFF_GQKV // bn, SKIP_GATE, 0))
        return pl.multiple_of(j * bn + skip, 16)

    return pl.pallas_call(
        _in_proj_kernel,
        out_shape=(jax.ShapeDtypeStruct((t, n), BF16), jax.ShapeDtypeStruct((t, ns), F32)),
        grid=(t // bm, n // bn),
        in_specs=[
            pl.BlockSpec((bm, d), lambda i, j: (i, 0)),
            pl.BlockSpec((1, d), lambda i, j: (0, 0)),
            pl.BlockSpec((pl.Element(bn), pl.Element(d)), lambda i, j: (w_row(i, j), 0)),
            pl.BlockSpec((ns, d), lambda i, j: (0, 0)),
        ],
        out_specs=(
            pl.BlockSpec((bm, bn), lambda i, j: (i, j)),
            pl.BlockSpec((bm, ns), lambda i, j: (i, 0)),
        ),
        scratch_shapes=[pltpu.VMEM((bm, d), BF16)],
        compiler_params=_cparams(("parallel", "arbitrary")),
        name="in_proj",
    )(x2, gain, w_t, w_small_t)


def _cmp_kernel(ck_ref, cv_ref, w_ref, pos_ref, kg_ref, kc_ref, vc_ref, xs_ref):
    dh = HEAD_DIM
    n = kc_ref.shape[0]

    def compress(x_ref, idx):
        xs_ref[...] = x_ref[...].astype(F32)
        lo = hi = None
        for l in range(CMP_STRIDE):
            xl = xs_ref[pl.ds(l, n, stride=CMP_STRIDE), :].astype(BF16)
            lo_l = _dot(xl, w_ref[idx, l * dh:(l + 1) * dh, :])
            hi_l = _dot(xl, w_ref[idx, (CMP_STRIDE + l) * dh:(CMP_STRIDE + l + 1) * dh, :])
            lo = lo_l if lo is None else lo + lo_l
            hi = hi_l if hi is None else hi + hi_l
        bias = _dot(pos_ref[idx], w_ref[idx])[0:1]
        return lo + pltpu.roll(hi, n - 1, 0) + bias

    kc = compress(ck_ref, 0)
    kc = kc * lax.rsqrt(jnp.mean(kc * kc, axis=-1, keepdims=True) + EPS) * kg_ref[...]
    kc_ref[...] = kc.astype(BF16)
    vc_ref[...] = compress(cv_ref, 1).astype(BF16)


def _nsa_compress(p3, w_cmp2, pos2, k_gain0):
    b, seq, _ = p3.shape
    g, dh = NSA_GROUPS, HEAD_DIM
    nc = seq // CMP_STRIDE
    width = CMP_BLOCK * dh
    kvb = OFF_KV // dh
    out = jax.ShapeDtypeStruct((b, g, nc, dh), BF16)
    return pl.pallas_call(
        _cmp_kernel,
        out_shape=(out, out),
        grid=(b, g),
        in_specs=[
            pl.BlockSpec((None, seq, dh), lambda i, j: (i, 0, kvb + j)),
            pl.BlockSpec((None, seq, dh), lambda i, j: (i, 0, kvb + g + j)),
            pl.BlockSpec((2, width, dh), lambda i, j: (0, 0, 0)),
            pl.BlockSpec((2, 8, width), lambda i, j: (0, 0, 0)),
            pl.BlockSpec((1, dh), lambda i, j: (0, 0)),
        ],
        out_specs=(
            pl.BlockSpec((None, None, nc, dh), lambda i, j: (i, j, 0, 0)),
            pl.BlockSpec((None, None, nc, dh), lambda i, j: (i, j, 0, 0)),
        ),
        scratch_shapes=[pltpu.VMEM((seq, dh), F32)],
        compiler_params=_cparams(("parallel", "parallel")),
        name="nsa_compress",
    )(p3, p3, w_cmp2, pos2, k_gain0)


NSA_TQ = 256
NSA_TK = 512
NORM_ROWS = 512
RANK_UNROLL = 4


def _rms(xf, gain):
    return xf * lax.rsqrt(jnp.mean(xf * xf, axis=-1, keepdims=True) + EPS) * gain


def _nsa_kernel(q_ref, gl_ref, ks_ref, vs_ref, kw_ref, vw_ref, kc_ref, vc_ref, aggt_ref, eye_ref, wb_ref,
                qg_ref, kg_ref, o_ref, ksa_ref, kwa_ref, vsa_ref, vwp_ref, sc_ref, s0_ref, s1_ref, m_ref, acc_ref,
                *, seq):
    tq, tk, rep, dh = NSA_TQ, NSA_TK, NSA_REP, HEAD_DIM
    rows = rep * tq
    n_sel = seq // SLC_BLOCK
    n_topk = min(SLC_TOPK, n_sel)
    qi = pl.program_id(2)
    q0 = qi * tq

    @pl.when(qi == 0)
    def _():
        kwa_ref[0:WINDOW, 0:dh] = jnp.zeros((WINDOW, dh), BF16)
        kwa_ref[0:WINDOW, dh:2 * dh] = jnp.where(
            lax.broadcasted_iota(jnp.int32, (WINDOW, dh), 1) == 0, 1.0, 0.0).astype(BF16)
        ones = jnp.ones((NORM_ROWS, dh), BF16)
        vwp_ref[0:WINDOW, 0:dh] = jnp.zeros((WINDOW, dh), BF16)
        vwp_ref[0:WINDOW, dh:2 * dh] = jnp.ones((WINDOW, dh), BF16)
        lane = lax.broadcasted_iota(jnp.int32, (NORM_ROWS, dh), 1)
        sub = lax.broadcasted_iota(jnp.int32, (NORM_ROWS, dh), 0)

        def body(c, carry):
            r0 = pl.multiple_of(c * NORM_ROWS, NORM_ROWS)
            r = pl.ds(r0, NORM_ROWS)
            rw = pl.ds(r0 + WINDOW, NORM_ROWS)
            ksa_ref[r, 0:dh] = _rms(ks_ref[r, :].astype(F32), kg_ref[1:2, :]).astype(BF16)
            ksa_ref[r, dh:2 * dh] = jnp.where((r0 + sub) // SLC_BLOCK == lane, 1.0, 0.0).astype(BF16)
            kwa_ref[rw, 0:dh] = _rms(kw_ref[r, :].astype(F32), kg_ref[2:3, :]).astype(BF16)
            kwa_ref[rw, dh:2 * dh] = jnp.zeros((NORM_ROWS, dh), BF16)
            vsa_ref[r, 0:dh] = vs_ref[r, :]
            vsa_ref[r, dh:2 * dh] = ones
            vwp_ref[rw, 0:dh] = vw_ref[r, :]
            vwp_ref[rw, dh:2 * dh] = ones
            return carry
        lax.fori_loop(0, seq // NORM_ROWS, body, 0)

    qscale = dh ** -0.5 * LOG2E
    q = q_ref[...].astype(F32)
    qs = jnp.concatenate(
        [(_rms(q[:, r * dh:(r + 1) * dh], qg_ref[...]) * qscale).astype(BF16) for r in range(rep)], axis=0)

    wk = WINDOW + tq
    padneg = jnp.where(lax.broadcasted_iota(jnp.int32, (rows, dh), 1) == 0, NEG_INF, 0.0).astype(BF16)
    qa_win = jnp.concatenate([qs, padneg], axis=1)
    kr = pl.ds(pl.multiple_of(q0, tq), wk)
    s_w = _dot_nt(qa_win, kwa_ref[kr, :]) + jnp.concatenate([wb_ref[...]] * rep, axis=0)

    n_cmp = kc_ref.shape[0]
    s = _dot_nt(qs, kc_ref[...])
    p_w = jnp.exp2(s_w - jnp.max(s_w, axis=-1, keepdims=True))
    ov_w = _dot(p_w.astype(BF16), vwp_ref[kr, :])
    o_win = ov_w[:, :dh] / ov_w[:, dh:]
    t_c = q0 + (lax.broadcasted_iota(jnp.int32, (rows, n_cmp), 0) & (tq - 1))
    cmp_end = lax.broadcasted_iota(jnp.int32, (rows, n_cmp), 1) * CMP_STRIDE + (CMP_BLOCK - 1)
    mask = cmp_end <= t_c
    s = jnp.where(mask, s, NEG_INF)
    e = jnp.where(mask, jnp.exp2(s - jnp.max(s, axis=-1, keepdims=True)), 0.0)
    vc_aug = jnp.concatenate([vc_ref[...], jnp.ones((n_cmp, dh), BF16)], axis=1)
    ov = _dot(e.astype(BF16), vc_aug)
    den = jnp.maximum(ov[:, dh:], TINY)
    o_cmp = ov[:, :dh] / den
    p = e / jnp.concatenate([den] * (n_cmp // dh), axis=1)
    p_sum = p[0:tq]
    for r in range(1, rep):
        p_sum = p_sum + p[r * tq:(r + 1) * tq]
    imp_t = _dot_nt(aggt_ref[...], p_sum.astype(BF16))

    jidx = lax.broadcasted_iota(jnp.int32, (n_sel, tq), 0)
    t_l = q0 + lax.broadcasted_iota(jnp.int32, (n_sel, tq), 1)
    cur = t_l // SLC_BLOCK
    forced = (jidx == 0) | (jidx == cur) | (jidx == cur - 1)
    causal = jidx * SLC_BLOCK <= t_l
    score = jnp.where(forced, FORCE_SCORE, jnp.where(causal, imp_t, -FORCE_SCORE))
    sc_ref[...] = score

    def rank_body(it, rank):
        for u in range(RANK_UNROLL):
            i = it * RANK_UNROLL + u
            row = sc_ref[pl.ds(i, 1), :]
            ge = jnp.where(row >= score, 1.0, 0.0)
            gt = jnp.where(row > score, 1.0, 0.0)
            rank = rank + jnp.where(jidx > i, ge, gt)
        return rank

    n_blk = (q0 + tq - 1) // SLC_BLOCK + 1
    n_it = (n_blk + RANK_UNROLL - 1) // RANK_UNROLL
    rank = lax.fori_loop(0, n_it, rank_body, jnp.zeros((n_sel, tq), F32))
    selneg_t = jnp.where(rank < n_topk, 0.0, NEG_INF)
    if n_sel < dh:
        selneg_t = jnp.concatenate([selneg_t, jnp.zeros((dh - n_sel, tq), F32)], axis=0)
    selneg = _dot_nt(eye_ref[...], selneg_t.astype(BF16)).astype(BF16)
    qa_sel = jnp.concatenate([qs, jnp.concatenate([selneg] * rep, axis=0)], axis=1)

    def qk_tile(kt):
        return _dot_nt(qa_sel, ksa_ref[pl.ds(pl.multiple_of(kt * tk, tk), tk), :])

    def consume(kt, buf, diagonal):
        k0 = pl.multiple_of(kt * tk, tk)
        s = buf[...]
        if diagonal:
            t_r = q0 + (lax.broadcasted_iota(jnp.int32, (rows, tk), 0) & (tq - 1))
            kpos = k0 + lax.broadcasted_iota(jnp.int32, (rows, tk), 1)
            s = jnp.where(kpos <= t_r, s, NEG_INF)
        m_prev = m_ref[...]
        m_new = jnp.maximum(m_prev, jnp.max(s, axis=-1, keepdims=True))
        p = jnp.exp2(s - jnp.concatenate([m_new] * (tk // 128), axis=1))
        alpha = jnp.exp2(m_prev - m_new)
        m_ref[...] = m_new
        acc_ref[...] = (jnp.concatenate([alpha, alpha], axis=1) * acc_ref[...]
                        + _dot(p.astype(BF16), vsa_ref[pl.ds(k0, tk), :]))

    m_ref[...] = jnp.full((rows, 128), NEG_INF, F32)
    acc_ref[...] = jnp.zeros((rows, 2 * dh), F32)
    s0_ref[...] = qk_tile(0)
    n_full = q0 // tk

    def sel_body(it, carry):
        kt = 2 * it
        s1_ref[...] = qk_tile(kt + 1)
        consume(kt, s0_ref, False)
        s0_ref[...] = qk_tile(kt + 2)
        consume(kt + 1, s1_ref, False)
        return carry

    lax.fori_loop(0, n_full // 2, sel_body, 0)

    @pl.when(n_full % 2 == 1)
    def _():
        s1_ref[...] = qk_tile(n_full)
        consume(n_full - 1, s0_ref, False)
        consume(n_full, s1_ref, True)

    @pl.when(n_full % 2 == 0)
    def _():
        consume(n_full, s0_ref, True)

    o_slc = acc_ref[:, 0:dh] / acc_ref[:, dh:2 * dh]

    gates = jax.nn.sigmoid(gl_ref[...])
    for r in range(rep):
        rs = slice(r * tq, (r + 1) * tq)
        o = (gates[:, 3 * r:3 * r + 1] * o_cmp[rs] + gates[:, 3 * r + 1:3 * r + 2] * o_slc[rs]
             + gates[:, 3 * r + 2:3 * r + 3] * o_win[rs])
        o_ref[:, r * dh:(r + 1) * dh] = o.astype(o_ref.dtype)


def _nsa_attention(p3, ps3, kc, vc, agg_t, eye, win_bias, q_gain, k_gain):
    b, seq, _ = p3.shape
    g, tq, dh = NSA_GROUPS, NSA_TQ, HEAD_DIM
    n_sel = seq // SLC_BLOCK
    assert n_sel <= dh and n_sel % 8 == 0 and seq % NSA_TK == 0
    nc = kc.shape[2]
    kvb = OFF_KV // dh

    def kv_spec(branch, is_v):
        base = kvb + (branch * 2 + is_v) * g
        return pl.BlockSpec((None, seq, dh), lambda i, j, k: (i, 0, base + j))

    qw = NSA_REP * dh
    return pl.pallas_call(
        functools.partial(_nsa_kernel, seq=seq),
        out_shape=jax.ShapeDtypeStruct((b, seq, NSA_Q_DIM), BF16),
        grid=(b, g, seq // tq),
        in_specs=[
            pl.BlockSpec((None, tq, qw), lambda i, j, k: (i, k, j)),
            pl.BlockSpec((None, tq, 128), lambda i, j, k: (i, k, j)),
            kv_spec(1, 0), kv_spec(1, 1), kv_spec(2, 0), kv_spec(2, 1),
            pl.BlockSpec((None, None, nc, dh), lambda i, j, k: (i, j, 0, 0)),
            pl.BlockSpec((None, None, nc, dh), lambda i, j, k: (i, j, 0, 0)),
            pl.BlockSpec((n_sel, nc), lambda i, j, k: (0, 0)),
            pl.BlockSpec((tq, tq), lambda i, j, k: (0, 0)),
            pl.BlockSpec((tq, WINDOW + tq), lambda i, j, k: (0, 0)),
            pl.BlockSpec((1, dh), lambda i, j, k: (0, 0)),
            pl.BlockSpec((N_BRANCH, dh), lambda i, j, k: (0, 0)),
        ],
        out_specs=pl.BlockSpec((None, tq, qw), lambda i, j, k: (i, k, j)),
        scratch_shapes=[
            pltpu.VMEM((seq, 2 * dh), BF16),
            pltpu.VMEM((seq + WINDOW, 2 * dh), BF16),
            pltpu.VMEM((seq, 2 * dh), BF16),
            pltpu.VMEM((seq + WINDOW, 2 * dh), BF16),
            pltpu.VMEM((n_sel, tq), F32),
            pltpu.VMEM((NSA_REP * tq, NSA_TK), F32),
            pltpu.VMEM((NSA_REP * tq, NSA_TK), F32),
            pltpu.VMEM((NSA_REP * tq, 128), F32),
            pltpu.VMEM((NSA_REP * tq, 2 * dh), F32),
        ],
        compiler_params=_cparams(("parallel", "parallel", "arbitrary")),
        name="nsa_attention",
    )(p3, ps3, p3, p3, p3, p3, kc, vc, agg_t, eye, win_bias, q_gain, k_gain)


GDN_TS = 256
GDN_INV = 128
HALO = 8


def _gdn_kernel(q_ref, k_ref, v_ref, z_ref, ab_ref, cw_ref, alog_ref, dtb_ref, og_ref, sh_ref, eye_ref,
                o_ref, xe_ref, st_ref):
    ts, hb, dh, ch = GDN_TS, GDN_HB, HEAD_DIM, GDN_CHUNK
    ti = pl.program_id(2)

    n_inv = ts // GDN_INV
    n_ch = ts // ch

    @pl.when(ti == 0)
    def _():
        xe_ref[...] = jnp.zeros_like(xe_ref)
        st_ref[...] = jnp.zeros_like(st_ref)

    row8 = lax.broadcasted_iota(jnp.int32, (HALO, hb * dh), 0)
    conv = []
    for part, ref in enumerate((q_ref, k_ref, v_ref)):
        xb = ref[...]
        xf = xb.astype(F32)
        prev = xe_ref[part]
        acc = cw_ref[part, GDN_CONV - 1:GDN_CONV, :] * xf
        corr = jnp.zeros((HALO, hb * dh), F32)
        for d in range(1, GDN_CONV):
            wd = cw_ref[part, GDN_CONV - 1 - d:GDN_CONV - d, :]
            acc = acc + wd * _dot(sh_ref[d - 1], xb)
            corr = corr + wd * jnp.where(row8 < d, pltpu.roll(prev, d, 0), 0.0)
        acc = jnp.concatenate([acc[0:HALO] + corr, acc[HALO:]], axis=0)
        conv.append(acc * jax.nn.sigmoid(acc))
        xe_ref[part] = xf[ts - HALO:ts]

    ab = ab_ref[...]
    xg = ab + dtb_ref[...]
    softplus = jnp.maximum(xg, 0.0) + jnp.log(1.0 + jnp.exp(-jnp.abs(xg)))
    g_cum = -jnp.exp(alog_ref[...]) * softplus
    beta = jax.nn.sigmoid(ab)
    row_in_chunk = lax.broadcasted_iota(jnp.int32, (ts, 128), 0) & (ch - 1)
    step = 1
    while step < ch:
        g_cum = g_cum + jnp.where(row_in_chunk >= step, pltpu.roll(g_cum, step, 0), 0.0)
        step *= 2
    g_end = jnp.concatenate(
        [jnp.broadcast_to(g_cum[(c + 1) * ch - 1:(c + 1) * ch, :], (ch, 128)) for c in range(n_ch)], axis=0)
    e_cum = jnp.exp(g_cum)
    e_rel = jnp.exp(g_end - g_cum)
    e_end = jnp.exp(g_end)
    g_cum_t = jnp.transpose(g_cum)

    ri = lax.broadcasted_iota(jnp.int32, (ts, ts), 0)
    ci = lax.broadcasted_iota(jnp.int32, (ts, ts), 1)
    same = (ri // ch) == (ci // ch)
    tril = same & (ri >= ci)
    strict = same & (ri > ci)
    ri_b = lax.broadcasted_iota(jnp.int32, (GDN_INV, GDN_INV), 0)
    ci_b = lax.broadcasted_iota(jnp.int32, (GDN_INV, GDN_INV), 1)
    ident = jnp.where(ri_b == ci_b, 1.0, 0.0)
    eye_bf = eye_ref[...]

    heads = []
    for hh in range(hb):
        cs = slice(hh * dh, (hh + 1) * dh)
        qh, kh, vh = conv[0][:, cs], conv[1][:, cs], conv[2][:, cs]
        qh = qh * lax.rsqrt(jnp.sum(qh * qh, axis=-1, keepdims=True) + EPS) * (dh ** -0.5)
        kh = kh * lax.rsqrt(jnp.sum(kh * kh, axis=-1, keepdims=True) + EPS)
        bt = beta[:, hb + hh:hb + hh + 1]
        diff = g_cum[:, hh:hh + 1] - g_cum_t[hh:hh + 1, :]
        e = jnp.exp(jnp.minimum(diff, 0.0))
        kb = kh * bt
        k_bf = kh.astype(BF16)
        a_mat = _dot_nt(kb.astype(BF16), k_bf) * jnp.where(strict, e, 0.0)
        qk = (_dot_nt(qh.astype(BF16), k_bf) * jnp.where(tril, e, 0.0)).astype(BF16)
        heads.append(dict(
            a=a_mat, qk=qk,
            vb=(vh * bt).astype(BF16),
            kbg=(kb * e_cum[:, hh:hh + 1]).astype(BF16),
            qg=qh * e_cum[:, hh:hh + 1],
        ))
        kd = (kh * e_rel[:, hh:hh + 1]).astype(BF16)
        heads[-1]["kdt"] = [_dot_nt(eye_bf, kd[c * ch:(c + 1) * ch]).astype(BF16)
                            for c in range(n_ch)]

    probs = []
    for hh in range(hb):
        for blk in range(n_inv):
            bs = slice(blk * GDN_INV, (blk + 1) * GDN_INV)
            pw = -heads[hh]["a"][bs, bs]
            probs.append([pw, ident + pw])
    for _ in range(5):
        for pr in probs:
            pw_bf = pr[0].astype(BF16)
            pr[0] = _dot(pw_bf, pw_bf)
        for pr in probs:
            pr[1] = pr[1] + _dot(pr[1].astype(BF16), pr[0].astype(BF16))
    for hh in range(hb):
        u_parts, w_parts = [], []
        for blk in range(n_inv):
            bs = slice(blk * GDN_INV, (blk + 1) * GDN_INV)
            t_bf = probs[hh * n_inv + blk][1].astype(BF16)
            u_parts.append(_dot(t_bf, heads[hh]["vb"][bs]))
            w_parts.append(_dot(t_bf, heads[hh]["kbg"][bs]))
        heads[hh]["u"] = jnp.concatenate(u_parts, axis=0).astype(BF16)
        heads[hh]["w"] = jnp.concatenate(w_parts, axis=0).astype(BF16)
    for hd in heads:
        hd["qkw"] = _dot(hd["qk"], hd["w"])
        hd["qku"] = _dot(hd["qk"], hd["u"])
        hd["mw"] = [_dot(hd["kdt"][c], jnp.concatenate([hd["w"][c * ch:(c + 1) * ch],
                                                        hd["u"][c * ch:(c + 1) * ch]], axis=1))
                    for c in range(n_ch)]
    for hd in heads:
        hd["qp"] = (hd["qg"] - hd["qkw"]).astype(BF16)
        hd["mc"] = [mw[:, :dh].astype(BF16) for mw in hd["mw"]]
        hd["bc"] = [mw[:, dh:] for mw in hd["mw"]]

    states = [st_ref[hh] for hh in range(hb)]
    outs = [[] for _ in range(hb)]
    for c in range(n_ch):
        rs = slice(c * ch, (c + 1) * ch)
        for hh in range(hb):
            hd = heads[hh]
            r = _dot(jnp.concatenate([hd["qp"][rs], hd["mc"][c]], axis=0), states[hh].astype(BF16))
            outs[hh].append(r[:ch] + hd["qku"][rs])
            states[hh] = states[hh] * e_end[c * ch:c * ch + 1, hh:hh + 1] - r[ch:] + hd["bc"][c]

    for hh in range(hb):
        cs = slice(hh * dh, (hh + 1) * dh)
        st_ref[hh] = states[hh]
        o = jnp.concatenate(outs[hh], axis=0)
        o = o * lax.rsqrt(jnp.mean(o * o, axis=-1, keepdims=True) + EPS) * og_ref[...]
        z = z_ref[:, cs].astype(F32)
        o_ref[:, cs] = (o * (z * jax.nn.sigmoid(z))).astype(o_ref.dtype)


def _gdn(p3, ps3, conv_w3, alog_l, dtb_l, out_gain, shifts, eye):
    b, seq, _ = p3.shape
    ts, hb, dh = GDN_TS, GDN_HB, HEAD_DIM
    wb = hb * dh
    qb, zb = OFF_GQKV // wb, OFF_Z // wb
    gw = GDN_DIM // wb
    return pl.pallas_call(
        _gdn_kernel,
        out_shape=jax.ShapeDtypeStruct((b, seq, GDN_DIM), BF16),
        grid=(b, GDN_NHG, seq // ts),
        in_specs=[
            pl.BlockSpec((None, ts, wb), lambda i, j, k: (i, k, qb + j)),
            pl.BlockSpec((None, ts, wb), lambda i, j, k: (i, k, qb + gw + j)),
            pl.BlockSpec((None, ts, wb), lambda i, j, k: (i, k, qb + 2 * gw + j)),
            pl.BlockSpec((None, ts, wb), lambda i, j, k: (i, k, zb + j)),
            pl.BlockSpec((None, ts, 128), lambda i, j, k: (i, k, NSA_GROUPS + j)),
            pl.BlockSpec((3, GDN_CONV, wb), lambda i, j, k: (0, 0, j)),
            pl.BlockSpec((None, 1, 128), lambda i, j, k: (j, 0, 0)),
            pl.BlockSpec((None, 1, 128), lambda i, j, k: (j, 0, 0)),
            pl.BlockSpec((1, dh), lambda i, j, k: (0, 0)),
            pl.BlockSpec((GDN_CONV - 1, ts, ts), lambda i, j, k: (0, 0, 0)),
            pl.BlockSpec((dh, dh), lambda i, j, k: (0, 0)),
        ],
        out_specs=pl.BlockSpec((None, ts, wb), lambda i, j, k: (i, k, j)),
        scratch_shapes=[
            pltpu.VMEM((3, HALO, wb), F32),
            pltpu.VMEM((hb, dh, dh), F32),
        ],
        compiler_params=_cparams(("parallel", "parallel", "arbitrary")),
        name="gdn",
    )(p3, p3, p3, p3, ps3, conv_w3, alog_l, dtb_l, out_gain, shifts, eye)


def _merge_kernel(oa_ref, ob_ref, wa_ref, wb_ref, ma_ref, mb_ref, o_ref):
    ya = _dot(oa_ref[...], wa_ref[...])
    yb = _dot(ob_ref[...], wb_ref[...])
    mix = jax.nn.sigmoid(ma_ref[...].astype(F32)) * ya + jax.nn.sigmoid(mb_ref[...].astype(F32)) * yb
    o_ref[...] = mix.astype(o_ref.dtype)


def _merge(oa, ob, wa, wb, p2, bm=512, bn=1024):
    t, d = oa.shape
    n = wa.shape[1]
    ma_b, mb_b = OFF_MA // bn, OFF_MB // bn
    return pl.pallas_call(
        _merge_kernel,
        out_shape=jax.ShapeDtypeStruct((t, n), BF16),
        grid=(n // bn, t // bm),
        in_specs=[
            pl.BlockSpec((bm, d), lambda j, i: (i, 0)),
            pl.BlockSpec((bm, d), lambda j, i: (i, 0)),
            pl.BlockSpec((d, bn), lambda j, i: (0, j)),
            pl.BlockSpec((d, bn), lambda j, i: (0, j)),
            pl.BlockSpec((bm, bn), lambda j, i: (i, ma_b + j)),
            pl.BlockSpec((bm, bn), lambda j, i: (i, mb_b + j)),
        ],
        out_specs=pl.BlockSpec((bm, bn), lambda j, i: (i, j)),
        compiler_params=_cparams(("parallel", "parallel")),
        name="merge",
    )(oa, ob, wa, wb, p2, p2)


def _resid_matmul_kernel(a_ref, w_ref, r_ref, o_ref):
    o_ref[...] = r_ref[...] + _dot(a_ref[...], w_ref[...])


def _resid_matmul(a, w, resid, bm, bn):
    t, k = a.shape
    n = w.shape[1]
    return pl.pallas_call(
        _resid_matmul_kernel,
        out_shape=jax.ShapeDtypeStruct((t, n), F32),
        grid=(n // bn, t // bm),
        in_specs=[
            pl.BlockSpec((bm, k), lambda j, i: (i, 0)),
            pl.BlockSpec((k, bn), lambda j, i: (0, j)),
            pl.BlockSpec((bm, bn), lambda j, i: (i, j)),
        ],
        out_specs=pl.BlockSpec((bm, bn), lambda j, i: (i, j)),
        compiler_params=_cparams(("parallel", "parallel")),
        name="resid_matmul",
    )(a, w, resid)


def _ffn_up_kernel(x_ref, g_ref, wg_ref, wu_ref, o_ref, hn_ref):
    @pl.when(pl.program_id(1) == 0)
    def _():
        x = x_ref[...]
        hn_ref[...] = (x * lax.rsqrt(jnp.mean(x * x, axis=-1, keepdims=True) + EPS) * g_ref[...]).astype(BF16)

    hn = hn_ref[...]
    gate = _dot(hn, wg_ref[...].astype(BF16))
    up = _dot(hn, wu_ref[...].astype(BF16))
    o_ref[...] = (gate * jax.nn.sigmoid(gate) * up).astype(o_ref.dtype)


def _ffn_up(x1, gain, wg, wu, bm=1024, bn=512):
    t, d = x1.shape
    n = wg.shape[1]
    return pl.pallas_call(
        _ffn_up_kernel,
        out_shape=jax.ShapeDtypeStruct((t, n), BF16),
        grid=(t // bm, n // bn),
        in_specs=[
            pl.BlockSpec((bm, d), lambda i, j: (i, 0)),
            pl.BlockSpec((1, d), lambda i, j: (0, 0)),
            pl.BlockSpec((d, bn), lambda i, j: (0, j)),
            pl.BlockSpec((d, bn), lambda i, j: (0, j)),
        ],
        out_specs=pl.BlockSpec((bm, bn), lambda i, j: (i, j)),
        scratch_shapes=[pltpu.VMEM((bm, d), BF16)],
        compiler_params=_cparams(("parallel", "arbitrary")),
        name="ffn_up",
    )(x1, gain, wg, wu)


def _pack_in_weights(w):
    w_t = jnp.transpose(w).astype(BF16)
    off_gate = OFF_GQKV
    off_a = OFF_MA + SKIP_GATE
    gate_t = w_t[off_gate:off_gate + SKIP_GATE]
    a_t = w_t[off_a:off_a + GDN_HEADS]
    b_t = w_t[off_a + GDN_HEADS:off_a + SKIP_AB]
    small = []
    per_group = NSA_REP * N_BRANCH
    for g in range(NSA_GROUPS):
        small.append(jnp.pad(gate_t[g * per_group:(g + 1) * per_group], ((0, 128 - per_group), (0, 0))))
    for hg in range(GDN_NHG):
        hs = slice(hg * GDN_HB, (hg + 1) * GDN_HB)
        small.append(jnp.pad(jnp.concatenate([a_t[hs], b_t[hs]], axis=0), ((0, 128 - 2 * GDN_HB), (0, 0))))
    w_small_t = jnp.concatenate(small, axis=0)
    assert w_t.shape[0] == NP_BIG + SKIP_GATE + SKIP_AB and w_small_t.shape[0] == NP_SMALL
    return w_t, w_small_t


def _lane_rows(v):
    return jnp.pad(v.reshape(GDN_NHG, 1, GDN_HB).astype(F32), ((0, 0), (0, 0), (0, 128 - GDN_HB)))


def _nsa_constants(seq):
    n_cmp = (seq - CMP_BLOCK) // CMP_STRIDE + 1
    nc = seq // CMP_STRIDE
    n_sel = seq // SLC_BLOCK
    cmp_start = np.arange(n_cmp) * CMP_STRIDE
    sel_start = np.arange(n_sel) * SLC_BLOCK
    overlap = (np.minimum(cmp_start[:, None] + CMP_BLOCK, sel_start[None, :] + SLC_BLOCK)
               - np.maximum(cmp_start[:, None], sel_start[None, :]))
    agg = np.zeros((nc, n_sel), np.float32)
    agg[:n_cmp] = np.clip(overlap, 0, None) / CMP_BLOCK
    eye = np.eye(NSA_TQ, dtype=np.float32)
    i = np.arange(NSA_TQ)[:, None]
    c = np.arange(WINDOW + NSA_TQ)[None, :]
    win_bias = np.where((c > i) & (c <= i + WINDOW), 0.0, NEG_INF).astype(np.float32)
    return jnp.asarray(agg.T, BF16), jnp.asarray(eye, BF16), jnp.asarray(win_bias)


def _gdn_constants():
    idx = np.arange(GDN_TS)
    shifts = np.stack([(idx[None, :] == idx[:, None] - d) for d in range(1, GDN_CONV)]).astype(np.float32)
    return jnp.asarray(shifts, BF16), jnp.asarray(np.eye(HEAD_DIM, dtype=np.float32), BF16)


def _layer(x, attn_norm, w_in, nsa_q_norm, nsa_k_norm, cmp_pos, w_cmp, gdn_conv, gdn_a_log, gdn_dt_bias,
           gdn_out_norm, w_branch_a, w_branch_b, w_out, ffn_norm, w_gate, w_up, w_down):
    b, seq, d = x.shape
    t = b * seq
    x2 = x.reshape(t, d)

    w_t, w_small_t = _pack_in_weights(w_in)
    p2, ps2 = _in_proj(x2, attn_norm.reshape(1, d), w_t, w_small_t)
    p3 = p2.reshape(b, seq, NP_BIG)
    ps3 = ps2.reshape(b, seq, NP_SMALL)

    w_cmp2 = w_cmp.reshape(2, CMP_BLOCK * HEAD_DIM, HEAD_DIM).astype(BF16)
    pos2 = jnp.broadcast_to(cmp_pos.reshape(2, 1, CMP_BLOCK * HEAD_DIM), (2, 8, CMP_BLOCK * HEAD_DIM)).astype(BF16)
    kc, vc = _nsa_compress(p3, w_cmp2, pos2, nsa_k_norm[0:1])

    agg_t, eye_q, win_bias = _nsa_constants(seq)
    o_a = _nsa_attention(p3, ps3, kc, vc, agg_t, eye_q, win_bias, nsa_q_norm.reshape(1, HEAD_DIM), nsa_k_norm)

    shifts, eye_h = _gdn_constants()
    conv_w3 = gdn_conv.reshape(GDN_CONV, 3, GDN_DIM).transpose(1, 0, 2)
    o_b = _gdn(p3, ps3, conv_w3, _lane_rows(gdn_a_log), _lane_rows(gdn_dt_bias),
               gdn_out_norm.reshape(1, HEAD_DIM), shifts, eye_h)

    mix = _merge(o_a.reshape(t, NSA_Q_DIM), o_b.reshape(t, GDN_DIM), w_branch_a.astype(BF16),
                 w_branch_b.astype(BF16), p2)
    x1 = _resid_matmul(mix, w_out.astype(BF16), x2, bm=1024, bn=1024)

    act = _ffn_up(x1, ffn_norm.reshape(1, d), w_gate, w_up)
    out = _resid_matmul(act, w_down.astype(BF16), x1, bm=512, bn=1024)
    return out.reshape(b, seq, d)


def kernel(x, attn_norm, w_in, nsa_q_norm, nsa_k_norm, cmp_pos, w_cmp, gdn_conv, gdn_a_log, gdn_dt_bias,
           gdn_out_norm, w_branch_a, w_branch_b, w_out, ffn_norm, w_gate, w_up, w_down):
    for l in range(attn_norm.shape[0]):
        x = _layer(x, attn_norm[l], w_in[l], nsa_q_norm[l], nsa_k_norm[l], cmp_pos[l], w_cmp[l], gdn_conv[l],
                   gdn_a_log[l], gdn_dt_bias[l], gdn_out_norm[l], w_branch_a[l], w_branch_b[l], w_out[l],
                   ffn_norm[l], w_gate[l], w_up[l], w_down[l])
    return x
```

```python
import functools

import numpy as np
import jax
import jax.numpy as jnp
from jax import lax
from jax.experimental import pallas as pl
from jax.experimental.pallas import tpu as pltpu

F32 = jnp.float32
BF16 = jnp.bfloat16

D_MODEL = 2048
EPS = 1e-6
NEG_INF = -1e30
TINY = 1e-30
FORCE_SCORE = 1e4
LOG2E = 1.4426950408889634

NSA_HEADS = 16
NSA_GROUPS = 4
NSA_REP = NSA_HEADS // NSA_GROUPS
HEAD_DIM = 128
N_BRANCH = 3
CMP_BLOCK = 32
CMP_STRIDE = 16
SLC_BLOCK = 64
SLC_TOPK = 16
WINDOW = 512

GDN_HEADS = 16
GDN_CONV = 4
GDN_CHUNK = 64

D_FF = 5632

NSA_Q_DIM = NSA_HEADS * HEAD_DIM
NSA_KV_DIM = N_BRANCH * 2 * NSA_GROUPS * HEAD_DIM
GDN_DIM = GDN_HEADS * HEAD_DIM
OFF_Q = 0
OFF_KV = OFF_Q + NSA_Q_DIM
OFF_GQKV = OFF_KV + NSA_KV_DIM
OFF_Z = OFF_GQKV + 3 * GDN_DIM
OFF_MA = OFF_Z + GDN_DIM
OFF_MB = OFF_MA + D_MODEL
NP_BIG = OFF_MB + D_MODEL

GDN_HB = 8
GDN_NHG = GDN_HEADS // GDN_HB
NP_SMALL = (NSA_GROUPS + GDN_NHG) * 128

VMEM_LIMIT = 56 * 1024 * 1024

NT_DIMS = (((1,), (1,)), ((), ()))


def _dot(a, b):
    return jnp.dot(a, b, preferred_element_type=F32)


def _dot_nt(a, b):
    return lax.dot_general(a, b, NT_DIMS, preferred_element_type=F32)


def _dot_tn(a, b):
    return lax.dot_general(a, b, (((0,), (0,)), ((), ())), preferred_element_type=F32)


def _cparams(sem):
    return pltpu.CompilerParams(dimension_semantics=sem, vmem_limit_bytes=VMEM_LIMIT)


def _in_proj_kernel(x_ref, g_ref, wt_ref, wst_ref, o_ref, os_ref, hn_ref):
    @pl.when(pl.program_id(1) == 0)
    def _():
        x = x_ref[...]
        y = x * lax.rsqrt(jnp.mean(x * x, axis=-1, keepdims=True) + EPS) * g_ref[...]
        hn = y.astype(BF16)
        hn_ref[...] = hn
        os_ref[...] = _dot_nt(hn, wst_ref[...])

    o_ref[...] = _dot_nt(hn_ref[...], wt_ref[...]).astype(o_ref.dtype)


SKIP_GATE = NSA_HEADS * N_BRANCH
SKIP_AB = 2 * GDN_HEADS


def _in_proj(x2, gain, w_t, w_small_t, bm=1024, bn=1024):
    t, d = x2.shape
    n = NP_BIG
    ns = w_small_t.shape[0]
    assert OFF_GQKV % bn == 0 and OFF_MA % bn == 0 and n % bn == 0

    def w_row(i, j):
        skip = jnp.where(j >= OFF_MA // bn, SKIP_GATE + SKIP_AB, jnp.where(j >= OFF_GQKV // bn, SKIP_GATE, 0))
        return pl.multiple_of(j * bn + skip, 16)

    return pl.pallas_call(
        _in_proj_kernel,
        out_shape=(jax.ShapeDtypeStruct((t, n), BF16), jax.ShapeDtypeStruct((t, ns), F32)),
        grid=(t // bm, n // bn),
        in_specs=[
            pl.BlockSpec((bm, d), lambda i, j: (i, 0)),
            pl.BlockSpec((1, d), lambda i, j: (0, 0)),
            pl.BlockSpec((pl.Element(bn), pl.Element(d)), lambda i, j: (w_row(i, j), 0)),
            pl.BlockSpec((ns, d), lambda i, j: (0, 0)),
        ],
        out_specs=(
            pl.BlockSpec((bm, bn), lambda i, j: (i, j)),
            pl.BlockSpec((bm, ns), lambda i, j: (i, 0)),
        ),
        scratch_shapes=[pltpu.VMEM((bm, d), BF16)],
        compiler_params=_cparams(("parallel", "arbitrary")),
        name="in_proj",
    )(x2, gain, w_t, w_small_t)


def _cmp_kernel(ck_ref, cv_ref, w_ref, pos_ref, kg_ref, kc_ref, vc_ref, xs_ref):
    dh = HEAD_DIM
    n = kc_ref.shape[0]

    def compress(x_ref, idx):
        xs_ref[...] = x_ref[...].astype(F32)
        lo = hi = None
        for l in range(CMP_STRIDE):
            xl = xs_ref[pl.ds(l, n, stride=CMP_STRIDE), :].astype(BF16)
            lo_l = _dot(xl, w_ref[idx, l * dh:(l + 1) * dh, :])
            hi_l = _dot(xl, w_ref[idx, (CMP_STRIDE + l) * dh:(CMP_STRIDE + l + 1) * dh, :])
            lo = lo_l if lo is None else lo + lo_l
            hi = hi_l if hi is None else hi + hi_l
        bias = _dot(pos_ref[idx], w_ref[idx])[0:1]
        return lo + pltpu.roll(hi, n - 1, 0) + bias

    kc = compress(ck_ref, 0)
    kc = kc * lax.rsqrt(jnp.mean(kc * kc, axis=-1, keepdims=True) + EPS) * kg_ref[...]
    kc_ref[...] = kc.astype(BF16)
    vc_ref[...] = compress(cv_ref, 1).astype(BF16)


def _nsa_compress(p3, w_cmp2, pos2, k_gain0):
    b, seq, _ = p3.shape
    g, dh = NSA_GROUPS, HEAD_DIM
    nc = seq // CMP_STRIDE
    width = CMP_BLOCK * dh
    kvb = OFF_KV // dh
    out = jax.ShapeDtypeStruct((b, g, nc, dh), BF16)
    return pl.pallas_call(
        _cmp_kernel,
        out_shape=(out, out),
        grid=(b, g),
        in_specs=[
            pl.BlockSpec((None, seq, dh), lambda i, j: (i, 0, kvb + j)),
            pl.BlockSpec((None, seq, dh), lambda i, j: (i, 0, kvb + g + j)),
            pl.BlockSpec((2, width, dh), lambda i, j: (0, 0, 0)),
            pl.BlockSpec((2, 8, width), lambda i, j: (0, 0, 0)),
            pl.BlockSpec((1, dh), lambda i, j: (0, 0)),
        ],
        out_specs=(
            pl.BlockSpec((None, None, nc, dh), lambda i, j: (i, j, 0, 0)),
            pl.BlockSpec((None, None, nc, dh), lambda i, j: (i, j, 0, 0)),
        ),
        scratch_shapes=[pltpu.VMEM((seq, dh), F32)],
        compiler_params=_cparams(("parallel", "parallel")),
        name="nsa_compress",
    )(p3, p3, w_cmp2, pos2, k_gain0)


NSA_TQ = 256
NSA_TK = 512
NORM_ROWS = 512
RANK_UNROLL = 4


def _rms(xf, gain):
    return xf * lax.rsqrt(jnp.mean(xf * xf, axis=-1, keepdims=True) + EPS) * gain


def _nsa_kernel(q_ref, gl_ref, ks_ref, vs_ref, kw_ref, vw_ref, kc_ref, vc_ref, aggt_ref, eye_ref, wb_ref, cq_ref,
                qg_ref, kg_ref, o_ref, ksa_ref, kwa_ref, vsa_ref, vwp_ref, sc_ref, s0_ref, s1_ref, m_ref, acc_ref,
                *, seq):
    tq, tk, rep, dh = NSA_TQ, NSA_TK, NSA_REP, HEAD_DIM
    rows = rep * tq
    n_sel = seq // SLC_BLOCK
    n_topk = min(SLC_TOPK, n_sel)
    qi = pl.program_id(2)
    q0 = qi * tq

    @pl.when(qi == 0)
    def _():
        kwa_ref[0:WINDOW, 0:dh] = jnp.zeros((WINDOW, dh), BF16)
        kwa_ref[0:WINDOW, dh:2 * dh] = jnp.where(
            lax.broadcasted_iota(jnp.int32, (WINDOW, dh), 1) == 0, 1.0, 0.0).astype(BF16)
        ones = jnp.ones((NORM_ROWS, dh), BF16)
        vwp_ref[0:WINDOW, 0:dh] = jnp.zeros((WINDOW, dh), BF16)
        vwp_ref[0:WINDOW, dh:2 * dh] = jnp.ones((WINDOW, dh), BF16)
        lane = lax.broadcasted_iota(jnp.int32, (NORM_ROWS, dh), 1)
        sub = lax.broadcasted_iota(jnp.int32, (NORM_ROWS, dh), 0)

        def body(c, carry):
            r0 = pl.multiple_of(c * NORM_ROWS, NORM_ROWS)
            r = pl.ds(r0, NORM_ROWS)
            rw = pl.ds(r0 + WINDOW, NORM_ROWS)
            ksa_ref[r, 0:dh] = _rms(ks_ref[r, :].astype(F32), kg_ref[1:2, :]).astype(BF16)
            ksa_ref[r, dh:2 * dh] = jnp.where((r0 + sub) // SLC_BLOCK == lane, 1.0, 0.0).astype(BF16)
            kwa_ref[rw, 0:dh] = _rms(kw_ref[r, :].astype(F32), kg_ref[2:3, :]).astype(BF16)
            kwa_ref[rw, dh:2 * dh] = jnp.zeros((NORM_ROWS, dh), BF16)
            vsa_ref[r, 0:dh] = vs_ref[r, :]
            vsa_ref[r, dh:2 * dh] = ones
            vwp_ref[rw, 0:dh] = vw_ref[r, :]
            vwp_ref[rw, dh:2 * dh] = ones
            return carry
        lax.fori_loop(0, seq // NORM_ROWS, body, 0)

    qscale = dh ** -0.5 * LOG2E
    q = q_ref[...].astype(F32)
    qs = jnp.concatenate(
        [(_rms(q[:, r * dh:(r + 1) * dh], qg_ref[...]) * qscale).astype(BF16) for r in range(rep)], axis=0)

    wk = WINDOW + tq
    padneg = jnp.where(lax.broadcasted_iota(jnp.int32, (rows, dh), 1) == 0, NEG_INF, 0.0).astype(BF16)
    qa_win = jnp.concatenate([qs, padneg], axis=1)
    kr = pl.ds(pl.multiple_of(q0, tq), wk)
    s_w = _dot_nt(qa_win, kwa_ref[kr, :]) + jnp.concatenate([wb_ref[...]] * rep, axis=0)

    n_cmp = kc_ref.shape[0]
    n_lo = q0 // CMP_STRIDE - 1
    c_n = jnp.minimum(lax.broadcasted_iota(jnp.int32, (n_cmp, dh), 0) - n_lo, tq // CMP_STRIDE + 1)
    ktag = jnp.where(lax.broadcasted_iota(jnp.int32, (n_cmp, dh), 1) == c_n, 1.0, 0.0).astype(BF16)
    qa_cmp = jnp.concatenate([qs, jnp.concatenate([cq_ref[...]] * rep, axis=0)], axis=1)
    s = _dot_nt(qa_cmp, jnp.concatenate([kc_ref[...], ktag], axis=1))
    p_w = jnp.exp2(s_w - jnp.max(s_w, axis=-1, keepdims=True))
    ov_w = _dot(p_w.astype(BF16), vwp_ref[kr, :])
    o_win = ov_w[:, :dh] / ov_w[:, dh:]
    e = jnp.exp2(s - jnp.max(s, axis=-1, keepdims=True))
    vc_aug = jnp.concatenate([vc_ref[...], jnp.ones((n_cmp, dh), BF16)], axis=1)
    ov = _dot(e.astype(BF16), vc_aug)
    t_row = q0 + (lax.broadcasted_iota(jnp.int32, (rows, dh), 0) & (tq - 1))
    inv = jnp.where(t_row >= CMP_BLOCK - 1, 1.0 / jnp.maximum(ov[:, dh:], TINY), 0.0)
    o_cmp = ov[:, :dh] * inv
    p = e * jnp.concatenate([inv] * (n_cmp // dh), axis=1)
    p_sum = p[0:tq]
    for r in range(1, rep):
        p_sum = p_sum + p[r * tq:(r + 1) * tq]
    imp_t = _dot_nt(aggt_ref[...], p_sum.astype(BF16))

    jidx = lax.broadcasted_iota(jnp.int32, (n_sel, tq), 0)
    t_l = q0 + lax.broadcasted_iota(jnp.int32, (n_sel, tq), 1)
    cur = t_l // SLC_BLOCK
    forced = (jidx == 0) | (jidx == cur) | (jidx == cur - 1)
    causal = jidx * SLC_BLOCK <= t_l
    score = jnp.where(forced, FORCE_SCORE, jnp.where(causal, imp_t, -FORCE_SCORE))
    sc_ref[...] = score

    n_blk = (q0 + tq - 1) // SLC_BLOCK + 1
    n_it = (n_blk + RANK_UNROLL - 1) // RANK_UNROLL

    def rank_body(it, rank):
        for u in range(RANK_UNROLL):
            i = it * RANK_UNROLL + u
            row = sc_ref[pl.ds(i, 1), :]
            ge = jnp.where(row >= score, 1.0, 0.0)
            gt = jnp.where(row > score, 1.0, 0.0)
            rank = rank + jnp.where(jidx > i, ge, gt)
        return rank

    rank = lax.fori_loop(0, n_it, rank_body, jnp.zeros((n_sel, tq), F32))
    selneg_t = jnp.where(rank < n_topk, 0.0, NEG_INF)
    if n_sel < dh:
        selneg_t = jnp.concatenate([selneg_t, jnp.zeros((dh - n_sel, tq), F32)], axis=0)
    selneg = _dot_nt(eye_ref[...], selneg_t.astype(BF16)).astype(BF16)
    qa_sel = jnp.concatenate([qs, jnp.concatenate([selneg] * rep, axis=0)], axis=1)

    def qk_tile(kt):
        return _dot_nt(qa_sel, ksa_ref[pl.ds(pl.multiple_of(kt * tk, tk), tk), :])

    def consume(kt, buf, diagonal):
        k0 = pl.multiple_of(kt * tk, tk)
        s = buf[...]
        if diagonal:
            t_r = q0 + (lax.broadcasted_iota(jnp.int32, (rows, tk), 0) & (tq - 1))
            kpos = k0 + lax.broadcasted_iota(jnp.int32, (rows, tk), 1)
            s = jnp.where(kpos <= t_r, s, NEG_INF)
        m_prev = m_ref[...]
        m_new = jnp.maximum(m_prev, jnp.max(s, axis=-1, keepdims=True))
        p = jnp.exp2(s - jnp.concatenate([m_new] * (tk // 128), axis=1))
        alpha = jnp.exp2(m_prev - m_new)
        m_ref[...] = m_new
        acc_ref[...] = (jnp.concatenate([alpha, alpha], axis=1) * acc_ref[...]
                        + _dot(p.astype(BF16), vsa_ref[pl.ds(k0, tk), :]))

    m_ref[...] = jnp.full((rows, 128), NEG_INF, F32)
    acc_ref[...] = jnp.zeros((rows, 2 * dh), F32)
    s0_ref[...] = qk_tile(0)
    n_full = q0 // tk

    def sel_body(it, carry):
        kt = 2 * it
        s1_ref[...] = qk_tile(kt + 1)
        consume(kt, s0_ref, False)
        s0_ref[...] = qk_tile(kt + 2)
        consume(kt + 1, s1_ref, False)
        return carry

    lax.fori_loop(0, n_full // 2, sel_body, 0)

    @pl.when(n_full % 2 == 1)
    def _():
        s1_ref[...] = qk_tile(n_full)
        consume(n_full - 1, s0_ref, False)
        consume(n_full, s1_ref, True)

    @pl.when(n_full % 2 == 0)
    def _():
        consume(n_full, s0_ref, True)

    o_slc = acc_ref[:, 0:dh] / acc_ref[:, dh:2 * dh]

    gates = jax.nn.sigmoid(gl_ref[...])
    for r in range(rep):
        rs = slice(r * tq, (r + 1) * tq)
        o = (gates[:, 3 * r:3 * r + 1] * o_cmp[rs] + gates[:, 3 * r + 1:3 * r + 2] * o_slc[rs]
             + gates[:, 3 * r + 2:3 * r + 3] * o_win[rs])
        o_ref[:, r * dh:(r + 1) * dh] = o.astype(o_ref.dtype)


def _nsa_attention(p3, ps3, kc, vc, agg_t, eye, win_bias, cmp_qtag, q_gain, k_gain):
    b, seq, _ = p3.shape
    g, tq, dh = NSA_GROUPS, NSA_TQ, HEAD_DIM
    n_sel = seq // SLC_BLOCK
    assert n_sel <= dh and n_sel % 8 == 0 and seq % NSA_TK == 0
    nc = kc.shape[2]
    kvb = OFF_KV // dh

    def kv_spec(branch, is_v):
        base = kvb + (branch * 2 + is_v) * g
        return pl.BlockSpec((None, seq, dh), lambda i, j, k: (i, 0, base + j))

    qw = NSA_REP * dh
    return pl.pallas_call(
        functools.partial(_nsa_kernel, seq=seq),
        out_shape=jax.ShapeDtypeStruct((b, seq, NSA_Q_DIM), BF16),
        grid=(b, g, seq // tq),
        in_specs=[
            pl.BlockSpec((None, tq, qw), lambda i, j, k: (i, k, j)),
            pl.BlockSpec((None, tq, 128), lambda i, j, k: (i, k, j)),
            kv_spec(1, 0), kv_spec(1, 1), kv_spec(2, 0), kv_spec(2, 1),
            pl.BlockSpec((None, None, nc, dh), lambda i, j, k: (i, j, 0, 0)),
            pl.BlockSpec((None, None, nc, dh), lambda i, j, k: (i, j, 0, 0)),
            pl.BlockSpec((n_sel, nc), lambda i, j, k: (0, 0)),
            pl.BlockSpec((tq, tq), lambda i, j, k: (0, 0)),
            pl.BlockSpec((tq, WINDOW + tq), lambda i, j, k: (0, 0)),
            pl.BlockSpec((tq, dh), lambda i, j, k: (0, 0)),
            pl.BlockSpec((1, dh), lambda i, j, k: (0, 0)),
            pl.BlockSpec((N_BRANCH, dh), lambda i, j, k: (0, 0)),
        ],
        out_specs=pl.BlockSpec((None, tq, qw), lambda i, j, k: (i, k, j)),
        scratch_shapes=[
            pltpu.VMEM((seq, 2 * dh), BF16),
            pltpu.VMEM((seq + WINDOW, 2 * dh), BF16),
            pltpu.VMEM((seq, 2 * dh), BF16),
            pltpu.VMEM((seq + WINDOW, 2 * dh), BF16),
            pltpu.VMEM((n_sel, tq), F32),
            pltpu.VMEM((NSA_REP * tq, NSA_TK), F32),
            pltpu.VMEM((NSA_REP * tq, NSA_TK), F32),
            pltpu.VMEM((NSA_REP * tq, 128), F32),
            pltpu.VMEM((NSA_REP * tq, 2 * dh), F32),
        ],
        compiler_params=_cparams(("parallel", "parallel", "arbitrary")),
        name="nsa_attention",
    )(p3, ps3, p3, p3, p3, p3, kc, vc, agg_t, eye, win_bias, cmp_qtag, q_gain, k_gain)


GDN_TS = 256
GDN_INV = 128
HALO = 8


def _gdn_kernel(q_ref, k_ref, v_ref, z_ref, ab_ref, cw_ref, alog_ref, dtb_ref, og_ref, sh_ref,
                o_ref, xe_ref, st_ref):
    ts, hb, dh, ch = GDN_TS, GDN_HB, HEAD_DIM, GDN_CHUNK
    ti = pl.program_id(2)

    n_inv = ts // GDN_INV
    n_ch = ts // ch

    @pl.when(ti == 0)
    def _():
        xe_ref[...] = jnp.zeros_like(xe_ref)
        st_ref[...] = jnp.zeros_like(st_ref)

    row8 = lax.broadcasted_iota(jnp.int32, (HALO, hb * dh), 0)
    conv = []
    for part, ref in enumerate((q_ref, k_ref, v_ref)):
        xb = ref[...]
        xf = xb.astype(F32)
        prev = xe_ref[part]
        acc = cw_ref[part, GDN_CONV - 1:GDN_CONV, :] * xf
        corr = jnp.zeros((HALO, hb * dh), F32)
        for d in range(1, GDN_CONV):
            wd = cw_ref[part, GDN_CONV - 1 - d:GDN_CONV - d, :]
            acc = acc + wd * _dot(sh_ref[d - 1], xb)
            corr = corr + wd * jnp.where(row8 < d, pltpu.roll(prev, d, 0), 0.0)
        acc = jnp.concatenate([acc[0:HALO] + corr, acc[HALO:]], axis=0)
        conv.append(acc * jax.nn.sigmoid(acc))
        xe_ref[part] = xf[ts - HALO:ts]

    ab = ab_ref[...]
    xg = ab + dtb_ref[...]
    softplus = jnp.maximum(xg, 0.0) + jnp.log(1.0 + jnp.exp(-jnp.abs(xg)))
    g_cum = -jnp.exp(alog_ref[...]) * softplus
    beta = jax.nn.sigmoid(ab)
    row_in_chunk = lax.broadcasted_iota(jnp.int32, (ts, 128), 0) & (ch - 1)
    step = 1
    while step < ch:
        g_cum = g_cum + jnp.where(row_in_chunk >= step, pltpu.roll(g_cum, step, 0), 0.0)
        step *= 2
    g_end = jnp.concatenate(
        [jnp.broadcast_to(g_cum[(c + 1) * ch - 1:(c + 1) * ch, :], (ch, 128)) for c in range(n_ch)], axis=0)
    e_cum = jnp.exp(g_cum)
    e_rel = jnp.exp(g_end - g_cum)
    e_end = jnp.exp(g_end)
    g_cum_t = jnp.transpose(g_cum)

    ri = lax.broadcasted_iota(jnp.int32, (ts, ts), 0)
    ci = lax.broadcasted_iota(jnp.int32, (ts, ts), 1)
    same = (ri // ch) == (ci // ch)
    tril = same & (ri >= ci)
    strict = same & (ri > ci)
    ri_b = lax.broadcasted_iota(jnp.int32, (GDN_INV, GDN_INV), 0)
    ci_b = lax.broadcasted_iota(jnp.int32, (GDN_INV, GDN_INV), 1)
    ident = jnp.where(ri_b == ci_b, 1.0, 0.0)

    heads = []
    for hh in range(hb):
        cs = slice(hh * dh, (hh + 1) * dh)
        qh, kh, vh = conv[0][:, cs], conv[1][:, cs], conv[2][:, cs]
        qh = qh * lax.rsqrt(jnp.sum(qh * qh, axis=-1, keepdims=True) + EPS) * (dh ** -0.5)
        kh = kh * lax.rsqrt(jnp.sum(kh * kh, axis=-1, keepdims=True) + EPS)
        bt = beta[:, hb + hh:hb + hh + 1]
        diff = g_cum[:, hh:hh + 1] - g_cum_t[hh:hh + 1, :]
        e = jnp.exp(jnp.minimum(diff, 0.0))
        kb = kh * bt
        k_bf = kh.astype(BF16)
        a_mat = _dot_nt(kb.astype(BF16), k_bf) * jnp.where(strict, e, 0.0)
        qk = (_dot_nt(qh.astype(BF16), k_bf) * jnp.where(tril, e, 0.0)).astype(BF16)
        heads.append(dict(
            a=a_mat, qk=qk,
            vb=(vh * bt).astype(BF16),
            kbg=(kb * e_cum[:, hh:hh + 1]).astype(BF16),
            qg=qh * e_cum[:, hh:hh + 1],
        ))
        heads[-1]["kd"] = (kh * e_rel[:, hh:hh + 1]).astype(BF16)

    probs = []
    for hh in range(hb):
        for blk in range(n_inv):
            bs = slice(blk * GDN_INV, (blk + 1) * GDN_INV)
            pw = -heads[hh]["a"][bs, bs]
            probs.append([pw, ident + pw])
    for _ in range(5):
        for pr in probs:
            pw_bf = pr[0].astype(BF16)
            pr[0] = _dot(pw_bf, pw_bf)
        for pr in probs:
            pr[1] = pr[1] + _dot(pr[1].astype(BF16), pr[0].astype(BF16))
    for hh in range(hb):
        u_parts, w_parts = [], []
        for blk in range(n_inv):
            bs = slice(blk * GDN_INV, (blk + 1) * GDN_INV)
            t_bf = probs[hh * n_inv + blk][1].astype(BF16)
            u_parts.append(_dot(t_bf, heads[hh]["vb"][bs]))
            w_parts.append(_dot(t_bf, heads[hh]["kbg"][bs]))
        heads[hh]["u"] = jnp.concatenate(u_parts, axis=0).astype(BF16)
        heads[hh]["w"] = jnp.concatenate(w_parts, axis=0).astype(BF16)
    for hd in heads:
        hd["qkw"] = _dot(hd["qk"], hd["w"])
        hd["qku"] = _dot(hd["qk"], hd["u"])
        hd["mw"] = [_dot_tn(hd["kd"][c * ch:(c + 1) * ch],
                            jnp.concatenate([hd["w"][c * ch:(c + 1) * ch], hd["u"][c * ch:(c + 1) * ch]], axis=1))
                    for c in range(n_ch)]
    for hd in heads:
        hd["qp"] = (hd["qg"] - hd["qkw"]).astype(BF16)
        hd["mc"] = [mw[:, :dh].astype(BF16) for mw in hd["mw"]]
        hd["bc"] = [mw[:, dh:] for mw in hd["mw"]]

    states = [st_ref[hh] for hh in range(hb)]
    outs = [[] for _ in range(hb)]
    for c in range(n_ch):
        rs = slice(c * ch, (c + 1) * ch)
        for hh in range(hb):
            hd = heads[hh]
            r = _dot(jnp.concatenate([hd["qp"][rs], hd["mc"][c]], axis=0), states[hh].astype(BF16))
            outs[hh].append(r[:ch] + hd["qku"][rs])
            states[hh] = states[hh] * e_end[c * ch:c * ch + 1, hh:hh + 1] - r[ch:] + hd["bc"][c]

    for hh in range(hb):
        cs = slice(hh * dh, (hh + 1) * dh)
        st_ref[hh] = states[hh]
        o = jnp.concatenate(outs[hh], axis=0)
        o = o * lax.rsqrt(jnp.mean(o * o, axis=-1, keepdims=True) + EPS) * og_ref[...]
        z = z_ref[:, cs].astype(F32)
        o_ref[:, cs] = (o * (z * jax.nn.sigmoid(z))).astype(o_ref.dtype)


def _gdn(p3, ps3, conv_w3, alog_l, dtb_l, out_gain, shifts):
    b, seq, _ = p3.shape
    ts, hb, dh = GDN_TS, GDN_HB, HEAD_DIM
    wb = hb * dh
    qb, zb = OFF_GQKV // wb, OFF_Z // wb
    gw = GDN_DIM // wb
    return pl.pallas_call(
        _gdn_kernel,
        out_shape=jax.ShapeDtypeStruct((b, seq, GDN_DIM), BF16),
        grid=(b, GDN_NHG, seq // ts),
        in_specs=[
            pl.BlockSpec((None, ts, wb), lambda i, j, k: (i, k, qb + j)),
            pl.BlockSpec((None, ts, wb), lambda i, j, k: (i, k, qb + gw + j)),
            pl.BlockSpec((None, ts, wb), lambda i, j, k: (i, k, qb + 2 * gw + j)),
            pl.BlockSpec((None, ts, wb), lambda i, j, k: (i, k, zb + j)),
            pl.BlockSpec((None, ts, 128), lambda i, j, k: (i, k, NSA_GROUPS + j)),
            pl.BlockSpec((3, GDN_CONV, wb), lambda i, j, k: (0, 0, j)),
            pl.BlockSpec((None, 1, 128), lambda i, j, k: (j, 0, 0)),
            pl.BlockSpec((None, 1, 128), lambda i, j, k: (j, 0, 0)),
            pl.BlockSpec((1, dh), lambda i, j, k: (0, 0)),
            pl.BlockSpec((GDN_CONV - 1, ts, ts), lambda i, j, k: (0, 0, 0)),
        ],
        out_specs=pl.BlockSpec((None, ts, wb), lambda i, j, k: (i, k, j)),
        scratch_shapes=[
            pltpu.VMEM((3, HALO, wb), F32),
            pltpu.VMEM((hb, dh, dh), F32),
        ],
        compiler_params=_cparams(("parallel", "parallel", "arbitrary")),
        name="gdn",
    )(p3, p3, p3, p3, ps3, conv_w3, alog_l, dtb_l, out_gain, shifts)


def _merge_kernel(oa_ref, ob_ref, wa_ref, wb_ref, ma_ref, mb_ref, o_ref):
    ya = _dot(oa_ref[...], wa_ref[...])
    yb = _dot(ob_ref[...], wb_ref[...])
    mix = jax.nn.sigmoid(ma_ref[...].astype(F32)) * ya + jax.nn.sigmoid(mb_ref[...].astype(F32)) * yb
    o_ref[...] = mix.astype(o_ref.dtype)


def _merge(oa, ob, wa, wb, p2, bm=512, bn=1024):
    t, d = oa.shape
    n = wa.shape[1]
    ma_b, mb_b = OFF_MA // bn, OFF_MB // bn
    return pl.pallas_call(
        _merge_kernel,
        out_shape=jax.ShapeDtypeStruct((t, n), BF16),
        grid=(n // bn, t // bm),
        in_specs=[
            pl.BlockSpec((bm, d), lambda j, i: (i, 0)),
            pl.BlockSpec((bm, d), lambda j, i: (i, 0)),
            pl.BlockSpec((d, bn), lambda j, i: (0, j)),
            pl.BlockSpec((d, bn), lambda j, i: (0, j)),
            pl.BlockSpec((bm, bn), lambda j, i: (i, ma_b + j)),
            pl.BlockSpec((bm, bn), lambda j, i: (i, mb_b + j)),
        ],
        out_specs=pl.BlockSpec((bm, bn), lambda j, i: (i, j)),
        compiler_params=_cparams(("parallel", "parallel")),
        name="merge",
    )(oa, ob, wa, wb, p2, p2)


def _resid_matmul_kernel(a_ref, w_ref, r_ref, o_ref):
    o_ref[...] = r_ref[...] + _dot(a_ref[...], w_ref[...])


def _resid_matmul(a, w, resid, bm, bn):
    t, k = a.shape
    n = w.shape[1]
    return pl.pallas_call(
        _resid_matmul_kernel,
        out_shape=jax.ShapeDtypeStruct((t, n), F32),
        grid=(n // bn, t // bm),
        in_specs=[
            pl.BlockSpec((bm, k), lambda j, i: (i, 0)),
            pl.BlockSpec((k, bn), lambda j, i: (0, j)),
            pl.BlockSpec((bm, bn), lambda j, i: (i, j)),
        ],
        out_specs=pl.BlockSpec((bm, bn), lambda j, i: (i, j)),
        compiler_params=_cparams(("parallel", "parallel")),
        name="resid_matmul",
    )(a, w, resid)


def _ffn_up_kernel(x_ref, g_ref, wg_ref, wu_ref, o_ref, hn_ref):
    @pl.when(pl.program_id(1) == 0)
    def _():
        x = x_ref[...]
        hn_ref[...] = (x * lax.rsqrt(jnp.mean(x * x, axis=-1, keepdims=True) + EPS) * g_ref[...]).astype(BF16)

    hn = hn_ref[...]
    gate = _dot(hn, wg_ref[...].astype(BF16))
    up = _dot(hn, wu_ref[...].astype(BF16))
    o_ref[...] = (gate * jax.nn.sigmoid(gate) * up).astype(o_ref.dtype)


def _ffn_up(x1, gain, wg, wu, bm=1024, bn=512):
    t, d = x1.shape
    n = wg.shape[1]
    return pl.pallas_call(
        _ffn_up_kernel,
        out_shape=jax.ShapeDtypeStruct((t, n), BF16),
        grid=(t // bm, n // bn),
        in_specs=[
            pl.BlockSpec((bm, d), lambda i, j: (i, 0)),
            pl.BlockSpec((1, d), lambda i, j: (0, 0)),
            pl.BlockSpec((d, bn), lambda i, j: (0, j)),
            pl.BlockSpec((d, bn), lambda i, j: (0, j)),
        ],
        out_specs=pl.BlockSpec((bm, bn), lambda i, j: (i, j)),
        scratch_shapes=[pltpu.VMEM((bm, d), BF16)],
        compiler_params=_cparams(("parallel", "arbitrary")),
        name="ffn_up",
    )(x1, gain, wg, wu)


def _pack_in_weights(w):
    w_t = jnp.transpose(w).astype(BF16)
    off_gate = OFF_GQKV
    off_a = OFF_MA + SKIP_GATE
    gate_t = w_t[off_gate:off_gate + SKIP_GATE]
    a_t = w_t[off_a:off_a + GDN_HEADS]
    b_t = w_t[off_a + GDN_HEADS:off_a + SKIP_AB]
    small = []
    per_group = NSA_REP * N_BRANCH
    for g in range(NSA_GROUPS):
        small.append(jnp.pad(gate_t[g * per_group:(g + 1) * per_group], ((0, 128 - per_group), (0, 0))))
    for hg in range(GDN_NHG):
        hs = slice(hg * GDN_HB, (hg + 1) * GDN_HB)
        small.append(jnp.pad(jnp.concatenate([a_t[hs], b_t[hs]], axis=0), ((0, 128 - 2 * GDN_HB), (0, 0))))
    w_small_t = jnp.concatenate(small, axis=0)
    assert w_t.shape[0] == NP_BIG + SKIP_GATE + SKIP_AB and w_small_t.shape[0] == NP_SMALL
    return w_t, w_small_t


def _lane_rows(v):
    return jnp.pad(v.reshape(GDN_NHG, 1, GDN_HB).astype(F32), ((0, 0), (0, 0), (0, 128 - GDN_HB)))


def _nsa_constants(seq):
    n_cmp = (seq - CMP_BLOCK) // CMP_STRIDE + 1
    nc = seq // CMP_STRIDE
    n_sel = seq // SLC_BLOCK
    cmp_start = np.arange(n_cmp) * CMP_STRIDE
    sel_start = np.arange(n_sel) * SLC_BLOCK
    overlap = (np.minimum(cmp_start[:, None] + CMP_BLOCK, sel_start[None, :] + SLC_BLOCK)
               - np.maximum(cmp_start[:, None], sel_start[None, :]))
    agg = np.zeros((nc, n_sel), np.float32)
    agg[:n_cmp] = np.clip(overlap, 0, None) / CMP_BLOCK
    eye = np.eye(NSA_TQ, dtype=np.float32)
    i = np.arange(NSA_TQ)[:, None]
    c = np.arange(WINDOW + NSA_TQ)[None, :]
    win_bias = np.where((c > i) & (c <= i + WINDOW), 0.0, NEG_INF).astype(np.float32)
    lane = np.arange(HEAD_DIM)[None, :]
    n_part = NSA_TQ // CMP_STRIDE + 1
    hidden = ((lane < n_part) & (CMP_STRIDE * lane + CMP_STRIDE - 1 > i)) | (lane == n_part)
    cmp_qtag = np.where(hidden, NEG_INF, 0.0).astype(np.float32)
    return jnp.asarray(agg.T, BF16), jnp.asarray(eye, BF16), jnp.asarray(win_bias), jnp.asarray(cmp_qtag, BF16)


def _gdn_constants():
    idx = np.arange(GDN_TS)
    shifts = np.stack([(idx[None, :] == idx[:, None] - d) for d in range(1, GDN_CONV)]).astype(np.float32)
    return jnp.asarray(shifts, BF16)


def _layer(x, attn_norm, w_in, nsa_q_norm, nsa_k_norm, cmp_pos, w_cmp, gdn_conv, gdn_a_log, gdn_dt_bias,
           gdn_out_norm, w_branch_a, w_branch_b, w_out, ffn_norm, w_gate, w_up, w_down):
    b, seq, d = x.shape
    t = b * seq
    x2 = x.reshape(t, d)

    w_t, w_small_t = _pack_in_weights(w_in)
    p2, ps2 = _in_proj(x2, attn_norm.reshape(1, d), w_t, w_small_t)
    p3 = p2.reshape(b, seq, NP_BIG)
    ps3 = ps2.reshape(b, seq, NP_SMALL)

    w_cmp2 = w_cmp.reshape(2, CMP_BLOCK * HEAD_DIM, HEAD_DIM).astype(BF16)
    pos2 = jnp.broadcast_to(cmp_pos.reshape(2, 1, CMP_BLOCK * HEAD_DIM), (2, 8, CMP_BLOCK * HEAD_DIM)).astype(BF16)
    kc, vc = _nsa_compress(p3, w_cmp2, pos2, nsa_k_norm[0:1])

    agg_t, eye_q, win_bias, cmp_qtag = _nsa_constants(seq)
    o_a = _nsa_attention(p3, ps3, kc, vc, agg_t, eye_q, win_bias, cmp_qtag, nsa_q_norm.reshape(1, HEAD_DIM),
                         nsa_k_norm)

    conv_w3 = gdn_conv.reshape(GDN_CONV, 3, GDN_DIM).transpose(1, 0, 2)
    o_b = _gdn(p3, ps3, conv_w3, _lane_rows(gdn_a_log), _lane_rows(gdn_dt_bias),
               gdn_out_norm.reshape(1, HEAD_DIM), _gdn_constants())

    mix = _merge(o_a.reshape(t, NSA_Q_DIM), o_b.reshape(t, GDN_DIM), w_branch_a.astype(BF16),
                 w_branch_b.astype(BF16), p2)
    x1 = _resid_matmul(mix, w_out.astype(BF16), x2, bm=1024, bn=1024)

    act = _ffn_up(x1, ffn_norm.reshape(1, d), w_gate, w_up)
    out = _resid_matmul(act, w_down.astype(BF16), x1, bm=512, bn=1024)
    return out.reshape(b, seq, d)


def kernel(x, attn_norm, w_in, nsa_q_norm, nsa_k_norm, cmp_pos, w_cmp, gdn_conv, gdn_a_log, gdn_dt_bias,
           gdn_out_norm, w_branch_a, w_branch_b, w_out, ffn_norm, w_gate, w_up, w_down):
    for l in range(attn_norm.shape[0]):
        x = _layer(x, attn_norm[l], w_in[l], nsa_q_norm[l], nsa_k_norm[l], cmp_pos[l], w_cmp[l], gdn_conv[l],
                   gdn_a_log[l], gdn_dt_bias[l], gdn_out_norm[l], w_branch_a[l], w_branch_b[l], w_out[l],
                   ffn_norm[l], w_gate[l], w_up[l], w_down[l])
    return x
```

```python
import functools

import numpy as np
import jax
import jax.numpy as jnp
from jax import lax
from jax.experimental import pallas as pl
from jax.experimental.pallas import tpu as pltpu

F32 = jnp.float32
BF16 = jnp.bfloat16

D_MODEL = 2048
EPS = 1e-6
NEG_INF = -1e30
TINY = 1e-30
FORCE_SCORE = 1e4
LOG2E = 1.4426950408889634

NSA_HEADS = 16
NSA_GROUPS = 4
NSA_REP = NSA_HEADS // NSA_GROUPS
HEAD_DIM = 128
N_BRANCH = 3
CMP_BLOCK = 32
CMP_STRIDE = 16
SLC_BLOCK = 64
SLC_TOPK = 16
WINDOW = 512

GDN_HEADS = 16
GDN_CONV = 4
GDN_CHUNK = 64

D_FF = 5632

NSA_Q_DIM = NSA_HEADS * HEAD_DIM
NSA_KV_DIM = N_BRANCH * 2 * NSA_GROUPS * HEAD_DIM
GDN_DIM = GDN_HEADS * HEAD_DIM
OFF_Q = 0
OFF_KV = OFF_Q + NSA_Q_DIM
OFF_GQKV = OFF_KV + NSA_KV_DIM
OFF_Z = OFF_GQKV + 3 * GDN_DIM
OFF_MA = OFF_Z + GDN_DIM
OFF_MB = OFF_MA + D_MODEL
NP_BIG = OFF_MB + D_MODEL

GDN_HB = 8
GDN_NHG = GDN_HEADS // GDN_HB
NP_SMALL = (NSA_GROUPS + GDN_NHG) * 128

VMEM_LIMIT = 56 * 1024 * 1024

NT_DIMS = (((1,), (1,)), ((), ()))


def _dot(a, b):
    return jnp.dot(a, b, preferred_element_type=F32)


def _dot_nt(a, b):
    return lax.dot_general(a, b, NT_DIMS, preferred_element_type=F32)


def _dot_tn(a, b):
    return lax.dot_general(a, b, (((0,), (0,)), ((), ())), preferred_element_type=F32)


def _cparams(sem):
    return pltpu.CompilerParams(dimension_semantics=sem, vmem_limit_bytes=VMEM_LIMIT)


def _in_proj_kernel(x_ref, g_ref, wt_ref, wst_ref, o_ref, os_ref, hn_ref):
    @pl.when(pl.program_id(1) == 0)
    def _():
        x = x_ref[...]
        y = x * lax.rsqrt(jnp.mean(x * x, axis=-1, keepdims=True) + EPS) * g_ref[...]
        hn = y.astype(BF16)
        hn_ref[...] = hn
        os_ref[...] = _dot_nt(hn, wst_ref[...])

    o_ref[...] = _dot_nt(hn_ref[...], wt_ref[...]).astype(o_ref.dtype)


SKIP_GATE = NSA_HEADS * N_BRANCH
SKIP_AB = 2 * GDN_HEADS


def _in_proj(x2, gain, w_t, w_small_t, bm=1024, bn=1024):
    t, d = x2.shape
    n = NP_BIG
    ns = w_small_t.shape[0]
    assert OFF_GQKV % bn == 0 and OFF_MA % bn == 0 and n % bn == 0

    def w_row(i, j):
        skip = jnp.where(j >= OFF_MA // bn, SKIP_GATE + SKIP_AB, jnp.where(j >= OFF_GQKV // bn, SKIP_GATE, 0))
        return pl.multiple_of(j * bn + skip, 16)

    return pl.pallas_call(
        _in_proj_kernel,
        out_shape=(jax.ShapeDtypeStruct((t, n), BF16), jax.ShapeDtypeStruct((t, ns), F32)),
        grid=(t // bm, n // bn),
        in_specs=[
            pl.BlockSpec((bm, d), lambda i, j: (i, 0)),
            pl.BlockSpec((1, d), lambda i, j: (0, 0)),
            pl.BlockSpec((pl.Element(bn), pl.Element(d)), lambda i, j: (w_row(i, j), 0)),
            pl.BlockSpec((ns, d), lambda i, j: (0, 0)),
        ],
        out_specs=(
            pl.BlockSpec((bm, bn), lambda i, j: (i, j)),
            pl.BlockSpec((bm, ns), lambda i, j: (i, 0)),
        ),
        scratch_shapes=[pltpu.VMEM((bm, d), BF16)],
        compiler_params=_cparams(("parallel", "arbitrary")),
        name="in_proj",
    )(x2, gain, w_t, w_small_t)


def _cmp_kernel(ck_ref, cv_ref, w_ref, pos_ref, kg_ref, kc_ref, vc_ref, xs_ref):
    dh = HEAD_DIM
    n = kc_ref.shape[0]

    def compress(x_ref, idx):
        xs_ref[...] = x_ref[...].astype(F32)
        lo = hi = None
        for l in range(CMP_STRIDE):
            xl = xs_ref[pl.ds(l, n, stride=CMP_STRIDE), :].astype(BF16)
            lo_l = _dot(xl, w_ref[idx, l * dh:(l + 1) * dh, :])
            hi_l = _dot(xl, w_ref[idx, (CMP_STRIDE + l) * dh:(CMP_STRIDE + l + 1) * dh, :])
            lo = lo_l if lo is None else lo + lo_l
            hi = hi_l if hi is None else hi + hi_l
        bias = _dot(pos_ref[idx], w_ref[idx])[0:1]
        return lo + pltpu.roll(hi, n - 1, 0) + bias

    kc = compress(ck_ref, 0)
    kc = kc * lax.rsqrt(jnp.mean(kc * kc, axis=-1, keepdims=True) + EPS) * kg_ref[...]
    kc_ref[...] = kc.astype(BF16)
    vc_ref[...] = compress(cv_ref, 1).astype(BF16)


def _nsa_compress(p3, w_cmp2, pos2, k_gain0):
    b, seq, _ = p3.shape
    g, dh = NSA_GROUPS, HEAD_DIM
    nc = seq // CMP_STRIDE
    width = CMP_BLOCK * dh
    kvb = OFF_KV // dh
    out = jax.ShapeDtypeStruct((b, g, nc, dh), BF16)
    return pl.pallas_call(
        _cmp_kernel,
        out_shape=(out, out),
        grid=(b, g),
        in_specs=[
            pl.BlockSpec((None, seq, dh), lambda i, j: (i, 0, kvb + j)),
            pl.BlockSpec((None, seq, dh), lambda i, j: (i, 0, kvb + g + j)),
            pl.BlockSpec((2, width, dh), lambda i, j: (0, 0, 0)),
            pl.BlockSpec((2, 8, width), lambda i, j: (0, 0, 0)),
            pl.BlockSpec((1, dh), lambda i, j: (0, 0)),
        ],
        out_specs=(
            pl.BlockSpec((None, None, nc, dh), lambda i, j: (i, j, 0, 0)),
            pl.BlockSpec((None, None, nc, dh), lambda i, j: (i, j, 0, 0)),
        ),
        scratch_shapes=[pltpu.VMEM((seq, dh), F32)],
        compiler_params=_cparams(("parallel", "parallel")),
        name="nsa_compress",
    )(p3, p3, w_cmp2, pos2, k_gain0)


NSA_TQ = 256
NSA_TK = 512
NORM_ROWS = 512
RANK_UNROLL = 4


def _rms(xf, gain):
    return xf * lax.rsqrt(jnp.mean(xf * xf, axis=-1, keepdims=True) + EPS) * gain


def _nsa_kernel(q_ref, gl_ref, ks_ref, vs_ref, kw_ref, vw_ref, kc_ref, vc_ref, aggt_ref, eye_ref, wb_ref, cq_ref,
                e0_ref, qg_ref, kg_ref, o_ref, ksa_ref, kwa_ref, vsa_ref, vwp_ref, sc_ref, s0_ref, s1_ref, m_ref, acc_ref,
                *, seq):
    tq, tk, rep, dh = NSA_TQ, NSA_TK, NSA_REP, HEAD_DIM
    rows = rep * tq
    n_sel = seq // SLC_BLOCK
    n_topk = min(SLC_TOPK, n_sel)
    qi = pl.program_id(2)
    q0 = qi * tq

    @pl.when(qi == 0)
    def _():
        kwa_ref[0:WINDOW, 0:dh] = jnp.zeros((WINDOW, dh), BF16)
        kwa_ref[0:WINDOW, dh:2 * dh] = jnp.where(
            lax.broadcasted_iota(jnp.int32, (WINDOW, dh), 1) == 0, 1.0, 0.0).astype(BF16)
        ones = jnp.ones((NORM_ROWS, dh), BF16)
        vwp_ref[0:WINDOW, 0:dh] = jnp.zeros((WINDOW, dh), BF16)
        vwp_ref[0:WINDOW, dh:2 * dh] = jnp.ones((WINDOW, dh), BF16)
        lane = lax.broadcasted_iota(jnp.int32, (NORM_ROWS, dh), 1)
        sub = lax.broadcasted_iota(jnp.int32, (NORM_ROWS, dh), 0)

        def body(c, carry):
            r0 = pl.multiple_of(c * NORM_ROWS, NORM_ROWS)
            r = pl.ds(r0, NORM_ROWS)
            rw = pl.ds(r0 + WINDOW, NORM_ROWS)
            ksa_ref[r, 0:dh] = _rms(ks_ref[r, :].astype(F32), kg_ref[1:2, :]).astype(BF16)
            ksa_ref[r, dh:2 * dh] = jnp.where((r0 + sub) // SLC_BLOCK == lane, 1.0, 0.0).astype(BF16)
            kwa_ref[rw, 0:dh] = _rms(kw_ref[r, :].astype(F32), kg_ref[2:3, :]).astype(BF16)
            kwa_ref[rw, dh:2 * dh] = jnp.zeros((NORM_ROWS, dh), BF16)
            vsa_ref[r, 0:dh] = vs_ref[r, :]
            vsa_ref[r, dh:2 * dh] = ones
            vwp_ref[rw, 0:dh] = vw_ref[r, :]
            vwp_ref[rw, dh:2 * dh] = ones
            return carry
        lax.fori_loop(0, seq // NORM_ROWS, body, 0)

    qscale = dh ** -0.5 * LOG2E
    q = q_ref[...].astype(F32)
    qs = jnp.concatenate(
        [(_rms(q[:, r * dh:(r + 1) * dh], qg_ref[...]) * qscale).astype(BF16) for r in range(rep)], axis=0)

    s0_ref[...] = _dot_nt(qs, ksa_ref[0:tk, 0:dh])

    wk = WINDOW + tq
    padneg = jnp.where(lax.broadcasted_iota(jnp.int32, (rows, dh), 1) == 0, NEG_INF, 0.0).astype(BF16)
    qa_win = jnp.concatenate([qs, padneg], axis=1)
    kr = pl.ds(pl.multiple_of(q0, tq), wk)
    s_w = _dot_nt(qa_win, kwa_ref[kr, :]) + jnp.concatenate([wb_ref[...]] * rep, axis=0)

    n_cmp = kc_ref.shape[0]
    n_lo = q0 // CMP_STRIDE - 1
    c_n = jnp.minimum(lax.broadcasted_iota(jnp.int32, (n_cmp, dh), 0) - n_lo, tq // CMP_STRIDE + 1)
    ktag = jnp.where(lax.broadcasted_iota(jnp.int32, (n_cmp, dh), 1) == c_n, 1.0, 0.0).astype(BF16)
    qa_cmp = jnp.concatenate([qs, jnp.concatenate([cq_ref[...]] * rep, axis=0)], axis=1)
    s = _dot_nt(qa_cmp, jnp.concatenate([kc_ref[...], ktag], axis=1))
    p_w = jnp.exp2(s_w - jnp.max(s_w, axis=-1, keepdims=True))
    ov_w = _dot(p_w.astype(BF16), vwp_ref[kr, :])
    o_win = ov_w[:, :dh] / ov_w[:, dh:]
    e = jnp.exp2(s - jnp.max(s, axis=-1, keepdims=True))
    vc_aug = jnp.concatenate([vc_ref[...], jnp.ones((n_cmp, dh), BF16)], axis=1)
    ov = _dot(e.astype(BF16), vc_aug)
    t_row = q0 + (lax.broadcasted_iota(jnp.int32, (rows, dh), 0) & (tq - 1))
    inv = jnp.where(t_row >= CMP_BLOCK - 1, 1.0 / jnp.maximum(ov[:, dh:], TINY), 0.0)
    o_cmp = ov[:, :dh] * inv
    p = e * jnp.concatenate([inv] * (n_cmp // dh), axis=1)
    p_sum = p[0:tq]
    for r in range(1, rep):
        p_sum = p_sum + p[r * tq:(r + 1) * tq]
    imp_t = _dot_nt(aggt_ref[...], p_sum.astype(BF16))

    jidx = lax.broadcasted_iota(jnp.int32, (n_sel, tq), 0)
    t_l = q0 + lax.broadcasted_iota(jnp.int32, (n_sel, tq), 1)
    cur = t_l // SLC_BLOCK
    forced = (jidx == 0) | (jidx == cur) | (jidx == cur - 1)
    causal = jidx * SLC_BLOCK <= t_l
    score = jnp.where(forced, FORCE_SCORE, jnp.where(causal, imp_t, -FORCE_SCORE))
    sc_ref[...] = score

    n_blk = (q0 + tq - 1) // SLC_BLOCK + 1
    n_it = (n_blk + RANK_UNROLL - 1) // RANK_UNROLL

    def rank_body(it, rank):
        for u in range(RANK_UNROLL):
            i = it * RANK_UNROLL + u
            row = sc_ref[pl.ds(i, 1), :]
            ge = jnp.where(row >= score, 1.0, 0.0)
            gt = jnp.where(row > score, 1.0, 0.0)
            rank = rank + jnp.where(jidx > i, ge, gt)
        return rank

    rank = lax.fori_loop(0, n_it, rank_body, jnp.zeros((n_sel, tq), F32))
    selneg_t = jnp.where(rank < n_topk, 0.0, NEG_INF)
    if n_sel < dh:
        selneg_t = jnp.concatenate([selneg_t, jnp.zeros((dh - n_sel, tq), F32)], axis=0)
    selneg = _dot_nt(eye_ref[...], selneg_t.astype(BF16)).astype(BF16)
    qa_sel = jnp.concatenate([qs, jnp.concatenate([selneg] * rep, axis=0)], axis=1)

    def qk_tile(kt):
        return _dot_nt(qa_sel, ksa_ref[pl.ds(pl.multiple_of(kt * tk, tk), tk), :])

    def consume(kt, buf, diagonal):
        k0 = pl.multiple_of(kt * tk, tk)
        s = buf[...]
        if diagonal:
            t_r = q0 + (lax.broadcasted_iota(jnp.int32, (rows, tk), 0) & (tq - 1))
            kpos = k0 + lax.broadcasted_iota(jnp.int32, (rows, tk), 1)
            s = jnp.where(kpos <= t_r, s, NEG_INF)
        m_prev = m_ref[...]
        m_new = jnp.maximum(m_prev, jnp.max(s, axis=-1, keepdims=True))
        p = jnp.exp2(s - jnp.concatenate([m_new] * (tk // 128), axis=1))
        alpha = jnp.exp2(m_prev - m_new)
        m_ref[...] = m_new
        acc_ref[...] = (jnp.concatenate([alpha, alpha], axis=1) * acc_ref[...]
                        + _dot(p.astype(BF16), vsa_ref[pl.ds(k0, tk), :]))

    m_ref[...] = jnp.full((rows, 128), NEG_INF, F32)
    acc_ref[...] = jnp.zeros((rows, 2 * dh), F32)
    s0_ref[...] = s0_ref[...] + jnp.concatenate([_dot(selneg, e0_ref[...])] * rep, axis=0)
    n_full = q0 // tk

    def sel_body(it, carry):
        kt = 2 * it
        s1_ref[...] = qk_tile(kt + 1)
        consume(kt, s0_ref, False)
        s0_ref[...] = qk_tile(kt + 2)
        consume(kt + 1, s1_ref, False)
        return carry

    lax.fori_loop(0, n_full // 2, sel_body, 0)

    @pl.when(n_full % 2 == 1)
    def _():
        s1_ref[...] = qk_tile(n_full)
        consume(n_full - 1, s0_ref, False)
        consume(n_full, s1_ref, True)

    @pl.when(n_full % 2 == 0)
    def _():
        consume(n_full, s0_ref, True)

    o_slc = acc_ref[:, 0:dh] / acc_ref[:, dh:2 * dh]

    gates = jax.nn.sigmoid(gl_ref[...])
    for r in range(rep):
        rs = slice(r * tq, (r + 1) * tq)
        o = (gates[:, 3 * r:3 * r + 1] * o_cmp[rs] + gates[:, 3 * r + 1:3 * r + 2] * o_slc[rs]
             + gates[:, 3 * r + 2:3 * r + 3] * o_win[rs])
        o_ref[:, r * dh:(r + 1) * dh] = o.astype(o_ref.dtype)


def _nsa_attention(p3, ps3, kc, vc, agg_t, eye, win_bias, cmp_qtag, expand0, q_gain, k_gain):
    b, seq, _ = p3.shape
    g, tq, dh = NSA_GROUPS, NSA_TQ, HEAD_DIM
    n_sel = seq // SLC_BLOCK
    assert n_sel <= dh and n_sel % 8 == 0 and seq % NSA_TK == 0
    nc = kc.shape[2]
    kvb = OFF_KV // dh

    def kv_spec(branch, is_v):
        base = kvb + (branch * 2 + is_v) * g
        return pl.BlockSpec((None, seq, dh), lambda i, j, k: (i, 0, base + j))

    qw = NSA_REP * dh
    return pl.pallas_call(
        functools.partial(_nsa_kernel, seq=seq),
        out_shape=jax.ShapeDtypeStruct((b, seq, NSA_Q_DIM), BF16),
        grid=(b, g, seq // tq),
        in_specs=[
            pl.BlockSpec((None, tq, qw), lambda i, j, k: (i, k, j)),
            pl.BlockSpec((None, tq, 128), lambda i, j, k: (i, k, j)),
            kv_spec(1, 0), kv_spec(1, 1), kv_spec(2, 0), kv_spec(2, 1),
            pl.BlockSpec((None, None, nc, dh), lambda i, j, k: (i, j, 0, 0)),
            pl.BlockSpec((None, None, nc, dh), lambda i, j, k: (i, j, 0, 0)),
            pl.BlockSpec((n_sel, nc), lambda i, j, k: (0, 0)),
            pl.BlockSpec((tq, tq), lambda i, j, k: (0, 0)),
            pl.BlockSpec((tq, WINDOW + tq), lambda i, j, k: (0, 0)),
            pl.BlockSpec((tq, dh), lambda i, j, k: (0, 0)),
            pl.BlockSpec((dh, NSA_TK), lambda i, j, k: (0, 0)),
            pl.BlockSpec((1, dh), lambda i, j, k: (0, 0)),
            pl.BlockSpec((N_BRANCH, dh), lambda i, j, k: (0, 0)),
        ],
        out_specs=pl.BlockSpec((None, tq, qw), lambda i, j, k: (i, k, j)),
        scratch_shapes=[
            pltpu.VMEM((seq, 2 * dh), BF16),
            pltpu.VMEM((seq + WINDOW, 2 * dh), BF16),
            pltpu.VMEM((seq, 2 * dh), BF16),
            pltpu.VMEM((seq + WINDOW, 2 * dh), BF16),
            pltpu.VMEM((n_sel, tq), F32),
            pltpu.VMEM((NSA_REP * tq, NSA_TK), F32),
            pltpu.VMEM((NSA_REP * tq, NSA_TK), F32),
            pltpu.VMEM((NSA_REP * tq, 128), F32),
            pltpu.VMEM((NSA_REP * tq, 2 * dh), F32),
        ],
        compiler_params=_cparams(("parallel", "parallel", "arbitrary")),
        name="nsa_attention",
    )(p3, ps3, p3, p3, p3, p3, kc, vc, agg_t, eye, win_bias, cmp_qtag, expand0, q_gain, k_gain)


GDN_TS = 256
GDN_INV = 128
HALO = 8


def _gdn_kernel(q_ref, k_ref, v_ref, z_ref, ab_ref, cw_ref, alog_ref, dtb_ref, og_ref, sh_ref,
                o_ref, xe_ref, st_ref):
    ts, hb, dh, ch = GDN_TS, GDN_HB, HEAD_DIM, GDN_CHUNK
    ti = pl.program_id(2)

    n_inv = ts // GDN_INV
    n_ch = ts // ch

    @pl.when(ti == 0)
    def _():
        xe_ref[...] = jnp.zeros_like(xe_ref)
        st_ref[...] = jnp.zeros_like(st_ref)

    row8 = lax.broadcasted_iota(jnp.int32, (HALO, hb * dh), 0)
    conv = []
    for part, ref in enumerate((q_ref, k_ref, v_ref)):
        xb = ref[...]
        xf = xb.astype(F32)
        prev = xe_ref[part]
        acc = cw_ref[part, GDN_CONV - 1:GDN_CONV, :] * xf
        corr = jnp.zeros((HALO, hb * dh), F32)
        for d in range(1, GDN_CONV):
            wd = cw_ref[part, GDN_CONV - 1 - d:GDN_CONV - d, :]
            acc = acc + wd * _dot(sh_ref[d - 1], xb)
            corr = corr + wd * jnp.where(row8 < d, pltpu.roll(prev, d, 0), 0.0)
        acc = jnp.concatenate([acc[0:HALO] + corr, acc[HALO:]], axis=0)
        conv.append(acc * jax.nn.sigmoid(acc))
        xe_ref[part] = xf[ts - HALO:ts]

    ab = ab_ref[...]
    xg = ab + dtb_ref[...]
    softplus = jnp.maximum(xg, 0.0) + jnp.log(1.0 + jnp.exp(-jnp.abs(xg)))
    g_cum = -jnp.exp(alog_ref[...]) * softplus
    beta = jax.nn.sigmoid(ab)
    row_in_chunk = lax.broadcasted_iota(jnp.int32, (ts, 128), 0) & (ch - 1)
    step = 1
    while step < ch:
        g_cum = g_cum + jnp.where(row_in_chunk >= step, pltpu.roll(g_cum, step, 0), 0.0)
        step *= 2
    g_end = jnp.concatenate(
        [jnp.broadcast_to(g_cum[(c + 1) * ch - 1:(c + 1) * ch, :], (ch, 128)) for c in range(n_ch)], axis=0)
    e_cum = jnp.exp(g_cum)
    e_rel = jnp.exp(g_end - g_cum)
    e_end = jnp.exp(g_end)
    g_cum_t = jnp.transpose(g_cum)

    ri = lax.broadcasted_iota(jnp.int32, (ts, ts), 0)
    ci = lax.broadcasted_iota(jnp.int32, (ts, ts), 1)
    same = (ri // ch) == (ci // ch)
    tril = same & (ri >= ci)
    strict = same & (ri > ci)
    ri_b = lax.broadcasted_iota(jnp.int32, (GDN_INV, GDN_INV), 0)
    ci_b = lax.broadcasted_iota(jnp.int32, (GDN_INV, GDN_INV), 1)
    ident = jnp.where(ri_b == ci_b, 1.0, 0.0)

    heads = []
    for hh in range(hb):
        cs = slice(hh * dh, (hh + 1) * dh)
        qh, kh, vh = conv[0][:, cs], conv[1][:, cs], conv[2][:, cs]
        qh = qh * lax.rsqrt(jnp.sum(qh * qh, axis=-1, keepdims=True) + EPS) * (dh ** -0.5)
        kh = kh * lax.rsqrt(jnp.sum(kh * kh, axis=-1, keepdims=True) + EPS)
        bt = beta[:, hb + hh:hb + hh + 1]
        diff = g_cum[:, hh:hh + 1] - g_cum_t[hh:hh + 1, :]
        e = jnp.exp(jnp.minimum(diff, 0.0))
        kb = kh * bt
        k_bf = kh.astype(BF16)
        a_mat = _dot_nt(kb.astype(BF16), k_bf) * jnp.where(strict, e, 0.0)
        qk = (_dot_nt(qh.astype(BF16), k_bf) * jnp.where(tril, e, 0.0)).astype(BF16)
        heads.append(dict(
            a=a_mat, qk=qk,
            vb=(vh * bt).astype(BF16),
            kbg=(kb * e_cum[:, hh:hh + 1]).astype(BF16),
            qg=qh * e_cum[:, hh:hh + 1],
        ))
        heads[-1]["kd"] = (kh * e_rel[:, hh:hh + 1]).astype(BF16)

    probs = []
    for hh in range(hb):
        for blk in range(n_inv):
            bs = slice(blk * GDN_INV, (blk + 1) * GDN_INV)
            pw = -heads[hh]["a"][bs, bs]
            probs.append([pw, ident + pw])
    for _ in range(5):
        for pr in probs:
            pw_bf = pr[0].astype(BF16)
            pr[0] = _dot(pw_bf, pw_bf)
        for pr in probs:
            pr[1] = pr[1] + _dot(pr[1].astype(BF16), pr[0].astype(BF16))
    for hh in range(hb):
        u_parts, w_parts = [], []
        for blk in range(n_inv):
            bs = slice(blk * GDN_INV, (blk + 1) * GDN_INV)
            t_bf = probs[hh * n_inv + blk][1].astype(BF16)
            u_parts.append(_dot(t_bf, heads[hh]["vb"][bs]))
            w_parts.append(_dot(t_bf, heads[hh]["kbg"][bs]))
        heads[hh]["u"] = jnp.concatenate(u_parts, axis=0).astype(BF16)
        heads[hh]["w"] = jnp.concatenate(w_parts, axis=0).astype(BF16)
    for hd in heads:
        hd["qkw"] = _dot(hd["qk"], hd["w"])
        hd["qku"] = _dot(hd["qk"], hd["u"])
        hd["mw"] = [_dot_tn(hd["kd"][c * ch:(c + 1) * ch],
                            jnp.concatenate([hd["w"][c * ch:(c + 1) * ch], hd["u"][c * ch:(c + 1) * ch]], axis=1))
                    for c in range(n_ch)]
    for hd in heads:
        hd["qp"] = (hd["qg"] - hd["qkw"]).astype(BF16)
        hd["mc"] = [mw[:, :dh].astype(BF16) for mw in hd["mw"]]
        hd["bc"] = [mw[:, dh:] for mw in hd["mw"]]

    states = [st_ref[hh] for hh in range(hb)]
    outs = [[] for _ in range(hb)]
    for c in range(n_ch):
        rs = slice(c * ch, (c + 1) * ch)
        for hh in range(hb):
            hd = heads[hh]
            r = _dot(jnp.concatenate([hd["qp"][rs], hd["mc"][c]], axis=0), states[hh].astype(BF16))
            outs[hh].append(r[:ch] + hd["qku"][rs])
            states[hh] = states[hh] * e_end[c * ch:c * ch + 1, hh:hh + 1] - r[ch:] + hd["bc"][c]

    for hh in range(hb):
        cs = slice(hh * dh, (hh + 1) * dh)
        st_ref[hh] = states[hh]
        o = jnp.concatenate(outs[hh], axis=0)
        o = o * lax.rsqrt(jnp.mean(o * o, axis=-1, keepdims=True) + EPS) * og_ref[...]
        z = z_ref[:, cs].astype(F32)
        o_ref[:, cs] = (o * (z * jax.nn.sigmoid(z))).astype(o_ref.dtype)


def _gdn(p3, ps3, conv_w3, alog_l, dtb_l, out_gain, shifts):
    b, seq, _ = p3.shape
    ts, hb, dh = GDN_TS, GDN_HB, HEAD_DIM
    wb = hb * dh
    qb, zb = OFF_GQKV // wb, OFF_Z // wb
    gw = GDN_DIM // wb
    return pl.pallas_call(
        _gdn_kernel,
        out_shape=jax.ShapeDtypeStruct((b, seq, GDN_DIM), BF16),
        grid=(b, GDN_NHG, seq // ts),
        in_specs=[
            pl.BlockSpec((None, ts, wb), lambda i, j, k: (i, k, qb + j)),
            pl.BlockSpec((None, ts, wb), lambda i, j, k: (i, k, qb + gw + j)),
            pl.BlockSpec((None, ts, wb), lambda i, j, k: (i, k, qb + 2 * gw + j)),
            pl.BlockSpec((None, ts, wb), lambda i, j, k: (i, k, zb + j)),
            pl.BlockSpec((None, ts, 128), lambda i, j, k: (i, k, NSA_GROUPS + j)),
            pl.BlockSpec((3, GDN_CONV, wb), lambda i, j, k: (0, 0, j)),
            pl.BlockSpec((None, 1, 128), lambda i, j, k: (j, 0, 0)),
            pl.BlockSpec((None, 1, 128), lambda i, j, k: (j, 0, 0)),
            pl.BlockSpec((1, dh), lambda i, j, k: (0, 0)),
            pl.BlockSpec((GDN_CONV - 1, ts, ts), lambda i, j, k: (0, 0, 0)),
        ],
        out_specs=pl.BlockSpec((None, ts, wb), lambda i, j, k: (i, k, j)),
        scratch_shapes=[
            pltpu.VMEM((3, HALO, wb), F32),
            pltpu.VMEM((hb, dh, dh), F32),
        ],
        compiler_params=_cparams(("parallel", "parallel", "arbitrary")),
        name="gdn",
    )(p3, p3, p3, p3, ps3, conv_w3, alog_l, dtb_l, out_gain, shifts)


def _merge_kernel(oa_ref, ob_ref, wa_ref, wb_ref, ma_ref, mb_ref, o_ref):
    ya = _dot(oa_ref[...], wa_ref[...])
    yb = _dot(ob_ref[...], wb_ref[...])
    mix = jax.nn.sigmoid(ma_ref[...].astype(F32)) * ya + jax.nn.sigmoid(mb_ref[...].astype(F32)) * yb
    o_ref[...] = mix.astype(o_ref.dtype)


def _merge(oa, ob, wa, wb, p2, bm=512, bn=1024):
    t, d = oa.shape
    n = wa.shape[1]
    ma_b, mb_b = OFF_MA // bn, OFF_MB // bn
    return pl.pallas_call(
        _merge_kernel,
        out_shape=jax.ShapeDtypeStruct((t, n), BF16),
        grid=(n // bn, t // bm),
        in_specs=[
            pl.BlockSpec((bm, d), lambda j, i: (i, 0)),
            pl.BlockSpec((bm, d), lambda j, i: (i, 0)),
            pl.BlockSpec((d, bn), lambda j, i: (0, j)),
            pl.BlockSpec((d, bn), lambda j, i: (0, j)),
            pl.BlockSpec((bm, bn), lambda j, i: (i, ma_b + j)),
            pl.BlockSpec((bm, bn), lambda j, i: (i, mb_b + j)),
        ],
        out_specs=pl.BlockSpec((bm, bn), lambda j, i: (i, j)),
        compiler_params=_cparams(("parallel", "parallel")),
        name="merge",
    )(oa, ob, wa, wb, p2, p2)


def _resid_matmul_kernel(a_ref, w_ref, r_ref, o_ref):
    o_ref[...] = r_ref[...] + _dot(a_ref[...], w_ref[...])


def _resid_matmul(a, w, resid, bm, bn):
    t, k = a.shape
    n = w.shape[1]
    return pl.pallas_call(
        _resid_matmul_kernel,
        out_shape=jax.ShapeDtypeStruct((t, n), F32),
        grid=(n // bn, t // bm),
        in_specs=[
            pl.BlockSpec((bm, k), lambda j, i: (i, 0)),
            pl.BlockSpec((k, bn), lambda j, i: (0, j)),
            pl.BlockSpec((bm, bn), lambda j, i: (i, j)),
        ],
        out_specs=pl.BlockSpec((bm, bn), lambda j, i: (i, j)),
        compiler_params=_cparams(("parallel", "parallel")),
        name="resid_matmul",
    )(a, w, resid)


def _ffn_up_kernel(x_ref, g_ref, wg_ref, wu_ref, o_ref, hn_ref):
    @pl.when(pl.program_id(1) == 0)
    def _():
        x = x_ref[...]
        hn_ref[...] = (x * lax.rsqrt(jnp.mean(x * x, axis=-1, keepdims=True) + EPS) * g_ref[...]).astype(BF16)

    hn = hn_ref[...]
    gate = _dot(hn, wg_ref[...].astype(BF16))
    up = _dot(hn, wu_ref[...].astype(BF16))
    o_ref[...] = (gate * jax.nn.sigmoid(gate) * up).astype(o_ref.dtype)


def _ffn_up(x1, gain, wg, wu, bm=1024, bn=512):
    t, d = x1.shape
    n = wg.shape[1]
    return pl.pallas_call(
        _ffn_up_kernel,
        out_shape=jax.ShapeDtypeStruct((t, n), BF16),
        grid=(t // bm, n // bn),
        in_specs=[
            pl.BlockSpec((bm, d), lambda i, j: (i, 0)),
            pl.BlockSpec((1, d), lambda i, j: (0, 0)),
            pl.BlockSpec((d, bn), lambda i, j: (0, j)),
            pl.BlockSpec((d, bn), lambda i, j: (0, j)),
        ],
        out_specs=pl.BlockSpec((bm, bn), lambda i, j: (i, j)),
        scratch_shapes=[pltpu.VMEM((bm, d), BF16)],
        compiler_params=_cparams(("parallel", "arbitrary")),
        name="ffn_up",
    )(x1, gain, wg, wu)


def _pack_in_weights(w):
    w_t = jnp.transpose(w).astype(BF16)
    off_gate = OFF_GQKV
    off_a = OFF_MA + SKIP_GATE
    gate_t = w_t[off_gate:off_gate + SKIP_GATE]
    a_t = w_t[off_a:off_a + GDN_HEADS]
    b_t = w_t[off_a + GDN_HEADS:off_a + SKIP_AB]
    small = []
    per_group = NSA_REP * N_BRANCH
    for g in range(NSA_GROUPS):
        small.append(jnp.pad(gate_t[g * per_group:(g + 1) * per_group], ((0, 128 - per_group), (0, 0))))
    for hg in range(GDN_NHG):
        hs = slice(hg * GDN_HB, (hg + 1) * GDN_HB)
        small.append(jnp.pad(jnp.concatenate([a_t[hs], b_t[hs]], axis=0), ((0, 128 - 2 * GDN_HB), (0, 0))))
    w_small_t = jnp.concatenate(small, axis=0)
    assert w_t.shape[0] == NP_BIG + SKIP_GATE + SKIP_AB and w_small_t.shape[0] == NP_SMALL
    return w_t, w_small_t


def _lane_rows(v):
    return jnp.pad(v.reshape(GDN_NHG, 1, GDN_HB).astype(F32), ((0, 0), (0, 0), (0, 128 - GDN_HB)))


def _nsa_constants(seq):
    n_cmp = (seq - CMP_BLOCK) // CMP_STRIDE + 1
    nc = seq // CMP_STRIDE
    n_sel = seq // SLC_BLOCK
    cmp_start = np.arange(n_cmp) * CMP_STRIDE
    sel_start = np.arange(n_sel) * SLC_BLOCK
    overlap = (np.minimum(cmp_start[:, None] + CMP_BLOCK, sel_start[None, :] + SLC_BLOCK)
               - np.maximum(cmp_start[:, None], sel_start[None, :]))
    agg = np.zeros((nc, n_sel), np.float32)
    agg[:n_cmp] = np.clip(overlap, 0, None) / CMP_BLOCK
    eye = np.eye(NSA_TQ, dtype=np.float32)
    i = np.arange(NSA_TQ)[:, None]
    c = np.arange(WINDOW + NSA_TQ)[None, :]
    win_bias = np.where((c > i) & (c <= i + WINDOW), 0.0, NEG_INF).astype(np.float32)
    lane = np.arange(HEAD_DIM)[None, :]
    n_part = NSA_TQ // CMP_STRIDE + 1
    hidden = ((lane < n_part) & (CMP_STRIDE * lane + CMP_STRIDE - 1 > i)) | (lane == n_part)
    cmp_qtag = np.where(hidden, NEG_INF, 0.0).astype(np.float32)
    expand0 = (np.arange(NSA_TK)[None, :] // SLC_BLOCK == np.arange(HEAD_DIM)[:, None]).astype(np.float32)
    return (jnp.asarray(agg.T, BF16), jnp.asarray(eye, BF16), jnp.asarray(win_bias), jnp.asarray(cmp_qtag, BF16),
            jnp.asarray(expand0, BF16))


def _gdn_constants():
    idx = np.arange(GDN_TS)
    shifts = np.stack([(idx[None, :] == idx[:, None] - d) for d in range(1, GDN_CONV)]).astype(np.float32)
    return jnp.asarray(shifts, BF16)


def _layer(x, attn_norm, w_in, nsa_q_norm, nsa_k_norm, cmp_pos, w_cmp, gdn_conv, gdn_a_log, gdn_dt_bias,
           gdn_out_norm, w_branch_a, w_branch_b, w_out, ffn_norm, w_gate, w_up, w_down):
    b, seq, d = x.shape
    t = b * seq
    x2 = x.reshape(t, d)

    w_t, w_small_t = _pack_in_weights(w_in)
    p2, ps2 = _in_proj(x2, attn_norm.reshape(1, d), w_t, w_small_t)
    p3 = p2.reshape(b, seq, NP_BIG)
    ps3 = ps2.reshape(b, seq, NP_SMALL)

    w_cmp2 = w_cmp.reshape(2, CMP_BLOCK * HEAD_DIM, HEAD_DIM).astype(BF16)
    pos2 = jnp.broadcast_to(cmp_pos.reshape(2, 1, CMP_BLOCK * HEAD_DIM), (2, 8, CMP_BLOCK * HEAD_DIM)).astype(BF16)
    kc, vc = _nsa_compress(p3, w_cmp2, pos2, nsa_k_norm[0:1])

    o_a = _nsa_attention(p3, ps3, kc, vc, *_nsa_constants(seq), nsa_q_norm.reshape(1, HEAD_DIM), nsa_k_norm)

    conv_w3 = gdn_conv.reshape(GDN_CONV, 3, GDN_DIM).transpose(1, 0, 2)
    o_b = _gdn(p3, ps3, conv_w3, _lane_rows(gdn_a_log), _lane_rows(gdn_dt_bias),
               gdn_out_norm.reshape(1, HEAD_DIM), _gdn_constants())

    mix = _merge(o_a.reshape(t, NSA_Q_DIM), o_b.reshape(t, GDN_DIM), w_branch_a.astype(BF16),
                 w_branch_b.astype(BF16), p2)
    x1 = _resid_matmul(mix, w_out.astype(BF16), x2, bm=1024, bn=1024)

    act = _ffn_up(x1, ffn_norm.reshape(1, d), w_gate, w_up)
    out = _resid_matmul(act, w_down.astype(BF16), x1, bm=512, bn=1024)
    return out.reshape(b, seq, d)


def kernel(x, attn_norm, w_in, nsa_q_norm, nsa_k_norm, cmp_pos, w_cmp, gdn_conv, gdn_a_log, gdn_dt_bias,
           gdn_out_norm, w_branch_a, w_branch_b, w_out, ffn_norm, w_gate, w_up, w_down):
    for l in range(attn_norm.shape[0]):
        x = _layer(x, attn_norm[l], w_in[l], nsa_q_norm[l], nsa_k_norm[l], cmp_pos[l], w_cmp[l], gdn_conv[l],
                   gdn_a_log[l], gdn_dt_bias[l], gdn_out_norm[l], w_branch_a[l], w_branch_b[l], w_out[l],
                   ffn_norm[l], w_gate[l], w_up[l], w_down[l])
    return x
```

```python
import functools

import numpy as np
import jax
import jax.numpy as jnp
from jax import lax
from jax.experimental import pallas as pl
from jax.experimental.pallas import tpu as pltpu

F32 = jnp.float32
BF16 = jnp.bfloat16

D_MODEL = 2048
EPS = 1e-6
NEG_INF = -1e30
TINY = 1e-30
FORCE_SCORE = 1e4
LOG2E = 1.4426950408889634

NSA_HEADS = 16
NSA_GROUPS = 4
NSA_REP = NSA_HEADS // NSA_GROUPS
HEAD_DIM = 128
N_BRANCH = 3
CMP_BLOCK = 32
CMP_STRIDE = 16
SLC_BLOCK = 64
SLC_TOPK = 16
WINDOW = 512

GDN_HEADS = 16
GDN_CONV = 4
GDN_CHUNK = 64

D_FF = 5632

NSA_Q_DIM = NSA_HEADS * HEAD_DIM
NSA_KV_DIM = N_BRANCH * 2 * NSA_GROUPS * HEAD_DIM
GDN_DIM = GDN_HEADS * HEAD_DIM
OFF_Q = 0
OFF_KV = OFF_Q + NSA_Q_DIM
OFF_GQKV = OFF_KV + NSA_KV_DIM
OFF_Z = OFF_GQKV + 3 * GDN_DIM
OFF_MA = OFF_Z + GDN_DIM
OFF_MB = OFF_MA + D_MODEL
NP_BIG = OFF_MB + D_MODEL

GDN_HB = 8
GDN_NHG = GDN_HEADS // GDN_HB
NP_SMALL = (NSA_GROUPS + GDN_NHG) * 128

VMEM_LIMIT = 56 * 1024 * 1024

NT_DIMS = (((1,), (1,)), ((), ()))


def _dot(a, b):
    return jnp.dot(a, b, preferred_element_type=F32)


def _dot_nt(a, b):
    return lax.dot_general(a, b, NT_DIMS, preferred_element_type=F32)


def _dot_tn(a, b):
    return lax.dot_general(a, b, (((0,), (0,)), ((), ())), preferred_element_type=F32)


def _cparams(sem):
    return pltpu.CompilerParams(dimension_semantics=sem, vmem_limit_bytes=VMEM_LIMIT)


def _in_proj_kernel(x_ref, g_ref, wt_ref, wst_ref, o_ref, os_ref, hn_ref):
    @pl.when(pl.program_id(1) == 0)
    def _():
        x = x_ref[...]
        y = x * lax.rsqrt(jnp.mean(x * x, axis=-1, keepdims=True) + EPS) * g_ref[...]
        hn = y.astype(BF16)
        hn_ref[...] = hn
        os_ref[...] = _dot_nt(hn, wst_ref[...])

    o_ref[...] = _dot_nt(hn_ref[...], wt_ref[...]).astype(o_ref.dtype)


SKIP_GATE = NSA_HEADS * N_BRANCH
SKIP_AB = 2 * GDN_HEADS


def _in_proj(x2, gain, w_t, w_small_t, bm=1024, bn=1024):
    t, d = x2.shape
    n = NP_BIG
    ns = w_small_t.shape[0]
    assert OFF_GQKV % bn == 0 and OFF_MA % bn == 0 and n % bn == 0

    def w_row(i, j):
        skip = jnp.where(j >= OFF_MA // bn, SKIP_GATE + SKIP_AB, jnp.where(j >= OFF_GQKV // bn, SKIP_GATE, 0))
        return pl.multiple_of(j * bn + skip, 16)

    return pl.pallas_call(
        _in_proj_kernel,
        out_shape=(jax.ShapeDtypeStruct((t, n), BF16), jax.ShapeDtypeStruct((t, ns), F32)),
        grid=(t // bm, n // bn),
        in_specs=[
            pl.BlockSpec((bm, d), lambda i, j: (i, 0)),
            pl.BlockSpec((1, d), lambda i, j: (0, 0)),
            pl.BlockSpec((pl.Element(bn), pl.Element(d)), lambda i, j: (w_row(i, j), 0)),
            pl.BlockSpec((ns, d), lambda i, j: (0, 0)),
        ],
        out_specs=(
            pl.BlockSpec((bm, bn), lambda i, j: (i, j)),
            pl.BlockSpec((bm, ns), lambda i, j: (i, 0)),
        ),
        scratch_shapes=[pltpu.VMEM((bm, d), BF16)],
        compiler_params=_cparams(("parallel", "arbitrary")),
        name="in_proj",
    )(x2, gain, w_t, w_small_t)


def _cmp_kernel(ck_ref, cv_ref, w_ref, pos_ref, kg_ref, kc_ref, vc_ref, xs_ref):
    dh = HEAD_DIM
    n = kc_ref.shape[0]

    def compress(x_ref, idx):
        xs_ref[...] = x_ref[...].astype(F32)
        lo = hi = None
        for l in range(CMP_STRIDE):
            xl = xs_ref[pl.ds(l, n, stride=CMP_STRIDE), :].astype(BF16)
            lo_l = _dot(xl, w_ref[idx, l * dh:(l + 1) * dh, :])
            hi_l = _dot(xl, w_ref[idx, (CMP_STRIDE + l) * dh:(CMP_STRIDE + l + 1) * dh, :])
            lo = lo_l if lo is None else lo + lo_l
            hi = hi_l if hi is None else hi + hi_l
        bias = _dot(pos_ref[idx], w_ref[idx])[0:1]
        return lo + pltpu.roll(hi, n - 1, 0) + bias

    kc = compress(ck_ref, 0)
    kc = kc * lax.rsqrt(jnp.mean(kc * kc, axis=-1, keepdims=True) + EPS) * kg_ref[...]
    kc_ref[...] = kc.astype(BF16)
    vc_ref[...] = compress(cv_ref, 1).astype(BF16)


def _nsa_compress(p3, w_cmp2, pos2, k_gain0):
    b, seq, _ = p3.shape
    g, dh = NSA_GROUPS, HEAD_DIM
    nc = seq // CMP_STRIDE
    width = CMP_BLOCK * dh
    kvb = OFF_KV // dh
    out = jax.ShapeDtypeStruct((b, g, nc, dh), BF16)
    return pl.pallas_call(
        _cmp_kernel,
        out_shape=(out, out),
        grid=(b, g),
        in_specs=[
            pl.BlockSpec((None, seq, dh), lambda i, j: (i, 0, kvb + j)),
            pl.BlockSpec((None, seq, dh), lambda i, j: (i, 0, kvb + g + j)),
            pl.BlockSpec((2, width, dh), lambda i, j: (0, 0, 0)),
            pl.BlockSpec((2, 8, width), lambda i, j: (0, 0, 0)),
            pl.BlockSpec((1, dh), lambda i, j: (0, 0)),
        ],
        out_specs=(
            pl.BlockSpec((None, None, nc, dh), lambda i, j: (i, j, 0, 0)),
            pl.BlockSpec((None, None, nc, dh), lambda i, j: (i, j, 0, 0)),
        ),
        scratch_shapes=[pltpu.VMEM((seq, dh), F32)],
        compiler_params=_cparams(("parallel", "parallel")),
        name="nsa_compress",
    )(p3, p3, w_cmp2, pos2, k_gain0)


NSA_TQ = 256
NSA_TK = 512
NORM_ROWS = 512
RANK_UNROLL = 4


def _rms(xf, gain):
    return xf * lax.rsqrt(jnp.mean(xf * xf, axis=-1, keepdims=True) + EPS) * gain


def _nsa_kernel(q_ref, gl_ref, ks_ref, vs_ref, kw_ref, vw_ref, kc_ref, vc_ref, aggt_ref, eye_ref, wb_ref, cq_ref,
                e0_ref, qg_ref, kg_ref, o_ref, ksa_ref, kwa_ref, vsa_ref, vwp_ref, sc_ref, rk_ref, s0_ref, s1_ref, m_ref, acc_ref, po_ref,
                *, seq):
    tq, tk, rep, dh = NSA_TQ, NSA_TK, NSA_REP, HEAD_DIM
    rows = rep * tq
    n_sel = seq // SLC_BLOCK
    n_topk = min(SLC_TOPK, n_sel)
    qi = pl.program_id(2)
    q0 = qi * tq

    @pl.when(qi == 0)
    def _():
        kwa_ref[0:WINDOW, 0:dh] = jnp.zeros((WINDOW, dh), BF16)
        kwa_ref[0:WINDOW, dh:2 * dh] = jnp.where(
            lax.broadcasted_iota(jnp.int32, (WINDOW, dh), 1) == 0, 1.0, 0.0).astype(BF16)
        ones = jnp.ones((NORM_ROWS, dh), BF16)
        vwp_ref[0:WINDOW, 0:dh] = jnp.zeros((WINDOW, dh), BF16)
        vwp_ref[0:WINDOW, dh:2 * dh] = jnp.ones((WINDOW, dh), BF16)
        lane = lax.broadcasted_iota(jnp.int32, (NORM_ROWS, dh), 1)
        sub = lax.broadcasted_iota(jnp.int32, (NORM_ROWS, dh), 0)

        def body(c, carry):
            r0 = pl.multiple_of(c * NORM_ROWS, NORM_ROWS)
            r = pl.ds(r0, NORM_ROWS)
            rw = pl.ds(r0 + WINDOW, NORM_ROWS)
            ksa_ref[r, 0:dh] = _rms(ks_ref[r, :].astype(F32), kg_ref[1:2, :]).astype(BF16)
            ksa_ref[r, dh:2 * dh] = jnp.where((r0 + sub) // SLC_BLOCK == lane, 1.0, 0.0).astype(BF16)
            kwa_ref[rw, 0:dh] = _rms(kw_ref[r, :].astype(F32), kg_ref[2:3, :]).astype(BF16)
            kwa_ref[rw, dh:2 * dh] = jnp.zeros((NORM_ROWS, dh), BF16)
            vsa_ref[r, 0:dh] = vs_ref[r, :]
            vsa_ref[r, dh:2 * dh] = ones
            vwp_ref[rw, 0:dh] = vw_ref[r, :]
            vwp_ref[rw, dh:2 * dh] = ones
            return carry
        lax.fori_loop(0, seq // NORM_ROWS, body, 0)

    qscale = dh ** -0.5 * LOG2E
    q = q_ref[...].astype(F32)
    qs = jnp.concatenate(
        [(_rms(q[:, r * dh:(r + 1) * dh], qg_ref[...]) * qscale).astype(BF16) for r in range(rep)], axis=0)

    s0_ref[...] = _dot_nt(qs, ksa_ref[0:tk, 0:dh])

    wk = WINDOW + tq
    padneg = jnp.where(lax.broadcasted_iota(jnp.int32, (rows, dh), 1) == 0, NEG_INF, 0.0).astype(BF16)
    qa_win = jnp.concatenate([qs, padneg], axis=1)
    kr = pl.ds(pl.multiple_of(q0, tq), wk)
    s_w = _dot_nt(qa_win, kwa_ref[kr, :]) + jnp.concatenate([wb_ref[...]] * rep, axis=0)

    n_cmp = kc_ref.shape[0]
    n_lo = q0 // CMP_STRIDE - 1
    c_n = jnp.minimum(lax.broadcasted_iota(jnp.int32, (n_cmp, dh), 0) - n_lo, tq // CMP_STRIDE + 1)
    ktag = jnp.where(lax.broadcasted_iota(jnp.int32, (n_cmp, dh), 1) == c_n, 1.0, 0.0).astype(BF16)
    qa_cmp = jnp.concatenate([qs, jnp.concatenate([cq_ref[...]] * rep, axis=0)], axis=1)
    s = _dot_nt(qa_cmp, jnp.concatenate([kc_ref[...], ktag], axis=1))
    p_w = jnp.exp2(s_w - jnp.max(s_w, axis=-1, keepdims=True))
    ov_w = _dot(p_w.astype(BF16), vwp_ref[kr, :])
    o_win = ov_w[:, :dh] / ov_w[:, dh:]
    e = jnp.exp2(s - jnp.max(s, axis=-1, keepdims=True))
    vc_aug = jnp.concatenate([vc_ref[...], jnp.ones((n_cmp, dh), BF16)], axis=1)
    ov = _dot(e.astype(BF16), vc_aug)
    t_row = q0 + (lax.broadcasted_iota(jnp.int32, (rows, dh), 0) & (tq - 1))
    inv = jnp.where(t_row >= CMP_BLOCK - 1, 1.0 / jnp.maximum(ov[:, dh:], TINY), 0.0)
    o_cmp = ov[:, :dh] * inv
    p = e * jnp.concatenate([inv] * (n_cmp // dh), axis=1)
    p_sum = p[0:tq]
    for r in range(1, rep):
        p_sum = p_sum + p[r * tq:(r + 1) * tq]
    imp_t = _dot_nt(aggt_ref[...], p_sum.astype(BF16))

    jidx = lax.broadcasted_iota(jnp.int32, (n_sel, tq), 0)
    t_l = q0 + lax.broadcasted_iota(jnp.int32, (n_sel, tq), 1)
    cur = t_l // SLC_BLOCK
    forced = (jidx == 0) | (jidx == cur) | (jidx == cur - 1)
    causal = jidx * SLC_BLOCK <= t_l
    jf = jidx.astype(F32)
    score = jnp.where(forced, FORCE_SCORE + jf, jnp.where(causal, imp_t, -FORCE_SCORE - jf))
    sc_ref[...] = score

    n_blk = (q0 + tq - 1) // SLC_BLOCK + 1
    n_it = (n_blk + RANK_UNROLL - 1) // RANK_UNROLL
    n_cand = n_it * RANK_UNROLL
    zeros = jnp.zeros((n_sel, tq), F32)

    def count_body(it, cnt):
        for u in range(RANK_UNROLL):
            row = sc_ref[pl.ds(it * RANK_UNROLL + u, 1), :]
            cnt = cnt + jnp.where(row > score, 1.0, 0.0)
        return cnt

    def rank_body(it, rank):
        for u in range(RANK_UNROLL):
            i = it * RANK_UNROLL + u
            row = sc_ref[pl.ds(i, 1), :]
            ge = jnp.where(row >= score, 1.0, 0.0)
            gt = jnp.where(row > score, 1.0, 0.0)
            rank = rank + jnp.where(jidx > i, ge, gt)
        return rank

    cnt = lax.fori_loop(0, n_it, count_body, zeros)
    total = jnp.sum(jnp.where(jidx < n_cand, cnt, 0.0), axis=0, keepdims=True)
    has_tie = jnp.min(total) < (n_cand * (n_cand - 1) // 2).astype(F32)
    rk_ref[...] = cnt

    gates = jax.nn.sigmoid(gl_ref[...])
    for r in range(rep):
        rs = slice(r * tq, (r + 1) * tq)
        po_ref[rs, :] = gates[:, 3 * r:3 * r + 1] * o_cmp[rs] + gates[:, 3 * r + 2:3 * r + 3] * o_win[rs]

    @pl.when(has_tie)
    def _():
        rk_ref[...] = lax.fori_loop(0, n_it, rank_body, zeros)

    selneg_t = jnp.where(rk_ref[...] < n_topk, 0.0, NEG_INF)
    if n_sel < dh:
        selneg_t = jnp.concatenate([selneg_t, jnp.zeros((dh - n_sel, tq), F32)], axis=0)
    selneg = _dot_nt(eye_ref[...], selneg_t.astype(BF16)).astype(BF16)
    qa_sel = jnp.concatenate([qs, jnp.concatenate([selneg] * rep, axis=0)], axis=1)

    def qk_tile(kt):
        return _dot_nt(qa_sel, ksa_ref[pl.ds(pl.multiple_of(kt * tk, tk), tk), :])

    def consume(kt, buf, diagonal):
        k0 = pl.multiple_of(kt * tk, tk)
        s = buf[...]
        if diagonal:
            t_r = q0 + (lax.broadcasted_iota(jnp.int32, (rows, tk), 0) & (tq - 1))
            kpos = k0 + lax.broadcasted_iota(jnp.int32, (rows, tk), 1)
            s = jnp.where(kpos <= t_r, s, NEG_INF)
        m_prev = m_ref[...]
        m_new = jnp.maximum(m_prev, jnp.max(s, axis=-1, keepdims=True))
        p = jnp.exp2(s - jnp.concatenate([m_new] * (tk // 128), axis=1))
        alpha = jnp.exp2(m_prev - m_new)
        m_ref[...] = m_new
        acc_ref[...] = (jnp.concatenate([alpha, alpha], axis=1) * acc_ref[...]
                        + _dot(p.astype(BF16), vsa_ref[pl.ds(k0, tk), :]))

    m_ref[...] = jnp.full((rows, 128), NEG_INF, F32)
    acc_ref[...] = jnp.zeros((rows, 2 * dh), F32)
    s0_ref[...] = s0_ref[...] + jnp.concatenate([_dot(selneg, e0_ref[...])] * rep, axis=0)
    n_full = q0 // tk

    def sel_body(it, carry):
        kt = 2 * it
        s1_ref[...] = qk_tile(kt + 1)
        consume(kt, s0_ref, False)
        s0_ref[...] = qk_tile(kt + 2)
        consume(kt + 1, s1_ref, False)
        return carry

    lax.fori_loop(0, n_full // 2, sel_body, 0)

    @pl.when(n_full % 2 == 1)
    def _():
        s1_ref[...] = qk_tile(n_full)
        consume(n_full - 1, s0_ref, False)
        consume(n_full, s1_ref, True)

    @pl.when(n_full % 2 == 0)
    def _():
        consume(n_full, s0_ref, True)

    o_slc = acc_ref[:, 0:dh] / acc_ref[:, dh:2 * dh]

    gate_slc = jax.nn.sigmoid(gl_ref[...])
    for r in range(rep):
        rs = slice(r * tq, (r + 1) * tq)
        o = po_ref[rs, :] + gate_slc[:, 3 * r + 1:3 * r + 2] * o_slc[rs]
        o_ref[:, r * dh:(r + 1) * dh] = o.astype(o_ref.dtype)


def _nsa_attention(p3, ps3, kc, vc, agg_t, eye, win_bias, cmp_qtag, expand0, q_gain, k_gain):
    b, seq, _ = p3.shape
    g, tq, dh = NSA_GROUPS, NSA_TQ, HEAD_DIM
    n_sel = seq // SLC_BLOCK
    assert n_sel <= dh and n_sel % 8 == 0 and seq % NSA_TK == 0
    nc = kc.shape[2]
    kvb = OFF_KV // dh

    def kv_spec(branch, is_v):
        base = kvb + (branch * 2 + is_v) * g
        return pl.BlockSpec((None, seq, dh), lambda i, j, k: (i, 0, base + j))

    qw = NSA_REP * dh
    return pl.pallas_call(
        functools.partial(_nsa_kernel, seq=seq),
        out_shape=jax.ShapeDtypeStruct((b, seq, NSA_Q_DIM), BF16),
        grid=(b, g, seq // tq),
        in_specs=[
            pl.BlockSpec((None, tq, qw), lambda i, j, k: (i, k, j)),
            pl.BlockSpec((None, tq, 128), lambda i, j, k: (i, k, j)),
            kv_spec(1, 0), kv_spec(1, 1), kv_spec(2, 0), kv_spec(2, 1),
            pl.BlockSpec((None, None, nc, dh), lambda i, j, k: (i, j, 0, 0)),
            pl.BlockSpec((None, None, nc, dh), lambda i, j, k: (i, j, 0, 0)),
            pl.BlockSpec((n_sel, nc), lambda i, j, k: (0, 0)),
            pl.BlockSpec((tq, tq), lambda i, j, k: (0, 0)),
            pl.BlockSpec((tq, WINDOW + tq), lambda i, j, k: (0, 0)),
            pl.BlockSpec((tq, dh), lambda i, j, k: (0, 0)),
            pl.BlockSpec((dh, NSA_TK), lambda i, j, k: (0, 0)),
            pl.BlockSpec((1, dh), lambda i, j, k: (0, 0)),
            pl.BlockSpec((N_BRANCH, dh), lambda i, j, k: (0, 0)),
        ],
        out_specs=pl.BlockSpec((None, tq, qw), lambda i, j, k: (i, k, j)),
        scratch_shapes=[
            pltpu.VMEM((seq, 2 * dh), BF16),
            pltpu.VMEM((seq + WINDOW, 2 * dh), BF16),
            pltpu.VMEM((seq, 2 * dh), BF16),
            pltpu.VMEM((seq + WINDOW, 2 * dh), BF16),
            pltpu.VMEM((n_sel, tq), F32),
            pltpu.VMEM((n_sel, tq), F32),
            pltpu.VMEM((NSA_REP * tq, NSA_TK), F32),
            pltpu.VMEM((NSA_REP * tq, NSA_TK), F32),
            pltpu.VMEM((NSA_REP * tq, 128), F32),
            pltpu.VMEM((NSA_REP * tq, 2 * dh), F32),
            pltpu.VMEM((NSA_REP * tq, dh), F32),
        ],
        compiler_params=_cparams(("parallel", "parallel", "arbitrary")),
        name="nsa_attention",
    )(p3, ps3, p3, p3, p3, p3, kc, vc, agg_t, eye, win_bias, cmp_qtag, expand0, q_gain, k_gain)


GDN_TS = 256
GDN_INV = 128
HALO = 8


def _gdn_kernel(q_ref, k_ref, v_ref, z_ref, ab_ref, cw_ref, alog_ref, dtb_ref, og_ref, sh_ref,
                o_ref, xe_ref, st_ref):
    ts, hb, dh, ch = GDN_TS, GDN_HB, HEAD_DIM, GDN_CHUNK
    ti = pl.program_id(2)

    n_inv = ts // GDN_INV
    n_ch = ts // ch

    @pl.when(ti == 0)
    def _():
        xe_ref[...] = jnp.zeros_like(xe_ref)
        st_ref[...] = jnp.zeros_like(st_ref)

    row8 = lax.broadcasted_iota(jnp.int32, (HALO, hb * dh), 0)
    conv = []
    for part, ref in enumerate((q_ref, k_ref, v_ref)):
        xb = ref[...]
        xf = xb.astype(F32)
        prev = xe_ref[part]
        acc = cw_ref[part, GDN_CONV - 1:GDN_CONV, :] * xf
        corr = jnp.zeros((HALO, hb * dh), F32)
        for d in range(1, GDN_CONV):
            wd = cw_ref[part, GDN_CONV - 1 - d:GDN_CONV - d, :]
            acc = acc + wd * _dot(sh_ref[d - 1], xb)
            corr = corr + wd * jnp.where(row8 < d, pltpu.roll(prev, d, 0), 0.0)
        acc = jnp.concatenate([acc[0:HALO] + corr, acc[HALO:]], axis=0)
        conv.append(acc * jax.nn.sigmoid(acc))
        xe_ref[part] = xf[ts - HALO:ts]

    ab = ab_ref[...]
    xg = ab + dtb_ref[...]
    softplus = jnp.maximum(xg, 0.0) + jnp.log(1.0 + jnp.exp(-jnp.abs(xg)))
    g_cum = -jnp.exp(alog_ref[...]) * softplus
    beta = jax.nn.sigmoid(ab)
    row_in_chunk = lax.broadcasted_iota(jnp.int32, (ts, 128), 0) & (ch - 1)
    step = 1
    while step < ch:
        g_cum = g_cum + jnp.where(row_in_chunk >= step, pltpu.roll(g_cum, step, 0), 0.0)
        step *= 2
    g_end = jnp.concatenate(
        [jnp.broadcast_to(g_cum[(c + 1) * ch - 1:(c + 1) * ch, :], (ch, 128)) for c in range(n_ch)], axis=0)
    e_cum = jnp.exp(g_cum)
    e_rel = jnp.exp(g_end - g_cum)
    e_end = jnp.exp(g_end)
    g_cum_t = jnp.transpose(g_cum)

    ri = lax.broadcasted_iota(jnp.int32, (ts, ts), 0)
    ci = lax.broadcasted_iota(jnp.int32, (ts, ts), 1)
    same = (ri // ch) == (ci // ch)
    tril = same & (ri >= ci)
    strict = same & (ri > ci)
    ri_b = lax.broadcasted_iota(jnp.int32, (GDN_INV, GDN_INV), 0)
    ci_b = lax.broadcasted_iota(jnp.int32, (GDN_INV, GDN_INV), 1)
    ident = jnp.where(ri_b == ci_b, 1.0, 0.0)

    heads = []
    for hh in range(hb):
        cs = slice(hh * dh, (hh + 1) * dh)
        qh, kh, vh = conv[0][:, cs], conv[1][:, cs], conv[2][:, cs]
        qh = qh * lax.rsqrt(jnp.sum(qh * qh, axis=-1, keepdims=True) + EPS) * (dh ** -0.5)
        kh = kh * lax.rsqrt(jnp.sum(kh * kh, axis=-1, keepdims=True) + EPS)
        bt = beta[:, hb + hh:hb + hh + 1]
        diff = g_cum[:, hh:hh + 1] - g_cum_t[hh:hh + 1, :]
        e = jnp.exp(jnp.minimum(diff, 0.0))
        kb = kh * bt
        k_bf = kh.astype(BF16)
        a_mat = _dot_nt(kb.astype(BF16), k_bf) * jnp.where(strict, e, 0.0)
        qk = (_dot_nt(qh.astype(BF16), k_bf) * jnp.where(tril, e, 0.0)).astype(BF16)
        heads.append(dict(
            a=a_mat, qk=qk,
            vb=(vh * bt).astype(BF16),
            kbg=(kb * e_cum[:, hh:hh + 1]).astype(BF16),
            qg=qh * e_cum[:, hh:hh + 1],
        ))
        heads[-1]["kd"] = (kh * e_rel[:, hh:hh + 1]).astype(BF16)

    probs = []
    for hh in range(hb):
        for blk in range(n_inv):
            bs = slice(blk * GDN_INV, (blk + 1) * GDN_INV)
            pw = -heads[hh]["a"][bs, bs]
            probs.append([pw, ident + pw])
    for _ in range(5):
        for pr in probs:
            pw_bf = pr[0].astype(BF16)
            pr[0] = _dot(pw_bf, pw_bf)
        for pr in probs:
            pr[1] = pr[1] + _dot(pr[1].astype(BF16), pr[0].astype(BF16))
    for hh in range(hb):
        u_parts, w_parts = [], []
        for blk in range(n_inv):
            bs = slice(blk * GDN_INV, (blk + 1) * GDN_INV)
            t_bf = probs[hh * n_inv + blk][1].astype(BF16)
            u_parts.append(_dot(t_bf, heads[hh]["vb"][bs]))
            w_parts.append(_dot(t_bf, heads[hh]["kbg"][bs]))
        heads[hh]["u"] = jnp.concatenate(u_parts, axis=0).astype(BF16)
        heads[hh]["w"] = jnp.concatenate(w_parts, axis=0).astype(BF16)
    for hd in heads:
        hd["qkw"] = _dot(hd["qk"], hd["w"])
        hd["qku"] = _dot(hd["qk"], hd["u"])
        hd["mw"] = [_dot_tn(hd["kd"][c * ch:(c + 1) * ch],
                            jnp.concatenate([hd["w"][c * ch:(c + 1) * ch], hd["u"][c * ch:(c + 1) * ch]], axis=1))
                    for c in range(n_ch)]
    for hd in heads:
        hd["qp"] = (hd["qg"] - hd["qkw"]).astype(BF16)
        hd["mc"] = [mw[:, :dh].astype(BF16) for mw in hd["mw"]]
        hd["bc"] = [mw[:, dh:] for mw in hd["mw"]]

    states = [st_ref[hh] for hh in range(hb)]
    outs = [[] for _ in range(hb)]
    for c in range(n_ch):
        rs = slice(c * ch, (c + 1) * ch)
        for hh in range(hb):
            hd = heads[hh]
            r = _dot(jnp.concatenate([hd["qp"][rs], hd["mc"][c]], axis=0), states[hh].astype(BF16))
            outs[hh].append(r[:ch] + hd["qku"][rs])
            states[hh] = states[hh] * e_end[c * ch:c * ch + 1, hh:hh + 1] - r[ch:] + hd["bc"][c]

    for hh in range(hb):
        cs = slice(hh * dh, (hh + 1) * dh)
        st_ref[hh] = states[hh]
        o = jnp.concatenate(outs[hh], axis=0)
        o = o * lax.rsqrt(jnp.mean(o * o, axis=-1, keepdims=True) + EPS) * og_ref[...]
        z = z_ref[:, cs].astype(F32)
        o_ref[:, cs] = (o * (z * jax.nn.sigmoid(z))).astype(o_ref.dtype)


def _gdn(p3, ps3, conv_w3, alog_l, dtb_l, out_gain, shifts):
    b, seq, _ = p3.shape
    ts, hb, dh = GDN_TS, GDN_HB, HEAD_DIM
    wb = hb * dh
    qb, zb = OFF_GQKV // wb, OFF_Z // wb
    gw = GDN_DIM // wb
    return pl.pallas_call(
        _gdn_kernel,
        out_shape=jax.ShapeDtypeStruct((b, seq, GDN_DIM), BF16),
        grid=(b, GDN_NHG, seq // ts),
        in_specs=[
            pl.BlockSpec((None, ts, wb), lambda i, j, k: (i, k, qb + j)),
            pl.BlockSpec((None, ts, wb), lambda i, j, k: (i, k, qb + gw + j)),
            pl.BlockSpec((None, ts, wb), lambda i, j, k: (i, k, qb + 2 * gw + j)),
            pl.BlockSpec((None, ts, wb), lambda i, j, k: (i, k, zb + j)),
            pl.BlockSpec((None, ts, 128), lambda i, j, k: (i, k, NSA_GROUPS + j)),
            pl.BlockSpec((3, GDN_CONV, wb), lambda i, j, k: (0, 0, j)),
            pl.BlockSpec((None, 1, 128), lambda i, j, k: (j, 0, 0)),
            pl.BlockSpec((None, 1, 128), lambda i, j, k: (j, 0, 0)),
            pl.BlockSpec((1, dh), lambda i, j, k: (0, 0)),
            pl.BlockSpec((GDN_CONV - 1, ts, ts), lambda i, j, k: (0, 0, 0)),
        ],
        out_specs=pl.BlockSpec((None, ts, wb), lambda i, j, k: (i, k, j)),
        scratch_shapes=[
            pltpu.VMEM((3, HALO, wb), F32),
            pltpu.VMEM((hb, dh, dh), F32),
        ],
        compiler_params=_cparams(("parallel", "parallel", "arbitrary")),
        name="gdn",
    )(p3, p3, p3, p3, ps3, conv_w3, alog_l, dtb_l, out_gain, shifts)


def _merge_kernel(oa_ref, ob_ref, wa_ref, wb_ref, ma_ref, mb_ref, o_ref):
    ya = _dot(oa_ref[...], wa_ref[...])
    yb = _dot(ob_ref[...], wb_ref[...])
    mix = jax.nn.sigmoid(ma_ref[...].astype(F32)) * ya + jax.nn.sigmoid(mb_ref[...].astype(F32)) * yb
    o_ref[...] = mix.astype(o_ref.dtype)


def _merge(oa, ob, wa, wb, p2, bm=512, bn=1024):
    t, d = oa.shape
    n = wa.shape[1]
    ma_b, mb_b = OFF_MA // bn, OFF_MB // bn
    return pl.pallas_call(
        _merge_kernel,
        out_shape=jax.ShapeDtypeStruct((t, n), BF16),
        grid=(n // bn, t // bm),
        in_specs=[
            pl.BlockSpec((bm, d), lambda j, i: (i, 0)),
            pl.BlockSpec((bm, d), lambda j, i: (i, 0)),
            pl.BlockSpec((d, bn), lambda j, i: (0, j)),
            pl.BlockSpec((d, bn), lambda j, i: (0, j)),
            pl.BlockSpec((bm, bn), lambda j, i: (i, ma_b + j)),
            pl.BlockSpec((bm, bn), lambda j, i: (i, mb_b + j)),
        ],
        out_specs=pl.BlockSpec((bm, bn), lambda j, i: (i, j)),
        compiler_params=_cparams(("parallel", "parallel")),
        name="merge",
    )(oa, ob, wa, wb, p2, p2)


def _resid_matmul_kernel(a_ref, w_ref, r_ref, o_ref):
    o_ref[...] = r_ref[...] + _dot(a_ref[...], w_ref[...])


def _resid_matmul(a, w, resid, bm, bn):
    t, k = a.shape
    n = w.shape[1]
    return pl.pallas_call(
        _resid_matmul_kernel,
        out_shape=jax.ShapeDtypeStruct((t, n), F32),
        grid=(n // bn, t // bm),
        in_specs=[
            pl.BlockSpec((bm, k), lambda j, i: (i, 0)),
            pl.BlockSpec((k, bn), lambda j, i: (0, j)),
            pl.BlockSpec((bm, bn), lambda j, i: (i, j)),
        ],
        out_specs=pl.BlockSpec((bm, bn), lambda j, i: (i, j)),
        compiler_params=_cparams(("parallel", "parallel")),
        name="resid_matmul",
    )(a, w, resid)


def _ffn_up_kernel(x_ref, g_ref, wg_ref, wu_ref, o_ref, hn_ref):
    @pl.when(pl.program_id(1) == 0)
    def _():
        x = x_ref[...]
        hn_ref[...] = (x * lax.rsqrt(jnp.mean(x * x, axis=-1, keepdims=True) + EPS) * g_ref[...]).astype(BF16)

    hn = hn_ref[...]
    gate = _dot(hn, wg_ref[...].astype(BF16))
    up = _dot(hn, wu_ref[...].astype(BF16))
    o_ref[...] = (gate * jax.nn.sigmoid(gate) * up).astype(o_ref.dtype)


def _ffn_up(x1, gain, wg, wu, bm=1024, bn=512):
    t, d = x1.shape
    n = wg.shape[1]
    return pl.pallas_call(
        _ffn_up_kernel,
        out_shape=jax.ShapeDtypeStruct((t, n), BF16),
        grid=(t // bm, n // bn),
        in_specs=[
            pl.BlockSpec((bm, d), lambda i, j: (i, 0)),
            pl.BlockSpec((1, d), lambda i, j: (0, 0)),
            pl.BlockSpec((d, bn), lambda i, j: (0, j)),
            pl.BlockSpec((d, bn), lambda i, j: (0, j)),
        ],
        out_specs=pl.BlockSpec((bm, bn), lambda i, j: (i, j)),
        scratch_shapes=[pltpu.VMEM((bm, d), BF16)],
        compiler_params=_cparams(("parallel", "arbitrary")),
        name="ffn_up",
    )(x1, gain, wg, wu)


def _pack_in_weights(w):
    w_t = jnp.transpose(w).astype(BF16)
    off_gate = OFF_GQKV
    off_a = OFF_MA + SKIP_GATE
    gate_t = w_t[off_gate:off_gate + SKIP_GATE]
    a_t = w_t[off_a:off_a + GDN_HEADS]
    b_t = w_t[off_a + GDN_HEADS:off_a + SKIP_AB]
    small = []
    per_group = NSA_REP * N_BRANCH
    for g in range(NSA_GROUPS):
        small.append(jnp.pad(gate_t[g * per_group:(g + 1) * per_group], ((0, 128 - per_group), (0, 0))))
    for hg in range(GDN_NHG):
        hs = slice(hg * GDN_HB, (hg + 1) * GDN_HB)
        small.append(jnp.pad(jnp.concatenate([a_t[hs], b_t[hs]], axis=0), ((0, 128 - 2 * GDN_HB), (0, 0))))
    w_small_t = jnp.concatenate(small, axis=0)
    assert w_t.shape[0] == NP_BIG + SKIP_GATE + SKIP_AB and w_small_t.shape[0] == NP_SMALL
    return w_t, w_small_t


def _lane_rows(v):
    return jnp.pad(v.reshape(GDN_NHG, 1, GDN_HB).astype(F32), ((0, 0), (0, 0), (0, 128 - GDN_HB)))


def _nsa_constants(seq):
    n_cmp = (seq - CMP_BLOCK) // CMP_STRIDE + 1
    nc = seq // CMP_STRIDE
    n_sel = seq // SLC_BLOCK
    cmp_start = np.arange(n_cmp) * CMP_STRIDE
    sel_start = np.arange(n_sel) * SLC_BLOCK
    overlap = (np.minimum(cmp_start[:, None] + CMP_BLOCK, sel_start[None, :] + SLC_BLOCK)
               - np.maximum(cmp_start[:, None], sel_start[None, :]))
    agg = np.zeros((nc, n_sel), np.float32)
    agg[:n_cmp] = np.clip(overlap, 0, None) / CMP_BLOCK
    eye = np.eye(NSA_TQ, dtype=np.float32)
    i = np.arange(NSA_TQ)[:, None]
    c = np.arange(WINDOW + NSA_TQ)[None, :]
    win_bias = np.where((c > i) & (c <= i + WINDOW), 0.0, NEG_INF).astype(np.float32)
    lane = np.arange(HEAD_DIM)[None, :]
    n_part = NSA_TQ // CMP_STRIDE + 1
    hidden = ((lane < n_part) & (CMP_STRIDE * lane + CMP_STRIDE - 1 > i)) | (lane == n_part)
    cmp_qtag = np.where(hidden, NEG_INF, 0.0).astype(np.float32)
    expand0 = (np.arange(NSA_TK)[None, :] // SLC_BLOCK == np.arange(HEAD_DIM)[:, None]).astype(np.float32)
    return (jnp.asarray(agg.T, BF16), jnp.asarray(eye, BF16), jnp.asarray(win_bias), jnp.asarray(cmp_qtag, BF16),
            jnp.asarray(expand0, BF16))


def _gdn_constants():
    idx = np.arange(GDN_TS)
    shifts = np.stack([(idx[None, :] == idx[:, None] - d) for d in range(1, GDN_CONV)]).astype(np.float32)
    return jnp.asarray(shifts, BF16)


def _layer(x, attn_norm, w_in, nsa_q_norm, nsa_k_norm, cmp_pos, w_cmp, gdn_conv, gdn_a_log, gdn_dt_bias,
           gdn_out_norm, w_branch_a, w_branch_b, w_out, ffn_norm, w_gate, w_up, w_down):
    b, seq, d = x.shape
    t = b * seq
    x2 = x.reshape(t, d)

    w_t, w_small_t = _pack_in_weights(w_in)
    p2, ps2 = _in_proj(x2, attn_norm.reshape(1, d), w_t, w_small_t)
    p3 = p2.reshape(b, seq, NP_BIG)
    ps3 = ps2.reshape(b, seq, NP_SMALL)

    w_cmp2 = w_cmp.reshape(2, CMP_BLOCK * HEAD_DIM, HEAD_DIM).astype(BF16)
    pos2 = jnp.broadcast_to(cmp_pos.reshape(2, 1, CMP_BLOCK * HEAD_DIM), (2, 8, CMP_BLOCK * HEAD_DIM)).astype(BF16)
    kc, vc = _nsa_compress(p3, w_cmp2, pos2, nsa_k_norm[0:1])

    o_a = _nsa_attention(p3, ps3, kc, vc, *_nsa_constants(seq), nsa_q_norm.reshape(1, HEAD_DIM), nsa_k_norm)

    conv_w3 = gdn_conv.reshape(GDN_CONV, 3, GDN_DIM).transpose(1, 0, 2)
    o_b = _gdn(p3, ps3, conv_w3, _lane_rows(gdn_a_log), _lane_rows(gdn_dt_bias),
               gdn_out_norm.reshape(1, HEAD_DIM), _gdn_constants())

    mix = _merge(o_a.reshape(t, NSA_Q_DIM), o_b.reshape(t, GDN_DIM), w_branch_a.astype(BF16),
                 w_branch_b.astype(BF16), p2)
    x1 = _resid_matmul(mix, w_out.astype(BF16), x2, bm=1024, bn=1024)

    act = _ffn_up(x1, ffn_norm.reshape(1, d), w_gate, w_up)
    out = _resid_matmul(act, w_down.astype(BF16), x1, bm=512, bn=1024)
    return out.reshape(b, seq, d)


def kernel(x, attn_norm, w_in, nsa_q_norm, nsa_k_norm, cmp_pos, w_cmp, gdn_conv, gdn_a_log, gdn_dt_bias,
           gdn_out_norm, w_branch_a, w_branch_b, w_out, ffn_norm, w_gate, w_up, w_down):
    for l in range(attn_norm.shape[0]):
        x = _layer(x, attn_norm[l], w_in[l], nsa_q_norm[l], nsa_k_norm[l], cmp_pos[l], w_cmp[l], gdn_conv[l],
                   gdn_a_log[l], gdn_dt_bias[l], gdn_out_norm[l], w_branch_a[l], w_branch_b[l], w_out[l],
                   ffn_norm[l], w_gate[l], w_up[l], w_down[l])
    return x
```

```python
import functools

import numpy as np
import jax
import jax.numpy as jnp
from jax import lax
from jax.experimental import pallas as pl
from jax.experimental.pallas import tpu as pltpu

F32 = jnp.float32
BF16 = jnp.bfloat16

D_MODEL = 2048
EPS = 1e-6
NEG_INF = -1e30
TINY = 1e-30
FORCE_SCORE = 1e4
LOG2E = 1.4426950408889634

NSA_HEADS = 16
NSA_GROUPS = 4
NSA_REP = NSA_HEADS // NSA_GROUPS
HEAD_DIM = 128
N_BRANCH = 3
CMP_BLOCK = 32
CMP_STRIDE = 16
SLC_BLOCK = 64
SLC_TOPK = 16
WINDOW = 512

GDN_HEADS = 16
GDN_CONV = 4
GDN_CHUNK = 64

D_FF = 5632

NSA_Q_DIM = NSA_HEADS * HEAD_DIM
NSA_KV_DIM = N_BRANCH * 2 * NSA_GROUPS * HEAD_DIM
GDN_DIM = GDN_HEADS * HEAD_DIM
OFF_Q = 0
OFF_KV = OFF_Q + NSA_Q_DIM
OFF_GQKV = OFF_KV + NSA_KV_DIM
OFF_Z = OFF_GQKV + 3 * GDN_DIM
OFF_MA = OFF_Z + GDN_DIM
OFF_MB = OFF_MA + D_MODEL
NP_BIG = OFF_MB + D_MODEL

GDN_HB = 8
GDN_NHG = GDN_HEADS // GDN_HB
NP_SMALL = (NSA_GROUPS + GDN_NHG) * 128

VMEM_LIMIT = 56 * 1024 * 1024

NT_DIMS = (((1,), (1,)), ((), ()))


def _dot(a, b):
    return jnp.dot(a, b, preferred_element_type=F32)


def _dot_nt(a, b):
    return lax.dot_general(a, b, NT_DIMS, preferred_element_type=F32)


def _dot_tn(a, b):
    return lax.dot_general(a, b, (((0,), (0,)), ((), ())), preferred_element_type=F32)


def _cparams(sem):
    return pltpu.CompilerParams(dimension_semantics=sem, vmem_limit_bytes=VMEM_LIMIT)


def _in_proj_kernel(x_ref, g_ref, wt_ref, wst_ref, o_ref, os_ref, hn_ref):
    @pl.when(pl.program_id(1) == 0)
    def _():
        x = x_ref[...]
        y = x * lax.rsqrt(jnp.mean(x * x, axis=-1, keepdims=True) + EPS) * g_ref[...]
        hn = y.astype(BF16)
        hn_ref[...] = hn
        os_ref[...] = _dot_nt(hn, wst_ref[...])

    o_ref[...] = _dot_nt(hn_ref[...], wt_ref[...]).astype(o_ref.dtype)


SKIP_GATE = NSA_HEADS * N_BRANCH
SKIP_AB = 2 * GDN_HEADS


def _in_proj(x2, gain, w_t, w_small_t, bm=1024, bn=1024):
    t, d = x2.shape
    n = NP_BIG
    ns = w_small_t.shape[0]
    assert OFF_GQKV % bn == 0 and OFF_MA % bn == 0 and n % bn == 0

    def w_row(i, j):
        skip = jnp.where(j >= OFF_MA // bn, SKIP_GATE + SKIP_AB, jnp.where(j >= OFF_GQKV // bn, SKIP_GATE, 0))
        return pl.multiple_of(j * bn + skip, 16)

    return pl.pallas_call(
        _in_proj_kernel,
        out_shape=(jax.ShapeDtypeStruct((t, n), BF16), jax.ShapeDtypeStruct((t, ns), F32)),
        grid=(t // bm, n // bn),
        in_specs=[
            pl.BlockSpec((bm, d), lambda i, j: (i, 0)),
            pl.BlockSpec((1, d), lambda i, j: (0, 0)),
            pl.BlockSpec((pl.Element(bn), pl.Element(d)), lambda i, j: (w_row(i, j), 0)),
            pl.BlockSpec((ns, d), lambda i, j: (0, 0)),
        ],
        out_specs=(
            pl.BlockSpec((bm, bn), lambda i, j: (i, j)),
            pl.BlockSpec((bm, ns), lambda i, j: (i, 0)),
        ),
        scratch_shapes=[pltpu.VMEM((bm, d), BF16)],
        compiler_params=_cparams(("parallel", "arbitrary")),
        name="in_proj",
    )(x2, gain, w_t, w_small_t)


def _cmp_kernel(ck_ref, cv_ref, w_ref, pos_ref, kg_ref, kc_ref, vc_ref, xs_ref):
    dh = HEAD_DIM
    n = kc_ref.shape[0]

    def compress(x_ref, idx):
        xs_ref[...] = x_ref[...].astype(F32)
        lo = hi = None
        for l in range(CMP_STRIDE):
            xl = xs_ref[pl.ds(l, n, stride=CMP_STRIDE), :].astype(BF16)
            lo_l = _dot(xl, w_ref[idx, l * dh:(l + 1) * dh, :])
            hi_l = _dot(xl, w_ref[idx, (CMP_STRIDE + l) * dh:(CMP_STRIDE + l + 1) * dh, :])
            lo = lo_l if lo is None else lo + lo_l
            hi = hi_l if hi is None else hi + hi_l
        bias = _dot(pos_ref[idx], w_ref[idx])[0:1]
        return lo + pltpu.roll(hi, n - 1, 0) + bias

    kc = compress(ck_ref, 0)
    kc = kc * lax.rsqrt(jnp.mean(kc * kc, axis=-1, keepdims=True) + EPS) * kg_ref[...]
    kc_ref[...] = kc.astype(BF16)
    vc_ref[...] = compress(cv_ref, 1).astype(BF16)


def _nsa_compress(p3, w_cmp2, pos2, k_gain0):
    b, seq, _ = p3.shape
    g, dh = NSA_GROUPS, HEAD_DIM
    nc = seq // CMP_STRIDE
    width = CMP_BLOCK * dh
    kvb = OFF_KV // dh
    out = jax.ShapeDtypeStruct((b, g, nc, dh), BF16)
    return pl.pallas_call(
        _cmp_kernel,
        out_shape=(out, out),
        grid=(b, g),
        in_specs=[
            pl.BlockSpec((None, seq, dh), lambda i, j: (i, 0, kvb + j)),
            pl.BlockSpec((None, seq, dh), lambda i, j: (i, 0, kvb + g + j)),
            pl.BlockSpec((2, width, dh), lambda i, j: (0, 0, 0)),
            pl.BlockSpec((2, 8, width), lambda i, j: (0, 0, 0)),
            pl.BlockSpec((1, dh), lambda i, j: (0, 0)),
        ],
        out_specs=(
            pl.BlockSpec((None, None, nc, dh), lambda i, j: (i, j, 0, 0)),
            pl.BlockSpec((None, None, nc, dh), lambda i, j: (i, j, 0, 0)),
        ),
        scratch_shapes=[pltpu.VMEM((seq, dh), F32)],
        compiler_params=_cparams(("parallel", "parallel")),
        name="nsa_compress",
    )(p3, p3, w_cmp2, pos2, k_gain0)


NSA_TQ = 256
NSA_TK = 512
NORM_ROWS = 512
RANK_UNROLL = 4


def _rms(xf, gain):
    return xf * lax.rsqrt(jnp.mean(xf * xf, axis=-1, keepdims=True) + EPS) * gain


def _nsa_kernel(q_ref, gl_ref, ks_ref, vs_ref, kw_ref, vw_ref, kc_ref, vc_ref, aggt_ref, eye_ref, wb_ref, cq_ref,
                e0_ref, qg_ref, kg_ref, o_ref, ksa_ref, kwa_ref, vsa_ref, vwp_ref, sc_ref, s0_ref, s1_ref, m_ref, acc_ref,
                *, seq):
    tq, tk, rep, dh = NSA_TQ, NSA_TK, NSA_REP, HEAD_DIM
    rows = rep * tq
    n_sel = seq // SLC_BLOCK
    n_topk = min(SLC_TOPK, n_sel)
    qi = pl.program_id(2)
    q0 = qi * tq

    @pl.when(qi == 0)
    def _():
        kwa_ref[0:WINDOW, 0:dh] = jnp.zeros((WINDOW, dh), BF16)
        kwa_ref[0:WINDOW, dh:2 * dh] = jnp.where(
            lax.broadcasted_iota(jnp.int32, (WINDOW, dh), 1) == 0, 1.0, 0.0).astype(BF16)
        ones = jnp.ones((NORM_ROWS, dh), BF16)
        vwp_ref[0:WINDOW, 0:dh] = jnp.zeros((WINDOW, dh), BF16)
        vwp_ref[0:WINDOW, dh:2 * dh] = jnp.ones((WINDOW, dh), BF16)
        lane = lax.broadcasted_iota(jnp.int32, (NORM_ROWS, dh), 1)
        sub = lax.broadcasted_iota(jnp.int32, (NORM_ROWS, dh), 0)

        def body(c, carry):
            r0 = pl.multiple_of(c * NORM_ROWS, NORM_ROWS)
            r = pl.ds(r0, NORM_ROWS)
            rw = pl.ds(r0 + WINDOW, NORM_ROWS)
            ksa_ref[r, 0:dh] = _rms(ks_ref[r, :].astype(F32), kg_ref[1:2, :]).astype(BF16)
            ksa_ref[r, dh:2 * dh] = jnp.where((r0 + sub) // SLC_BLOCK == lane, 1.0, 0.0).astype(BF16)
            kwa_ref[rw, 0:dh] = _rms(kw_ref[r, :].astype(F32), kg_ref[2:3, :]).astype(BF16)
            kwa_ref[rw, dh:2 * dh] = jnp.zeros((NORM_ROWS, dh), BF16)
            vsa_ref[r, 0:dh] = vs_ref[r, :]
            vsa_ref[r, dh:2 * dh] = ones
            vwp_ref[rw, 0:dh] = vw_ref[r, :]
            vwp_ref[rw, dh:2 * dh] = ones
            return carry
        lax.fori_loop(0, seq // NORM_ROWS, body, 0)

    qscale = dh ** -0.5 * LOG2E
    q = q_ref[...].astype(F32)
    qs = jnp.concatenate(
        [(_rms(q[:, r * dh:(r + 1) * dh], qg_ref[...]) * qscale).astype(BF16) for r in range(rep)], axis=0)

    s0_ref[...] = _dot_nt(qs, ksa_ref[0:tk, 0:dh])

    wk = WINDOW + tq
    padneg = jnp.where(lax.broadcasted_iota(jnp.int32, (rows, dh), 1) == 0, NEG_INF, 0.0).astype(BF16)
    qa_win = jnp.concatenate([qs, padneg], axis=1)
    kr = pl.ds(pl.multiple_of(q0, tq), wk)
    s_w = _dot_nt(qa_win, kwa_ref[kr, :]) + jnp.concatenate([wb_ref[...]] * rep, axis=0)

    n_cmp = kc_ref.shape[0]
    n_lo = q0 // CMP_STRIDE - 1
    c_n = jnp.minimum(lax.broadcasted_iota(jnp.int32, (n_cmp, dh), 0) - n_lo, tq // CMP_STRIDE + 1)
    ktag = jnp.where(lax.broadcasted_iota(jnp.int32, (n_cmp, dh), 1) == c_n, 1.0, 0.0).astype(BF16)
    qa_cmp = jnp.concatenate([qs, jnp.concatenate([cq_ref[...]] * rep, axis=0)], axis=1)
    s = _dot_nt(qa_cmp, jnp.concatenate([kc_ref[...], ktag], axis=1))
    p_w = jnp.exp2(s_w - jnp.max(s_w, axis=-1, keepdims=True))
    ov_w = _dot(p_w.astype(BF16), vwp_ref[kr, :])
    o_win = ov_w[:, :dh] / ov_w[:, dh:]
    e = jnp.exp2(s - jnp.max(s, axis=-1, keepdims=True))
    vc_aug = jnp.concatenate([vc_ref[...], jnp.ones((n_cmp, dh), BF16)], axis=1)
    ov = _dot(e.astype(BF16), vc_aug)
    t_row = q0 + (lax.broadcasted_iota(jnp.int32, (rows, dh), 0) & (tq - 1))
    inv = jnp.where(t_row >= CMP_BLOCK - 1, 1.0 / jnp.maximum(ov[:, dh:], TINY), 0.0)
    o_cmp = ov[:, :dh] * inv
    p = e * jnp.concatenate([inv] * (n_cmp // dh), axis=1)
    p_sum = p[0:tq]
    for r in range(1, rep):
        p_sum = p_sum + p[r * tq:(r + 1) * tq]
    imp_t = _dot_nt(aggt_ref[...], p_sum.astype(BF16))

    jidx = lax.broadcasted_iota(jnp.int32, (n_sel, tq), 0)
    t_l = q0 + lax.broadcasted_iota(jnp.int32, (n_sel, tq), 1)
    cur = t_l // SLC_BLOCK
    forced = (jidx == 0) | (jidx == cur) | (jidx == cur - 1)
    causal = jidx * SLC_BLOCK <= t_l
    score = jnp.where(forced, FORCE_SCORE, jnp.where(causal, imp_t, -FORCE_SCORE))
    sc_ref[...] = score

    n_blk = (q0 + tq - 1) // SLC_BLOCK + 1
    n_it = (n_blk + RANK_UNROLL - 1) // RANK_UNROLL

    def rank_body(it, rank):
        for u in range(RANK_UNROLL):
            i = it * RANK_UNROLL + u
            row = sc_ref[pl.ds(i, 1), :]
            ge = jnp.where(row >= score, 1.0, 0.0)
            gt = jnp.where(row > score, 1.0, 0.0)
            rank = rank + jnp.where(jidx > i, ge, gt)
        return rank

    rank = lax.fori_loop(0, n_it, rank_body, jnp.zeros((n_sel, tq), F32))
    selneg_t = jnp.where(rank < n_topk, 0.0, NEG_INF)
    if n_sel < dh:
        selneg_t = jnp.concatenate([selneg_t, jnp.zeros((dh - n_sel, tq), F32)], axis=0)
    selneg = _dot_nt(eye_ref[...], selneg_t.astype(BF16)).astype(BF16)
    qa_sel = jnp.concatenate([qs, jnp.concatenate([selneg] * rep, axis=0)], axis=1)

    def qk_tile(kt):
        return _dot_nt(qa_sel, ksa_ref[pl.ds(pl.multiple_of(kt * tk, tk), tk), :])

    def consume(kt, buf, diagonal):
        k0 = pl.multiple_of(kt * tk, tk)
        s = buf[...]
        if diagonal:
            t_r = q0 + (lax.broadcasted_iota(jnp.int32, (rows, tk), 0) & (tq - 1))
            kpos = k0 + lax.broadcasted_iota(jnp.int32, (rows, tk), 1)
            s = jnp.where(kpos <= t_r, s, NEG_INF)
        m_prev = m_ref[...]
        m_new = jnp.maximum(m_prev, jnp.max(s, axis=-1, keepdims=True))
        p = jnp.exp2(s - jnp.concatenate([m_new] * (tk // 128), axis=1))
        alpha = jnp.exp2(m_prev - m_new)
        m_ref[...] = m_new
        acc_ref[...] = (jnp.concatenate([alpha, alpha], axis=1) * acc_ref[...]
                        + _dot(p.astype(BF16), vsa_ref[pl.ds(k0, tk), :]))

    m_ref[...] = jnp.full((rows, 128), NEG_INF, F32)
    acc_ref[...] = jnp.zeros((rows, 2 * dh), F32)
    s0_ref[...] = s0_ref[...] + jnp.concatenate([_dot(selneg, e0_ref[...])] * rep, axis=0)
    n_full = q0 // tk

    def sel_body(it, carry):
        kt = 2 * it
        s1_ref[...] = qk_tile(kt + 1)
        consume(kt, s0_ref, False)
        s0_ref[...] = qk_tile(kt + 2)
        consume(kt + 1, s1_ref, False)
        return carry

    lax.fori_loop(0, n_full // 2, sel_body, 0)

    @pl.when(n_full % 2 == 1)
    def _():
        s1_ref[...] = qk_tile(n_full)
        consume(n_full - 1, s0_ref, False)
        consume(n_full, s1_ref, True)

    @pl.when(n_full % 2 == 0)
    def _():
        consume(n_full, s0_ref, True)

    o_slc = acc_ref[:, 0:dh] / acc_ref[:, dh:2 * dh]

    gates = jax.nn.sigmoid(gl_ref[...])
    for r in range(rep):
        rs = slice(r * tq, (r + 1) * tq)
        o = (gates[:, 3 * r:3 * r + 1] * o_cmp[rs] + gates[:, 3 * r + 1:3 * r + 2] * o_slc[rs]
             + gates[:, 3 * r + 2:3 * r + 3] * o_win[rs])
        o_ref[:, r * dh:(r + 1) * dh] = o.astype(o_ref.dtype)


def _nsa_attention(p3, ps3, kc, vc, agg_t, eye, win_bias, cmp_qtag, expand0, q_gain, k_gain):
    b, seq, _ = p3.shape
    g, tq, dh = NSA_GROUPS, NSA_TQ, HEAD_DIM
    n_sel = seq // SLC_BLOCK
    assert n_sel <= dh and n_sel % 8 == 0 and seq % NSA_TK == 0
    nc = kc.shape[2]
    kvb = OFF_KV // dh

    def kv_spec(branch, is_v):
        base = kvb + (branch * 2 + is_v) * g
        return pl.BlockSpec((None, seq, dh), lambda i, j, k: (i, 0, base + j))

    qw = NSA_REP * dh
    return pl.pallas_call(
        functools.partial(_nsa_kernel, seq=seq),
        out_shape=jax.ShapeDtypeStruct((b, seq, NSA_Q_DIM), BF16),
        grid=(b, g, seq // tq),
        in_specs=[
            pl.BlockSpec((None, tq, qw), lambda i, j, k: (i, k, j)),
            pl.BlockSpec((None, tq, 128), lambda i, j, k: (i, k, j)),
            kv_spec(1, 0), kv_spec(1, 1), kv_spec(2, 0), kv_spec(2, 1),
            pl.BlockSpec((None, None, nc, dh), lambda i, j, k: (i, j, 0, 0)),
            pl.BlockSpec((None, None, nc, dh), lambda i, j, k: (i, j, 0, 0)),
            pl.BlockSpec((n_sel, nc), lambda i, j, k: (0, 0)),
            pl.BlockSpec((tq, tq), lambda i, j, k: (0, 0)),
            pl.BlockSpec((tq, WINDOW + tq), lambda i, j, k: (0, 0)),
            pl.BlockSpec((tq, dh), lambda i, j, k: (0, 0)),
            pl.BlockSpec((dh, NSA_TK), lambda i, j, k: (0, 0)),
            pl.BlockSpec((1, dh), lambda i, j, k: (0, 0)),
            pl.BlockSpec((N_BRANCH, dh), lambda i, j, k: (0, 0)),
        ],
        out_specs=pl.BlockSpec((None, tq, qw), lambda i, j, k: (i, k, j)),
        scratch_shapes=[
            pltpu.VMEM((seq, 2 * dh), BF16),
            pltpu.VMEM((seq + WINDOW, 2 * dh), BF16),
            pltpu.VMEM((seq, 2 * dh), BF16),
            pltpu.VMEM((seq + WINDOW, 2 * dh), BF16),
            pltpu.VMEM((n_sel, tq), F32),
            pltpu.VMEM((NSA_REP * tq, NSA_TK), F32),
            pltpu.VMEM((NSA_REP * tq, NSA_TK), F32),
            pltpu.VMEM((NSA_REP * tq, 128), F32),
            pltpu.VMEM((NSA_REP * tq, 2 * dh), F32),
        ],
        compiler_params=_cparams(("parallel", "parallel", "arbitrary")),
        name="nsa_attention",
    )(p3, ps3, p3, p3, p3, p3, kc, vc, agg_t, eye, win_bias, cmp_qtag, expand0, q_gain, k_gain)


GDN_TS = 256
GDN_INV = 128
HALO = 8


def _gdn_kernel(q_ref, k_ref, v_ref, z_ref, ab_ref, cw_ref, alog_ref, dtb_ref, og_ref, sh_ref,
                o_ref, xe_ref, st_ref):
    ts, hb, dh, ch = GDN_TS, GDN_HB, HEAD_DIM, GDN_CHUNK
    ti = pl.program_id(2)

    n_inv = ts // GDN_INV
    n_ch = ts // ch

    @pl.when(ti == 0)
    def _():
        xe_ref[...] = jnp.zeros_like(xe_ref)
        st_ref[...] = jnp.zeros_like(st_ref)

    row8 = lax.broadcasted_iota(jnp.int32, (HALO, hb * dh), 0)
    conv = []
    for part, ref in enumerate((q_ref, k_ref, v_ref)):
        xb = ref[...]
        xf = xb.astype(F32)
        prev = xe_ref[part]
        acc = cw_ref[part, GDN_CONV - 1:GDN_CONV, :] * xf
        corr = jnp.zeros((HALO, hb * dh), F32)
        for d in range(1, GDN_CONV):
            wd = cw_ref[part, GDN_CONV - 1 - d:GDN_CONV - d, :]
            acc = acc + wd * _dot(sh_ref[d - 1], xb)
            corr = corr + wd * jnp.where(row8 < d, pltpu.roll(prev, d, 0), 0.0)
        acc = jnp.concatenate([acc[0:HALO] + corr, acc[HALO:]], axis=0)
        conv.append(acc * jax.nn.sigmoid(acc))
        xe_ref[part] = xf[ts - HALO:ts]

    ab = ab_ref[...]
    xg = ab + dtb_ref[...]
    softplus = jnp.maximum(xg, 0.0) + jnp.log(1.0 + jnp.exp(-jnp.abs(xg)))
    g_cum = -jnp.exp(alog_ref[...]) * softplus
    beta = jax.nn.sigmoid(ab)
    row_in_chunk = lax.broadcasted_iota(jnp.int32, (ts, 128), 0) & (ch - 1)
    step = 1
    while step < ch:
        g_cum = g_cum + jnp.where(row_in_chunk >= step, pltpu.roll(g_cum, step, 0), 0.0)
        step *= 2
    g_end = jnp.concatenate(
        [jnp.broadcast_to(g_cum[(c + 1) * ch - 1:(c + 1) * ch, :], (ch, 128)) for c in range(n_ch)], axis=0)
    e_cum = jnp.exp(g_cum)
    e_rel = jnp.exp(g_end - g_cum)
    e_end = jnp.exp(g_end)
    g_cum_t = jnp.transpose(g_cum)

    ri = lax.broadcasted_iota(jnp.int32, (ts, ts), 0)
    ci = lax.broadcasted_iota(jnp.int32, (ts, ts), 1)
    same = (ri // ch) == (ci // ch)
    tril = same & (ri >= ci)
    strict = same & (ri > ci)
    ri_b = lax.broadcasted_iota(jnp.int32, (GDN_INV, GDN_INV), 0)
    ci_b = lax.broadcasted_iota(jnp.int32, (GDN_INV, GDN_INV), 1)
    ident = jnp.where(ri_b == ci_b, 1.0, 0.0)

    heads = []
    for hh in range(hb):
        cs = slice(hh * dh, (hh + 1) * dh)
        qh, kh, vh = conv[0][:, cs], conv[1][:, cs], conv[2][:, cs]
        qh = qh * lax.rsqrt(jnp.sum(qh * qh, axis=-1, keepdims=True) + EPS) * (dh ** -0.5)
        kh = kh * lax.rsqrt(jnp.sum(kh * kh, axis=-1, keepdims=True) + EPS)
        bt = beta[:, hb + hh:hb + hh + 1]
        diff = g_cum[:, hh:hh + 1] - g_cum_t[hh:hh + 1, :]
        e = jnp.exp(jnp.minimum(diff, 0.0))
        kb = kh * bt
        k_bf = kh.astype(BF16)
        a_mat = _dot_nt(kb.astype(BF16), k_bf) * jnp.where(strict, e, 0.0)
        qk = (_dot_nt(qh.astype(BF16), k_bf) * jnp.where(tril, e, 0.0)).astype(BF16)
        heads.append(dict(
            a=a_mat, qk=qk,
            vb=(vh * bt).astype(BF16),
            kbg=(kb * e_cum[:, hh:hh + 1]).astype(BF16),
            qg=qh * e_cum[:, hh:hh + 1],
        ))
        heads[-1]["kd"] = (kh * e_rel[:, hh:hh + 1]).astype(BF16)

    probs = []
    for hh in range(hb):
        for blk in range(n_inv):
            bs = slice(blk * GDN_INV, (blk + 1) * GDN_INV)
            pw = -heads[hh]["a"][bs, bs]
            probs.append([pw, ident + pw])
    for _ in range(5):
        for pr in probs:
            pw_bf = pr[0].astype(BF16)
            pr[0] = _dot(pw_bf, pw_bf)
        for pr in probs:
            pr[1] = pr[1] + _dot(pr[1].astype(BF16), pr[0].astype(BF16))
    for hh in range(hb):
        u_parts, w_parts = [], []
        for blk in range(n_inv):
            bs = slice(blk * GDN_INV, (blk + 1) * GDN_INV)
            t_bf = probs[hh * n_inv + blk][1].astype(BF16)
            u_parts.append(_dot(t_bf, heads[hh]["vb"][bs]))
            w_parts.append(_dot(t_bf, heads[hh]["kbg"][bs]))
        heads[hh]["u"] = jnp.concatenate(u_parts, axis=0).astype(BF16)
        heads[hh]["w"] = jnp.concatenate(w_parts, axis=0).astype(BF16)
    for hd in heads:
        hd["qkw"] = _dot(hd["qk"], hd["w"])
        hd["qku"] = _dot(hd["qk"], hd["u"])
        hd["mw"] = [_dot_tn(hd["kd"][c * ch:(c + 1) * ch],
                            jnp.concatenate([hd["w"][c * ch:(c + 1) * ch], hd["u"][c * ch:(c + 1) * ch]], axis=1))
                    for c in range(n_ch)]
    for hd in heads:
        hd["qp"] = (hd["qg"] - hd["qkw"]).astype(BF16)
        hd["mc"] = [mw[:, :dh].astype(BF16) for mw in hd["mw"]]
        hd["bc"] = [mw[:, dh:] for mw in hd["mw"]]

    states = [st_ref[hh] for hh in range(hb)]
    outs = [[] for _ in range(hb)]
    for c in range(n_ch):
        rs = slice(c * ch, (c + 1) * ch)
        for hh in range(hb):
            hd = heads[hh]
            r = _dot(jnp.concatenate([hd["qp"][rs], hd["mc"][c]], axis=0), states[hh].astype(BF16))
            outs[hh].append(r[:ch] + hd["qku"][rs])
            states[hh] = states[hh] * e_end[c * ch:c * ch + 1, hh:hh + 1] - r[ch:] + hd["bc"][c]

    for hh in range(hb):
        cs = slice(hh * dh, (hh + 1) * dh)
        st_ref[hh] = states[hh]
        o = jnp.concatenate(outs[hh], axis=0)
        o = o * lax.rsqrt(jnp.mean(o * o, axis=-1, keepdims=True) + EPS) * og_ref[...]
        z = z_ref[:, cs].astype(F32)
        o_ref[:, cs] = (o * (z * jax.nn.sigmoid(z))).astype(o_ref.dtype)


def _gdn(p3, ps3, conv_w3, alog_l, dtb_l, out_gain, shifts):
    b, seq, _ = p3.shape
    ts, hb, dh = GDN_TS, GDN_HB, HEAD_DIM
    wb = hb * dh
    qb, zb = OFF_GQKV // wb, OFF_Z // wb
    gw = GDN_DIM // wb
    return pl.pallas_call(
        _gdn_kernel,
        out_shape=jax.ShapeDtypeStruct((b, seq, GDN_DIM), BF16),
        grid=(b, GDN_NHG, seq // ts),
        in_specs=[
            pl.BlockSpec((None, ts, wb), lambda i, j, k: (i, k, qb + j)),
            pl.BlockSpec((None, ts, wb), lambda i, j, k: (i, k, qb + gw + j)),
            pl.BlockSpec((None, ts, wb), lambda i, j, k: (i, k, qb + 2 * gw + j)),
            pl.BlockSpec((None, ts, wb), lambda i, j, k: (i, k, zb + j)),
            pl.BlockSpec((None, ts, 128), lambda i, j, k: (i, k, NSA_GROUPS + j)),
            pl.BlockSpec((3, GDN_CONV, wb), lambda i, j, k: (0, 0, j)),
            pl.BlockSpec((None, 1, 128), lambda i, j, k: (j, 0, 0)),
            pl.BlockSpec((None, 1, 128), lambda i, j, k: (j, 0, 0)),
            pl.BlockSpec((1, dh), lambda i, j, k: (0, 0)),
            pl.BlockSpec((GDN_CONV - 1, ts, ts), lambda i, j, k: (0, 0, 0)),
        ],
        out_specs=pl.BlockSpec((None, ts, wb), lambda i, j, k: (i, k, j)),
        scratch_shapes=[
            pltpu.VMEM((3, HALO, wb), F32),
            pltpu.VMEM((hb, dh, dh), F32),
        ],
        compiler_params=_cparams(("parallel", "parallel", "arbitrary")),
        name="gdn",
    )(p3, p3, p3, p3, ps3, conv_w3, alog_l, dtb_l, out_gain, shifts)


def _merge_kernel(oa_ref, ob_ref, wa_ref, wb_ref, ma_ref, mb_ref, o_ref):
    ya = _dot(oa_ref[...], wa_ref[...])
    yb = _dot(ob_ref[...], wb_ref[...])
    mix = jax.nn.sigmoid(ma_ref[...].astype(F32)) * ya + jax.nn.sigmoid(mb_ref[...].astype(F32)) * yb
    o_ref[...] = mix.astype(o_ref.dtype)


def _merge(oa, ob, wa, wb, p2, bm=512, bn=1024):
    t, d = oa.shape
    n = wa.shape[1]
    ma_b, mb_b = OFF_MA // bn, OFF_MB // bn
    return pl.pallas_call(
        _merge_kernel,
        out_shape=jax.ShapeDtypeStruct((t, n), BF16),
        grid=(n // bn, t // bm),
        in_specs=[
            pl.BlockSpec((bm, d), lambda j, i: (i, 0)),
            pl.BlockSpec((bm, d), lambda j, i: (i, 0)),
            pl.BlockSpec((d, bn), lambda j, i: (0, j)),
            pl.BlockSpec((d, bn), lambda j, i: (0, j)),
            pl.BlockSpec((bm, bn), lambda j, i: (i, ma_b + j)),
            pl.BlockSpec((bm, bn), lambda j, i: (i, mb_b + j)),
        ],
        out_specs=pl.BlockSpec((bm, bn), lambda j, i: (i, j)),
        compiler_params=_cparams(("parallel", "parallel")),
        name="merge",
    )(oa, ob, wa, wb, p2, p2)


def _resid_matmul_kernel(a_ref, w_ref, r_ref, o_ref):
    o_ref[...] = r_ref[...] + _dot(a_ref[...], w_ref[...])


def _resid_matmul(a, w, resid, bm, bn):
    t, k = a.shape
    n = w.shape[1]
    return pl.pallas_call(
        _resid_matmul_kernel,
        out_shape=jax.ShapeDtypeStruct((t, n), F32),
        grid=(n // bn, t // bm),
        in_specs=[
            pl.BlockSpec((bm, k), lambda j, i: (i, 0)),
            pl.BlockSpec((k, bn), lambda j, i: (0, j)),
            pl.BlockSpec((bm, bn), lambda j, i: (i, j)),
        ],
        out_specs=pl.BlockSpec((bm, bn), lambda j, i: (i, j)),
        compiler_params=_cparams(("parallel", "parallel")),
        name="resid_matmul",
    )(a, w, resid)


def _ffn_fused_kernel(x_ref, g_ref, wg_ref, wu_ref, wd_ref, o_ref, hn_ref):
    @pl.when(pl.program_id(1) == 0)
    def _():
        x = x_ref[...]
        hn_ref[...] = (x * lax.rsqrt(jnp.mean(x * x, axis=-1, keepdims=True) + EPS) * g_ref[...]).astype(BF16)
        o_ref[...] = x

    hn = hn_ref[...]
    gate = _dot(hn, wg_ref[...])
    up = _dot(hn, wu_ref[...])
    act = (gate * jax.nn.sigmoid(gate) * up).astype(BF16)
    o_ref[...] += _dot(act, wd_ref[...])


def _ffn_fused(x1, gain, wg, wu, wd, bm=512, bn=512):
    t, d = x1.shape
    n = wg.shape[1]
    return pl.pallas_call(
        _ffn_fused_kernel,
        out_shape=jax.ShapeDtypeStruct((t, d), F32),
        grid=(t // bm, n // bn),
        in_specs=[
            pl.BlockSpec((bm, d), lambda i, j: (i, 0)),
            pl.BlockSpec((1, d), lambda i, j: (0, 0)),
            pl.BlockSpec((d, bn), lambda i, j: (0, j)),
            pl.BlockSpec((d, bn), lambda i, j: (0, j)),
            pl.BlockSpec((bn, d), lambda i, j: (j, 0)),
        ],
        out_specs=pl.BlockSpec((bm, d), lambda i, j: (i, 0)),
        scratch_shapes=[pltpu.VMEM((bm, d), BF16)],
        compiler_params=_cparams(("parallel", "arbitrary")),
        name="ffn_fused",
    )(x1, gain, wg, wu, wd)


def _pack_in_weights(w):
    w_t = jnp.transpose(w).astype(BF16)
    off_gate = OFF_GQKV
    off_a = OFF_MA + SKIP_GATE
    gate_t = w_t[off_gate:off_gate + SKIP_GATE]
    a_t = w_t[off_a:off_a + GDN_HEADS]
    b_t = w_t[off_a + GDN_HEADS:off_a + SKIP_AB]
    small = []
    per_group = NSA_REP * N_BRANCH
    for g in range(NSA_GROUPS):
        small.append(jnp.pad(gate_t[g * per_group:(g + 1) * per_group], ((0, 128 - per_group), (0, 0))))
    for hg in range(GDN_NHG):
        hs = slice(hg * GDN_HB, (hg + 1) * GDN_HB)
        small.append(jnp.pad(jnp.concatenate([a_t[hs], b_t[hs]], axis=0), ((0, 128 - 2 * GDN_HB), (0, 0))))
    w_small_t = jnp.concatenate(small, axis=0)
    assert w_t.shape[0] == NP_BIG + SKIP_GATE + SKIP_AB and w_small_t.shape[0] == NP_SMALL
    return w_t, w_small_t


def _lane_rows(v):
    return jnp.pad(v.reshape(GDN_NHG, 1, GDN_HB).astype(F32), ((0, 0), (0, 0), (0, 128 - GDN_HB)))


def _nsa_constants(seq):
    n_cmp = (seq - CMP_BLOCK) // CMP_STRIDE + 1
    nc = seq // CMP_STRIDE
    n_sel = seq // SLC_BLOCK
    cmp_start = np.arange(n_cmp) * CMP_STRIDE
    sel_start = np.arange(n_sel) * SLC_BLOCK
    overlap = (np.minimum(cmp_start[:, None] + CMP_BLOCK, sel_start[None, :] + SLC_BLOCK)
               - np.maximum(cmp_start[:, None], sel_start[None, :]))
    agg = np.zeros((nc, n_sel), np.float32)
    agg[:n_cmp] = np.clip(overlap, 0, None) / CMP_BLOCK
    eye = np.eye(NSA_TQ, dtype=np.float32)
    i = np.arange(NSA_TQ)[:, None]
    c = np.arange(WINDOW + NSA_TQ)[None, :]
    win_bias = np.where((c > i) & (c <= i + WINDOW), 0.0, NEG_INF).astype(np.float32)
    lane = np.arange(HEAD_DIM)[None, :]
    n_part = NSA_TQ // CMP_STRIDE + 1
    hidden = ((lane < n_part) & (CMP_STRIDE * lane + CMP_STRIDE - 1 > i)) | (lane == n_part)
    cmp_qtag = np.where(hidden, NEG_INF, 0.0).astype(np.float32)
    expand0 = (np.arange(NSA_TK)[None, :] // SLC_BLOCK == np.arange(HEAD_DIM)[:, None]).astype(np.float32)
    return (jnp.asarray(agg.T, BF16), jnp.asarray(eye, BF16), jnp.asarray(win_bias), jnp.asarray(cmp_qtag, BF16),
            jnp.asarray(expand0, BF16))


def _gdn_constants():
    idx = np.arange(GDN_TS)
    shifts = np.stack([(idx[None, :] == idx[:, None] - d) for d in range(1, GDN_CONV)]).astype(np.float32)
    return jnp.asarray(shifts, BF16)


def _layer(x, attn_norm, w_in, nsa_q_norm, nsa_k_norm, cmp_pos, w_cmp, gdn_conv, gdn_a_log, gdn_dt_bias,
           gdn_out_norm, w_branch_a, w_branch_b, w_out, ffn_norm, w_gate, w_up, w_down):
    b, seq, d = x.shape
    t = b * seq
    x2 = x.reshape(t, d)

    w_t, w_small_t = _pack_in_weights(w_in)
    p2, ps2 = _in_proj(x2, attn_norm.reshape(1, d), w_t, w_small_t)
    p3 = p2.reshape(b, seq, NP_BIG)
    ps3 = ps2.reshape(b, seq, NP_SMALL)

    w_cmp2 = w_cmp.reshape(2, CMP_BLOCK * HEAD_DIM, HEAD_DIM).astype(BF16)
    pos2 = jnp.broadcast_to(cmp_pos.reshape(2, 1, CMP_BLOCK * HEAD_DIM), (2, 8, CMP_BLOCK * HEAD_DIM)).astype(BF16)
    kc, vc = _nsa_compress(p3, w_cmp2, pos2, nsa_k_norm[0:1])

    o_a = _nsa_attention(p3, ps3, kc, vc, *_nsa_constants(seq), nsa_q_norm.reshape(1, HEAD_DIM), nsa_k_norm)

    conv_w3 = gdn_conv.reshape(GDN_CONV, 3, GDN_DIM).transpose(1, 0, 2)
    o_b = _gdn(p3, ps3, conv_w3, _lane_rows(gdn_a_log), _lane_rows(gdn_dt_bias),
               gdn_out_norm.reshape(1, HEAD_DIM), _gdn_constants())

    mix = _merge(o_a.reshape(t, NSA_Q_DIM), o_b.reshape(t, GDN_DIM), w_branch_a.astype(BF16),
                 w_branch_b.astype(BF16), p2)
    x1 = _resid_matmul(mix, w_out.astype(BF16), x2, bm=1024, bn=1024)

    out = _ffn_fused(x1, ffn_norm.reshape(1, d), w_gate.astype(BF16), w_up.astype(BF16), w_down.astype(BF16))
    return out.reshape(b, seq, d)


def kernel(x, attn_norm, w_in, nsa_q_norm, nsa_k_norm, cmp_pos, w_cmp, gdn_conv, gdn_a_log, gdn_dt_bias,
           gdn_out_norm, w_branch_a, w_branch_b, w_out, ffn_norm, w_gate, w_up, w_down):
    for l in range(attn_norm.shape[0]):
        x = _layer(x, attn_norm[l], w_in[l], nsa_q_norm[l], nsa_k_norm[l], cmp_pos[l], w_cmp[l], gdn_conv[l],
                   gdn_a_log[l], gdn_dt_bias[l], gdn_out_norm[l], w_branch_a[l], w_branch_b[l], w_out[l],
                   ffn_norm[l], w_gate[l], w_up[l], w_down[l])
    return x
```
